```python
import math
import jax, jax.numpy as jnp
from jax import lax
import numpy as np

D_MODEL = 1024
BATCH = 16
SEQ = 2048
DEPTH = 1
DEC_BATCH = 32
DEC_SEQ = 64
PAST_LEN = 2048

CHUNK = 64
Q_BLOCK = 128
H_A = 8
DH_A = 64
H_B = 4
DH_B = 64
DV_B = 2 * DH_B
W_A = H_A * DH_A
W_B = H_B * DV_B
D_FF = 2816
CONV_W = 3
EPS = 1e-6
LAYER_IDX = 1
LAMBDA_INIT = 0.8 - 0.6 * math.exp(-0.3 * (LAYER_IDX - 1))
ALIBI_SLOPES = np.array([2.0 ** (-8.0 * (i + 1) / H_B) for i in range(H_B)], dtype=np.float32)
SPLIT_POINTS = [W_A, 2 * W_A, 3 * W_A, 3 * W_A + H_A,
                3 * W_A + H_A + W_B, 3 * W_A + H_A + 2 * W_B, 3 * W_A + H_A + 3 * W_B,
                3 * W_A + H_A + 3 * W_B + D_MODEL]
N_IN = 3 * W_A + H_A + 3 * W_B + 2 * D_MODEL

kernel_name = 'hybrid_fox_diff_convffn_stream_step'


def rms_norm(x, g):
    xf = x.astype(jnp.float32)
    y = xf * lax.rsqrt(jnp.mean(xf * xf, axis=-1, keepdims=True) + EPS)
    return (y * g.astype(jnp.float32)).astype(x.dtype)


def mixer_inputs(x, g_attn, w_in, b_f, qn_a, kn_a, qn_b, kn_b):
    B, T, _ = x.shape
    z = rms_norm(x, g_attn) @ w_in
    qa, ka, va, fa, qb, kb, vb, gate_a, gate_b = jnp.split(z, SPLIT_POINTS, axis=-1)
    qa = rms_norm(qa.reshape(B, T, H_A, DH_A), qn_a)
    ka = rms_norm(ka.reshape(B, T, H_A, DH_A), kn_a)
    va = va.reshape(B, T, H_A, DH_A)
    logf = jax.nn.log_sigmoid((fa + b_f).astype(jnp.float32))
    qb = rms_norm(qb.reshape(B, T, H_B, 2, DH_B), qn_b)
    kb = rms_norm(kb.reshape(B, T, H_B, 2, DH_B), kn_b)
    vb = vb.reshape(B, T, H_B, DV_B)
    return qa, ka, va, logf, qb, kb, vb, gate_a, gate_b


def fox_attend(q, cq, qpos, k, v, ck, kpos):
    s = jnp.einsum('bqhd,bshd->bhqs', q, k).astype(jnp.float32) * (DH_A ** -0.5)
    decay = jnp.swapaxes(cq, 1, 2)[..., :, None] - jnp.swapaxes(ck, 1, 2)[..., None, :]
    mask = kpos[None, :] <= qpos[:, None]
    s = jnp.where(mask, s + decay, -jnp.inf)
    p = jax.nn.softmax(s, axis=-1)
    return jnp.einsum('bhqs,bshd->bqhd', p.astype(v.dtype), v)


def diff_lambda(lq1, lk1, lq2, lk2):
    f32 = jnp.float32
    return (jnp.exp(jnp.sum(lq1.astype(f32) * lk1.astype(f32)))
            - jnp.exp(jnp.sum(lq2.astype(f32) * lk2.astype(f32))) + LAMBDA_INIT)


def diff_attend(q, qpos, k, v, kpos, lam):
    s = jnp.einsum('bqhmd,bshmd->bhmqs', q, k).astype(jnp.float32) * (DH_B ** -0.5)
    dist = jnp.abs(qpos[:, None] - kpos[None, :]).astype(jnp.float32)
    slopes = jnp.asarray(ALIBI_SLOPES)[:, None, None, None]
    mask = (kpos // CHUNK)[None, :] <= (qpos // CHUNK)[:, None]
    s = jnp.where(mask, s - slopes * dist, -jnp.inf)
    p = jax.nn.softmax(s, axis=-1)
    a = p[:, :, 0] - lam * p[:, :, 1]
    return jnp.einsum('bhqs,bshe->bqhe', a.astype(v.dtype), v)


def branch_merge(x, oa, ob, gate_a, gate_b, subln_b, w_oa, w_ob, w_out):
    B, T, _ = x.shape
    ob = rms_norm(ob, subln_b) * (1.0 - LAMBDA_INIT)
    ya = oa.reshape(B, T, W_A) @ w_oa
    yb = ob.reshape(B, T, W_B) @ w_ob
    m = jax.nn.sigmoid(gate_a) * ya + jax.nn.sigmoid(gate_b) * yb
    return x + m @ w_out


def conv_ffn(x, prev_u, g_ffn, w_up, conv_w, conv_b, w_down):
    T = x.shape[1]
    u = rms_norm(x, g_ffn) @ w_up
    up = jnp.concatenate([prev_u, u], axis=1)
    c = conv_w[0] * up[:, :T] + conv_w[1] * up[:, 1:T + 1] + conv_w[2] * up[:, 2:T + 2] + conv_b
    a, b = jnp.split(c, 2, axis=-1)
    out = (jax.nn.silu(a) * b) @ w_down
    return x + out, up[:, -(CONV_W - 1):]


def setup_inputs(seed: int = 0) -> dict:
    key = jax.random.key(seed)
    keys = jax.random.split(key, 28)
    def n(i, shape, scale):
        return scale * jax.random.normal(keys[i], shape, jnp.float32)
    return {
        'x_prompt': n(0, (BATCH, SEQ, D_MODEL), 1.0),
        'x_sample': n(1, (DEC_BATCH, DEC_SEQ, D_MODEL), 1.0),
        'cache_a_k': n(2, (DEC_BATCH, PAST_LEN, H_A, DH_A), 1.0),
        'cache_a_v': n(3, (DEC_BATCH, PAST_LEN, H_A, DH_A), 1.0),
        'cache_a_logf': jax.nn.log_sigmoid(2.0 + n(4, (DEC_BATCH, PAST_LEN, H_A), 1.0)),
        'cache_b_k': n(5, (DEC_BATCH, PAST_LEN, H_B, 2, DH_B), 1.0),
        'cache_b_v': n(6, (DEC_BATCH, PAST_LEN, H_B, DV_B), 1.0),
        'state_ffn_conv': n(7, (DEC_BATCH, CONV_W - 1, 2 * D_FF), 1.0),
        'g_attn': 1.0 + n(8, (D_MODEL,), 0.1),
        'w_in': n(9, (D_MODEL, N_IN), D_MODEL ** -0.5),
        'b_f': 2.0 + n(10, (H_A,), 0.5),
        'qn_a': 1.0 + n(11, (DH_A,), 0.1),
        'kn_a': 1.0 + n(12, (DH_A,), 0.1),
        'qn_b': 1.0 + n(13, (DH_B,), 0.1),
        'kn_b': 1.0 + n(14, (DH_B,), 0.1),
        'lambda_q1': n(15, (DH_B,), 0.1),
        'lambda_k1': n(16, (DH_B,), 0.1),
        'lambda_q2': n(17, (DH_B,), 0.1),
        'lambda_k2': n(18, (DH_B,), 0.1),
        'subln_b': 1.0 + n(19, (DV_B,), 0.1),
        'w_oa': n(20, (W_A, D_MODEL), W_A ** -0.5),
        'w_ob': n(21, (W_B, D_MODEL), W_B ** -0.5),
        'w_out': n(22, (D_MODEL, D_MODEL), D_MODEL ** -0.5),
        'g_ffn': 1.0 + n(23, (D_MODEL,), 0.1),
        'w_up': n(24, (D_MODEL, 2 * D_FF), D_MODEL ** -0.5),
        'conv_w': n(25, (CONV_W, 2 * D_FF), CONV_W ** -0.5),
        'conv_b': n(26, (2 * D_FF,), 0.01),
        'w_down': n(27, (D_FF, D_MODEL), D_FF ** -0.5),
    }


def reference(x_prompt, x_sample, cache_a_k, cache_a_v, cache_a_logf, cache_b_k, cache_b_v,
              state_ffn_conv, g_attn, w_in, b_f, qn_a, kn_a, qn_b, kn_b,
              lambda_q1, lambda_k1, lambda_q2, lambda_k2, subln_b, w_oa, w_ob, w_out,
              g_ffn, w_up, conv_w, conv_b, w_down):
    lam = diff_lambda(lambda_q1, lambda_k1, lambda_q2, lambda_k2)

    B, T, _ = x_prompt.shape
    qa, ka, va, logf, qb, kb, vb, ga, gb = mixer_inputs(x_prompt, g_attn, w_in, b_f, qn_a, kn_a, qn_b, kn_b)
    c = jnp.cumsum(logf, axis=1)
    pos = jnp.arange(T)
    nb = T // Q_BLOCK
    blk = lambda a: jnp.swapaxes(a.reshape((B, nb, Q_BLOCK) + a.shape[2:]), 0, 1)
    unblk = lambda a: jnp.swapaxes(a, 0, 1).reshape((B, T) + a.shape[3:])
    pos_blk = pos.reshape(nb, Q_BLOCK)
    oa = unblk(lax.map(lambda t: fox_attend(t[0], t[1], t[2], ka, va, c, pos), (blk(qa), blk(c), pos_blk)))
    ob = unblk(lax.map(lambda t: diff_attend(t[0], t[1], kb, vb, pos, lam), (blk(qb), pos_blk)))
    x1 = branch_merge(x_prompt, oa, ob, ga, gb, subln_b, w_oa, w_ob, w_out)
    y_prompt, p_conv = conv_ffn(x1, jnp.zeros((B, CONV_W - 1, 2 * D_FF), x1.dtype), g_ffn, w_up, conv_w, conv_b, w_down)

    Ts = x_sample.shape[1]
    P = cache_a_k.shape[1]
    qa_s, ka_s, va_s, logf_s, qb_s, kb_s, vb_s, ga_s, gb_s = mixer_inputs(x_sample, g_attn, w_in, b_f, qn_a, kn_a, qn_b, kn_b)
    qpos = P + jnp.arange(Ts)
    kpos = jnp.arange(P + Ts)
    ka_all = jnp.concatenate([cache_a_k.astype(ka_s.dtype), ka_s], axis=1)
    va_all = jnp.concatenate([cache_a_v.astype(va_s.dtype), va_s], axis=1)
    c_all = jnp.cumsum(jnp.concatenate([cache_a_logf.astype(jnp.float32), logf_s], axis=1), axis=1)
    kb_all = jnp.concatenate([cache_b_k.astype(kb_s.dtype), kb_s], axis=1)
    vb_all = jnp.concatenate([cache_b_v.astype(vb_s.dtype), vb_s], axis=1)
    oa_s = fox_attend(qa_s, c_all[:, P:], qpos, ka_all, va_all, c_all, kpos)
    ob_s = diff_attend(qb_s, qpos, kb_all, vb_all, kpos, lam)
    x1s = branch_merge(x_sample, oa_s, ob_s, ga_s, gb_s, subln_b, w_oa, w_ob, w_out)
    y_sample, s_conv = conv_ffn(x1s, state_ffn_conv.astype(x1s.dtype), g_ffn, w_up, conv_w, conv_b, w_down)

    return (y_prompt, y_sample,
            ka, va, logf.astype(x_prompt.dtype), kb, vb, p_conv,
            ka_s, va_s, logf_s.astype(x_sample.dtype), kb_s, vb_s, s_conv)
```

```python
import functools
import math

import jax
import jax.numpy as jnp
import numpy as np
from jax import lax
from jax.experimental import pallas as pl
from jax.experimental.pallas import tpu as pltpu

F32 = jnp.float32
BF16 = jnp.bfloat16

CHUNK = 64
H_A = 8
DH_A = 64
H_B = 4
DH_B = 64
DV_B = 128
W_A = H_A * DH_A
W_B = H_B * DV_B
EPS = 1e-6
LAMBDA_INIT = 0.8 - 0.6 * math.exp(-0.3 * 0)
ALIBI_SLOPES = tuple(2.0 ** (-8.0 * (i + 1) / H_B) for i in range(H_B))

LANES = 128
HEAD_SLOTS = 8
N_CUM_PIECES = 3
N_POS_PIECES = 2
NEG_BIG = -1e30
VMEM_LIMIT = 56 * 1024 * 1024


def _lane_iota(shape):
    return lax.broadcasted_iota(jnp.int32, shape, len(shape) - 1)


def _row_iota(shape):
    return lax.broadcasted_iota(jnp.int32, shape, len(shape) - 2)


def _split3(c):
    hi = c.astype(BF16).astype(F32)
    r1 = c - hi
    lo = r1.astype(BF16).astype(F32)
    lo2 = (r1 - lo).astype(BF16).astype(F32)
    return hi, lo, lo2


def _cumsum_pieces(lf, ltri_ref, base):
    hi, lo, lo2 = _split3(lf)
    ltri = ltri_ref[...]
    c = (jnp.dot(ltri, hi.astype(BF16), preferred_element_type=F32)
         + jnp.dot(ltri, lo.astype(BF16), preferred_element_type=F32)
         + jnp.dot(ltri, lo2.astype(BF16), preferred_element_type=F32)) + base
    chi, clo, clo2 = _split3(c)
    lane = _lane_iota(c.shape)
    pieces = jnp.where(lane < HEAD_SLOTS, chi, jnp.where(lane < 2 * HEAD_SLOTS, clo, clo2))
    return c, pieces.astype(BF16)


def _slot(z, j):
    pair = z[:, LANES * (j // 2):LANES * (j // 2 + 1)]
    if j % 2:
        pair = pltpu.roll(pair, 64, 1)
    return pair


def _pos_extra(pos, slope):
    v = pos.astype(F32) * slope
    hi = v.astype(BF16).astype(F32)
    lo = (v - hi).astype(BF16).astype(F32)
    lane = _lane_iota(v.shape)
    return jnp.where(lane == 64, hi, jnp.where(lane == 65, lo, 0.0))


def _store_slots(ref, tile, j, nb, rows):
    for bi in range(nb):
        ref[bi, j, :, :] = tile[bi * rows:(bi + 1) * rows, :]


def _emit_fox_keys(kaa_ref, kn, pieces, pc_ref, nb, rows):
    extra = jnp.dot(pieces, pc_ref[...], preferred_element_type=F32)
    lane = _lane_iota((kn.shape[0], LANES))
    for j in range(HEAD_SLOTS):
        tile = jnp.where(lane < 64, _slot(kn, j), extra[:, LANES * j:LANES * (j + 1)])
        _store_slots(kaa_ref, tile.astype(BF16), j, nb, rows)


def _emit_diff_keys(kba_ref, kn, pos, nb, rows):
    lane = _lane_iota((kn.shape[0], LANES))
    for h in range(H_B):
        extra = _pos_extra(pos, ALIBI_SLOPES[h])
        for m in range(2):
            j = 2 * h + m
            tile = jnp.where(lane < 64, _slot(kn, j), extra)
            _store_slots(kba_ref, tile.astype(BF16), j, nb, rows)


def _emit_queries(q_ref, qn, n_ones, nb, rows):
    lane = _lane_iota((qn.shape[0], LANES))
    ones = jnp.where(lane < 64 + n_ones, 1.0, 0.0)
    for j in range(HEAD_SLOTS):
        tile = jnp.where(lane < 64, _slot(qn, j), ones)
        _store_slots(q_ref, tile.astype(BF16), j, nb, rows)


def _cache_prep_kernel(ka_ref, lf_ref, kb_ref, ltri_ref, pc_ref,
                       kaa_ref, kba_ref, ctot_ref, carry_ref, *, tp):
    t = pl.program_id(1)

    @pl.when(t == 0)
    def _():
        carry_ref[...] = jnp.zeros_like(carry_ref)

    c, pieces = _cumsum_pieces(lf_ref[0], ltri_ref, carry_ref[...])
    carry_ref[...] = c[tp - 1:tp, :]
    ctot_ref[0] = c[tp - 1:tp, :]
    _emit_fox_keys(kaa_ref, ka_ref[0], pieces, pc_ref, 1, tp)
    pos = t * tp + _row_iota((tp, LANES))
    _emit_diff_keys(kba_ref, kb_ref[0], pos, 1, tp)


def _cache_prep(cache_a_k, lf3, cache_b_k, ltri, pc, tp):
    bx, p, _ = cache_a_k.shape
    grid = (bx, p // tp)
    const = lambda shape: pl.BlockSpec(shape, lambda b, t: (0,) * len(shape))
    return pl.pallas_call(
        functools.partial(_cache_prep_kernel, tp=tp),
        grid=grid,
        in_specs=[
            pl.BlockSpec((1, tp, W_A), lambda b, t: (b, t, 0)),
            pl.BlockSpec((1, tp, LANES), lambda b, t: (b, t, 0)),
            pl.BlockSpec((1, tp, W_B), lambda b, t: (b, t, 0)),
            const((tp, tp)),
            const((LANES, HEAD_SLOTS * LANES)),
        ],
        out_specs=[
            pl.BlockSpec((1, HEAD_SLOTS, tp, LANES), lambda b, t: (b, 0, t, 0)),
            pl.BlockSpec((1, HEAD_SLOTS, tp, LANES), lambda b, t: (b, 0, t, 0)),
            pl.BlockSpec((1, 1, LANES), lambda b, t: (b, 0, 0)),
        ],
        out_shape=[
            jax.ShapeDtypeStruct((bx, HEAD_SLOTS, p, LANES), BF16),
            jax.ShapeDtypeStruct((bx, HEAD_SLOTS, p, LANES), BF16),
            jax.ShapeDtypeStruct((bx, 1, LANES), F32),
        ],
        scratch_shapes=[pltpu.VMEM((1, LANES), F32)],
        compiler_params=pltpu.CompilerParams(
            dimension_semantics=("arbitrary", "arbitrary"), vmem_limit_bytes=VMEM_LIMIT),
        name="cache_prep",
    )(cache_a_k, lf3, cache_b_k, ltri, pc)


def _in_proj_kernel(x_ref, g_ref, wm_ref, wf_ref, bf_ref, qna_ref, kna_ref, qnb_ref, knb_ref,
                    bd_ref, ltri_ref, pc_ref, cinit_ref,
                    ka_ref, va_ref, logf_ref, kb_ref, vb_ref,
                    qaa_ref, kaa_ref, qba_ref, kba_ref, sig_ref, carry_ref,
                    *, tm, seq, pos_off):
    t = pl.program_id(0)
    carry_mode = seq >= tm
    nb = 1 if carry_mode else tm // seq
    rows = tm if carry_mode else seq
    tpb = max(1, seq // tm)

    x = x_ref[...]
    h = (x * lax.rsqrt(jnp.mean(x * x, axis=-1, keepdims=True) + EPS) * g_ref[...]).astype(BF16)

    def proj(lo, hi):
        return jnp.dot(h, wm_ref[:, lo:hi], preferred_element_type=F32)

    def head_norm(z, w_ref):
        ss = jnp.dot((z * z).astype(BF16), bd_ref[...], preferred_element_type=F32)
        return z * lax.rsqrt(ss * (1.0 / DH_A) + EPS) * w_ref[...]

    zf = jnp.dot(h, wf_ref[...], preferred_element_type=F32) + bf_ref[...]
    lf = jnp.minimum(zf, 0.0) - jnp.log1p(jnp.exp(-jnp.abs(zf)))
    lane = _lane_iota(lf.shape)
    lf = jnp.where(lane < N_CUM_PIECES * HEAD_SLOTS, lf, 0.0)
    logf_ref[...] = lf[:, :H_A]
    if carry_mode:
        @pl.when(t % tpb == 0)
        def _():
            carry_ref[...] = cinit_ref[0]
        base = carry_ref[...]
    else:
        base = cinit_ref[...]
    c, pieces = _cumsum_pieces(lf, ltri_ref, base)
    if carry_mode:
        carry_ref[...] = c[tm - 1:tm, :]

    row = t * tm + _row_iota((tm, LANES))
    pos = (row & (seq - 1)) + pos_off

    qn = head_norm(proj(0, W_A), qna_ref) * (DH_A ** -0.5)
    _emit_queries(qaa_ref, qn, N_CUM_PIECES, nb, rows)
    kn = head_norm(proj(W_A, 2 * W_A), kna_ref)
    ka_ref[...] = kn
    _emit_fox_keys(kaa_ref, kn, pieces, pc_ref, nb, rows)
    va_ref[...] = proj(2 * W_A, 3 * W_A)

    o = 3 * W_A
    qn = head_norm(proj(o, o + W_B), qnb_ref) * (DH_B ** -0.5)
    _emit_queries(qba_ref, qn, N_POS_PIECES, nb, rows)
    kn = head_norm(proj(o + W_B, o + 2 * W_B), knb_ref)
    kb_ref[...] = kn
    _emit_diff_keys(kba_ref, kn, pos, nb, rows)
    vb_ref[...] = proj(o + 2 * W_B, o + 3 * W_B)

    o = 3 * W_A + 3 * W_B
    d = (wm_ref.shape[1] - o) // 2
    for k in range(2):
        sig_ref[:, k * d:(k + 1) * d] = jax.nn.sigmoid(proj(o + k * d, o + (k + 1) * d)).astype(BF16)


def _in_proj(x2, seq, pos_off, cinit, prm, tm):
    n, d = x2.shape
    carry_mode = seq >= tm
    nb = 1 if carry_mode else tm // seq
    rows = tm if carry_mode else seq
    tpb = max(1, seq // tm)
    bx = n // seq
    assert n % tm == 0 and seq & (seq - 1) == 0
    grid = (n // tm,)
    const = lambda a: pl.BlockSpec(a.shape, lambda t: (0,) * a.ndim, pipeline_mode=pl.Buffered(1))
    rowblk = lambda w: pl.BlockSpec((tm, w), lambda t: (t, 0))
    if carry_mode:
        cinit_spec = pl.BlockSpec((1, 1, LANES), lambda t: (t // tpb, 0, 0))
        aug_spec = pl.BlockSpec((1, HEAD_SLOTS, tm, LANES), lambda t: (t // tpb, 0, t % tpb, 0))
    else:
        cinit_spec = rowblk(LANES)
        aug_spec = pl.BlockSpec((nb, HEAD_SLOTS, seq, LANES), lambda t: (t, 0, 0, 0))
    aug_shape = jax.ShapeDtypeStruct((bx, HEAD_SLOTS, seq, LANES), BF16)
    consts = [prm["g_attn"], prm["w_main"], prm["w_f"], prm["b_f"], prm["qn_a"], prm["kn_a"],
              prm["qn_b"], prm["kn_b"], prm["bd"], prm["ltri_in"], prm["pc"]]
    return pl.pallas_call(
        functools.partial(_in_proj_kernel, tm=tm, seq=seq, pos_off=pos_off),
        grid=grid,
        in_specs=[rowblk(d)] + [const(a) for a in consts] + [cinit_spec],
        out_specs=[rowblk(W_A), rowblk(W_A), rowblk(H_A), rowblk(W_B), rowblk(W_B),
                   aug_spec, aug_spec, aug_spec, aug_spec, rowblk(2 * d)],
        out_shape=[
            jax.ShapeDtypeStruct((n, W_A), F32), jax.ShapeDtypeStruct((n, W_A), F32),
            jax.ShapeDtypeStruct((n, H_A), F32),
            jax.ShapeDtypeStruct((n, W_B), F32), jax.ShapeDtypeStruct((n, W_B), F32),
            aug_shape, aug_shape, aug_shape, aug_shape,
            jax.ShapeDtypeStruct((n, 2 * d), BF16),
        ],
        scratch_shapes=[pltpu.VMEM((1, LANES), F32)],
        compiler_params=pltpu.CompilerParams(
            dimension_semantics=("arbitrary",), vmem_limit_bytes=VMEM_LIMIT),
        name="in_proj",
    )(x2, *consts, cinit)


def _softmax_step(q, k, v, carry, bias=None):
    m, l, acc = carry
    s = lax.dot_general(q, k, (((1,), (1,)), ((), ())), preferred_element_type=F32)
    if bias is not None:
        s = s + bias
    m_new = jnp.maximum(m, jnp.max(s, axis=-1, keepdims=True))
    alpha = jnp.exp(m - m_new)
    p = jnp.exp(s - m_new)
    l = alpha * l + jnp.sum(p, axis=-1, keepdims=True)
    acc = alpha * acc + jnp.dot(p.astype(BF16), v, preferred_element_type=F32)
    return m_new, l, acc


def _attend(q, kc_ref, vc_ref, kp_ref, vp_ref, slot, i, diag_bias, *, tq, n_past, tkp):
    carry = (jnp.full((tq, 1), NEG_BIG, F32), jnp.zeros((tq, 1), F32), jnp.zeros((tq, LANES), F32))
    if n_past:
        def past_body(j, carry):
            ks = pl.ds(pl.multiple_of(j * tkp, tkp), tkp)
            return _softmax_step(q, kp_ref[0, slot, ks, :], vp_ref[0, ks, :].astype(BF16), carry)
        carry = lax.fori_loop(0, n_past, past_body, carry)

    def cur_body(j, carry):
        ks = pl.ds(pl.multiple_of(j * tq, tq), tq)
        return _softmax_step(q, kc_ref[0, slot, ks, :], vc_ref[0, ks, :].astype(BF16), carry)
    carry = lax.fori_loop(0, i, cur_body, carry)

    ks = pl.ds(pl.multiple_of(i * tq, tq), tq)
    m, l, acc = _softmax_step(q, kc_ref[0, slot, ks, :], vc_ref[0, ks, :].astype(BF16), carry, diag_bias)
    return acc / l


def _fox_kernel(*refs, tq, n_past, tkp):
    if n_past:
        q_ref, kc_ref, vc_ref, kp_ref, vp_ref, o_ref = refs
    else:
        q_ref, kc_ref, vc_ref, o_ref = refs
        kp_ref = vp_ref = None
    i = pl.program_id(2)
    rowi = _row_iota((tq, tq))
    coli = _lane_iota((tq, tq))
    causal = jnp.where(coli <= rowi, 0.0, NEG_BIG)
    outs = [_attend(q_ref[0, s], kc_ref, vc_ref, kp_ref, vp_ref, s, i, causal,
                    tq=tq, n_past=n_past, tkp=tkp) for s in range(2)]
    lane = _lane_iota((tq, LANES))
    o_ref[0] = jnp.where(lane < 64, outs[0], outs[1]).astype(BF16)


def _diff_kernel(*refs, tq, n_past, tkp):
    if n_past:
        q_ref, kc_ref, vc_ref, kp_ref, vp_ref, lam_ref, sub_ref, o_ref = refs
    else:
        q_ref, kc_ref, vc_ref, lam_ref, sub_ref, o_ref = refs
        kp_ref = vp_ref = None
    hd = pl.program_id(1)
    i = pl.program_id(2)
    slope = jnp.float32(ALIBI_SLOPES[0])
    for k in range(1, H_B):
        slope = jnp.where(hd == k, jnp.float32(ALIBI_SLOPES[k]), slope)
    rowi = _row_iota((tq, tq))
    coli = _lane_iota((tq, tq))
    ahead = jnp.maximum(coli - rowi, 0).astype(F32)
    visible = (coli // CHUNK) <= (rowi // CHUNK)
    diag_bias = jnp.where(visible, -2.0 * slope * ahead, NEG_BIG)
    o1, o2 = [_attend(q_ref[0, s], kc_ref, vc_ref, kp_ref, vp_ref, s, i, diag_bias,
                      tq=tq, n_past=n_past, tkp=tkp) for s in range(2)]
    lq1, lk1, lq2, lk2 = (lam_ref[k:k + 1, :] for k in range(4))
    lam = (jnp.exp(jnp.sum(lq1 * lk1, axis=-1, keepdims=True))
           - jnp.exp(jnp.sum(lq2 * lk2, axis=-1, keepdims=True)) + LAMBDA_INIT)
    o = o1 - lam * o2
    o = o * lax.rsqrt(jnp.mean(o * o, axis=-1, keepdims=True) + EPS) * sub_ref[...] * (1.0 - LAMBDA_INIT)
    o_ref[0] = o.astype(BF16)


def _attention(kind, q_aug, k_aug, v_cur, past, extras, tq, tkp):
    bx, _, seq, _ = q_aug.shape
    groups = HEAD_SLOTS // 2
    grid = (bx, groups, seq // tq)
    n_past = 0
    in_specs = [
        pl.BlockSpec((1, 2, tq, LANES), lambda b, g, i: (b, g, i, 0)),
        pl.BlockSpec((1, 2, seq, LANES), lambda b, g, i: (b, g, 0, 0)),
        pl.BlockSpec((1, seq, LANES), lambda b, g, i: (b, 0, g)),
    ]
    args = [q_aug, k_aug, v_cur]
    if past is not None:
        kp, vp = past
        plen = kp.shape[2]
        n_past = plen // tkp
        in_specs += [
            pl.BlockSpec((1, 2, plen, LANES), lambda b, g, i: (b, g, 0, 0)),
            pl.BlockSpec((1, plen, LANES), lambda b, g, i: (b, 0, g)),
        ]
        args += [kp, vp]
    for a in extras:
        in_specs.append(pl.BlockSpec(a.shape, lambda b, g, i: (0,) * a.ndim))
        args.append(a)
    body = _fox_kernel if kind == "fox" else _diff_kernel
    return pl.pallas_call(
        functools.partial(body, tq=tq, n_past=n_past, tkp=tkp),
        grid=grid,
        in_specs=in_specs,
        out_specs=pl.BlockSpec((1, tq, LANES), lambda b, g, i: (b, i, g)),
        out_shape=jax.ShapeDtypeStruct((bx, seq, groups * LANES), BF16),
        compiler_params=pltpu.CompilerParams(
            dimension_semantics=("arbitrary", "arbitrary", "arbitrary"), vmem_limit_bytes=VMEM_LIMIT),
        name=kind + "_attn",
    )(*args)


FF_CHUNK = 256


def _merge_ffn_kernel(x_ref, oa_ref, ob_ref, sig_ref, woa_ref, wob_ref, wout_ref, gffn_ref,
                      wup_ref, cw_ref, cb_ref, wdown_ref, st_ref,
                      y_ref, ns_ref, carry_ref, *, tm, seq, d_ff):
    t = pl.program_id(0)
    carry_mode = seq >= tm
    nb = 1 if carry_mode else tm // seq
    rows = tm if carry_mode else seq
    tpb = max(1, seq // tm)
    d = x_ref.shape[1]

    ya = jnp.dot(oa_ref[...], woa_ref[...], preferred_element_type=F32)
    yb = jnp.dot(ob_ref[...], wob_ref[...], preferred_element_type=F32)
    m = sig_ref[:, :d].astype(F32) * ya + sig_ref[:, d:].astype(F32) * yb
    x1 = x_ref[...] + jnp.dot(m.astype(BF16), wout_ref[...], preferred_element_type=F32)
    h = (x1 * lax.rsqrt(jnp.mean(x1 * x1, axis=-1, keepdims=True) + EPS) * gffn_ref[...]).astype(BF16)

    if carry_mode:
        @pl.when(t % tpb == 0)
        def _():
            carry_ref[0:2, :] = st_ref[0]

    rowi = _row_iota((rows, FF_CHUNK))

    def conv(u, c0):
        cols = slice(c0, c0 + FF_CHUNK)
        w0, w1, w2 = (cw_ref[k:k + 1, cols] for k in range(3))
        outs = []
        for bi in range(nb):
            useg = u[bi * rows:(bi + 1) * rows, :]
            if carry_mode:
                p2, p1 = carry_ref[0:1, cols], carry_ref[1:2, cols]
            else:
                p2, p1 = st_ref[bi, 0:1, cols], st_ref[bi, 1:2, cols]
            u1 = jnp.where(rowi == 0, p1, pltpu.roll(useg, 1, 0))
            u2 = jnp.where(rowi == 0, p2, jnp.where(rowi == 1, p1, pltpu.roll(useg, 2, 0)))
            outs.append(w0 * u2 + w1 * u1 + w2 * useg + cb_ref[:, cols])
            ns_ref[bi, :, cols] = useg[rows - 2:rows, :]
        if carry_mode:
            carry_ref[0:2, cols] = u[tm - 2:tm, :]
        return outs[0] if nb == 1 else jnp.concatenate(outs, axis=0)

    acc = x1
    for c0 in range(0, d_ff, FF_CHUNK):
        ua = jnp.dot(h, wup_ref[:, c0:c0 + FF_CHUNK], preferred_element_type=F32)
        ub = jnp.dot(h, wup_ref[:, d_ff + c0:d_ff + c0 + FF_CHUNK], preferred_element_type=F32)
        a = conv(ua, c0)
        b = conv(ub, d_ff + c0)
        act = (a * jax.nn.sigmoid(a) * b).astype(BF16)
        acc = acc + jnp.dot(act, wdown_ref[c0:c0 + FF_CHUNK, :], preferred_element_type=F32)
    y_ref[...] = acc


def _merge_ffn(x2, oa, ob, sig, state, seq, prm, tm):
    n, d = x2.shape
    d_ff = prm["w_down"].shape[0]
    assert d_ff % FF_CHUNK == 0 and seq >= 2
    carry_mode = seq >= tm
    nb = 1 if carry_mode else tm // seq
    tpb = max(1, seq // tm)
    bx = n // seq
    grid = (n // tm,)
    const = lambda a: pl.BlockSpec(a.shape, lambda t: (0,) * a.ndim, pipeline_mode=pl.Buffered(1))
    rowblk = lambda w: pl.BlockSpec((tm, w), lambda t: (t, 0))
    if carry_mode:
        st_spec = pl.BlockSpec((1, 2, 2 * d_ff), lambda t: (t // tpb, 0, 0))
    else:
        st_spec = pl.BlockSpec((nb, 2, 2 * d_ff), lambda t: (t, 0, 0))
    consts1 = [prm["w_oa"], prm["w_ob"], prm["w_out"], prm["g_ffn"], prm["w_up"], prm["conv_w"],
               prm["conv_b"], prm["w_down"]]
    return pl.pallas_call(
        functools.partial(_merge_ffn_kernel, tm=tm, seq=seq, d_ff=d_ff),
        grid=grid,
        in_specs=[rowblk(d), rowblk(W_A), rowblk(W_B), rowblk(2 * d)] + [const(a) for a in consts1] + [st_spec],
        out_specs=[rowblk(d), st_spec],
        out_shape=[jax.ShapeDtypeStruct((n, d), F32), jax.ShapeDtypeStruct((bx, 2, 2 * d_ff), F32)],
        scratch_shapes=[pltpu.VMEM((8, 2 * d_ff), F32)],
        compiler_params=pltpu.CompilerParams(
            dimension_semantics=("arbitrary",), vmem_limit_bytes=VMEM_LIMIT),
        name="merge_ffn",
    )(x2, oa, ob, sig, *consts1, state)


def _tri(tm, seg):
    i = np.arange(tm)
    return jnp.asarray((i[None, :] <= i[:, None]) & (i[None, :] // seg == i[:, None] // seg), BF16)


def _piece_placer():
    p = np.zeros((LANES, HEAD_SLOTS * LANES), np.float32)
    for k in range(N_CUM_PIECES):
        for h in range(HEAD_SLOTS):
            p[k * HEAD_SLOTS + h, h * LANES + 64 + k] = -1.0
    return jnp.asarray(p, BF16)


def _block_diag_ones(width, blk):
    i = np.arange(width)
    return jnp.asarray(i[:, None] // blk == i[None, :] // blk, BF16)


def _rep3(a):
    pad = jnp.zeros(a.shape[:-1] + (LANES - N_CUM_PIECES * HEAD_SLOTS,), a.dtype)
    return jnp.concatenate([a] * N_CUM_PIECES + [pad], axis=-1)


def _layer(x, seq_pos_off, cinit, past_a, past_b, state, prm, lam_pack, tm, tq, tkp):
    bx, seq, d = x.shape
    x2 = x.reshape(bx * seq, d)
    ka, va, logf, kb, vb, qaa, kaa, qba, kba, sig = _in_proj(x2, seq, seq_pos_off, cinit, prm, tm)
    va3 = va.reshape(bx, seq, W_A)
    vb3 = vb.reshape(bx, seq, W_B)
    oa = _attention("fox", qaa, kaa, va3, past_a, (), tq, tkp)
    ob = _attention("diff", qba, kba, vb3, past_b, (lam_pack, prm["subln_b"]), tq, tkp)
    y, new_state = _merge_ffn(x2, oa.reshape(bx * seq, W_A), ob.reshape(bx * seq, W_B), sig, state, seq, prm, tm)
    return (y.reshape(bx, seq, d), ka.reshape(bx, seq, H_A, DH_A), va.reshape(bx, seq, H_A, DH_A),
            logf.reshape(bx, seq, H_A), kb.reshape(bx, seq, H_B, 2, DH_B), vb.reshape(bx, seq, H_B, DV_B),
            new_state)


def kernel(x_prompt, x_sample, cache_a_k, cache_a_v, cache_a_logf, cache_b_k, cache_b_v, state_ffn_conv,
           g_attn, w_in, b_f, qn_a, kn_a, qn_b, kn_b, lambda_q1, lambda_k1, lambda_q2, lambda_k2,
           subln_b, w_oa, w_ob, w_out, g_ffn, w_up, conv_w, conv_b, w_down):
    bp, tp_, d = x_prompt.shape
    bs, ts, _ = x_sample.shape
    plen = cache_a_k.shape[1]
    d_ff = w_down.shape[0]
    tm = 512

    f0, f1 = 3 * W_A, 3 * W_A + H_A
    prm = {
        "g_attn": g_attn.reshape(1, d),
        "w_main": jnp.concatenate([w_in[:, :f0], w_in[:, f1:]], axis=1).astype(BF16),
        "w_f": _rep3(w_in[:, f0:f1]).astype(BF16),
        "b_f": _rep3(b_f.reshape(1, H_A)),
        "qn_a": jnp.tile(qn_a, H_A).reshape(1, W_A), "kn_a": jnp.tile(kn_a, H_A).reshape(1, W_A),
        "qn_b": jnp.tile(qn_b, 2 * H_B).reshape(1, W_B), "kn_b": jnp.tile(kn_b, 2 * H_B).reshape(1, W_B),
        "bd": _block_diag_ones(W_A, DH_A),
        "pc": _piece_placer(),
        "subln_b": subln_b.reshape(1, DV_B),
        "w_oa": w_oa.astype(BF16), "w_ob": w_ob.astype(BF16), "w_out": w_out.astype(BF16),
        "g_ffn": g_ffn.reshape(1, d), "w_up": w_up.astype(BF16), "conv_w": conv_w,
        "conv_b": conv_b.reshape(1, 2 * d_ff), "w_down": w_down.astype(BF16),
    }
    lam_pack = jnp.stack([lambda_q1, lambda_k1, lambda_q2, lambda_k2])

    prm_p = dict(prm, ltri_in=_tri(tm, min(tm, tp_)))
    zeros_c = jnp.zeros((bp, 1, LANES), F32)
    zeros_state = jnp.zeros((bp, 2, 2 * d_ff), F32)
    (y_p, ka_p, va_p, lf_p, kb_p, vb_p, st_p) = _layer(
        x_prompt, 0, zeros_c, None, None, zeros_state, prm_p, lam_pack, tm, 256, 0)

    tpc = 512
    kpa, kpb, ctot = _cache_prep(
        cache_a_k.reshape(bs, plen, W_A), _rep3(cache_a_logf), cache_b_k.reshape(bs, plen, W_B),
        _tri(tpc, tpc), prm["pc"], tpc)
    prm_s = dict(prm, ltri_in=_tri(tm, min(tm, ts)))
    cinit_rows = jnp.repeat(ctot[:, 0, :], ts, axis=0)
    (y_s, ka_s, va_s, lf_s, kb_s, vb_s, st_s) = _layer(
        x_sample, plen, cinit_rows,
        (kpa, cache_a_v.reshape(bs, plen, W_A)), (kpb, cache_b_v.reshape(bs, plen, W_B)),
        state_ffn_conv, prm_s, lam_pack, tm, ts, 512)

    return (y_p, y_s, ka_p, va_p, lf_p, kb_p, vb_p, st_p, ka_s, va_s, lf_s, kb_s, vb_s, st_s)
```

```python
import functools
import math

import jax
import jax.numpy as jnp
import numpy as np
from jax import lax
from jax.experimental import pallas as pl
from jax.experimental.pallas import tpu as pltpu

F32 = jnp.float32
BF16 = jnp.bfloat16

CHUNK = 64
H_A = 8
DH_A = 64
H_B = 4
DH_B = 64
DV_B = 128
W_A = H_A * DH_A
W_B = H_B * DV_B
EPS = 1e-6
LAMBDA_INIT = 0.8 - 0.6 * math.exp(-0.3 * 0)
ALIBI_SLOPES = tuple(2.0 ** (-8.0 * (i + 1) / H_B) for i in range(H_B))

LANES = 128
HEAD_SLOTS = 8
N_CUM_PIECES = 3
N_POS_PIECES = 2
NEG_BIG = -1e30
VMEM_LIMIT = 56 * 1024 * 1024


def _lane_iota(shape):
    return lax.broadcasted_iota(jnp.int32, shape, len(shape) - 1)


def _row_iota(shape):
    return lax.broadcasted_iota(jnp.int32, shape, len(shape) - 2)


def _split3(c):
    hi = c.astype(BF16).astype(F32)
    r1 = c - hi
    lo = r1.astype(BF16).astype(F32)
    lo2 = (r1 - lo).astype(BF16).astype(F32)
    return hi, lo, lo2


def _cumsum_pieces(lf, ltri_ref, base):
    hi, lo, lo2 = _split3(lf)
    ltri = ltri_ref[...]
    c = (jnp.dot(ltri, hi.astype(BF16), preferred_element_type=F32)
         + jnp.dot(ltri, lo.astype(BF16), preferred_element_type=F32)
         + jnp.dot(ltri, lo2.astype(BF16), preferred_element_type=F32)) + base
    chi, clo, clo2 = _split3(c)
    lane = _lane_iota(c.shape)
    pieces = jnp.where(lane < HEAD_SLOTS, chi, jnp.where(lane < 2 * HEAD_SLOTS, clo, clo2))
    return c, pieces.astype(BF16)


def _slot(z, j):
    pair = z[:, LANES * (j // 2):LANES * (j // 2 + 1)]
    if j % 2:
        pair = pltpu.roll(pair, 64, 1)
    return pair


def _pos_extra(pos, slope):
    v = pos.astype(F32) * slope
    hi = v.astype(BF16).astype(F32)
    lo = (v - hi).astype(BF16).astype(F32)
    lane = _lane_iota(v.shape)
    return jnp.where(lane == 64, hi, jnp.where(lane == 65, lo, 0.0))


def _store_slots(ref, tile, j, nb, rows):
    for bi in range(nb):
        ref[bi, j, :, :] = tile[bi * rows:(bi + 1) * rows, :]


def _emit_fox_keys(kaa_ref, kn, pieces, pc_ref, nb, rows):
    extra = jnp.dot(pieces, pc_ref[...], preferred_element_type=F32)
    lane = _lane_iota((kn.shape[0], LANES))
    for j in range(HEAD_SLOTS):
        tile = jnp.where(lane < 64, _slot(kn, j), extra[:, LANES * j:LANES * (j + 1)])
        _store_slots(kaa_ref, tile.astype(BF16), j, nb, rows)


def _emit_diff_keys(kba_ref, kn, pos, nb, rows):
    lane = _lane_iota((kn.shape[0], LANES))
    for h in range(H_B):
        extra = _pos_extra(pos, ALIBI_SLOPES[h])
        for m in range(2):
            j = 2 * h + m
            tile = jnp.where(lane < 64, _slot(kn, j), extra)
            _store_slots(kba_ref, tile.astype(BF16), j, nb, rows)


def _emit_queries(q_ref, qn, n_ones, nb, rows):
    lane = _lane_iota((qn.shape[0], LANES))
    ones = jnp.where(lane < 64 + n_ones, 1.0, 0.0)
    for j in range(HEAD_SLOTS):
        tile = jnp.where(lane < 64, _slot(qn, j), ones)
        _store_slots(q_ref, tile.astype(BF16), j, nb, rows)


def _cache_prep_kernel(ka_ref, lf_ref, kb_ref, ltri_ref, pc_ref,
                       kaa_ref, kba_ref, ctot_ref, carry_ref, *, tp):
    t = pl.program_id(1)

    @pl.when(t == 0)
    def _():
        carry_ref[...] = jnp.zeros_like(carry_ref)

    c, pieces = _cumsum_pieces(lf_ref[0], ltri_ref, carry_ref[...])
    carry_ref[...] = c[tp - 1:tp, :]
    ctot_ref[0] = c[tp - 1:tp, :]
    _emit_fox_keys(kaa_ref, ka_ref[0], pieces, pc_ref, 1, tp)
    pos = t * tp + _row_iota((tp, LANES))
    _emit_diff_keys(kba_ref, kb_ref[0], pos, 1, tp)


def _cache_prep(cache_a_k, lf3, cache_b_k, ltri, pc, tp):
    bx, p, _ = cache_a_k.shape
    grid = (bx, p // tp)
    const = lambda shape: pl.BlockSpec(shape, lambda b, t: (0,) * len(shape))
    return pl.pallas_call(
        functools.partial(_cache_prep_kernel, tp=tp),
        grid=grid,
        in_specs=[
            pl.BlockSpec((1, tp, W_A), lambda b, t: (b, t, 0)),
            pl.BlockSpec((1, tp, LANES), lambda b, t: (b, t, 0)),
            pl.BlockSpec((1, tp, W_B), lambda b, t: (b, t, 0)),
            const((tp, tp)),
            const((LANES, HEAD_SLOTS * LANES)),
        ],
        out_specs=[
            pl.BlockSpec((1, HEAD_SLOTS, tp, LANES), lambda b, t: (b, 0, t, 0)),
            pl.BlockSpec((1, HEAD_SLOTS, tp, LANES), lambda b, t: (b, 0, t, 0)),
            pl.BlockSpec((1, 1, LANES), lambda b, t: (b, 0, 0)),
        ],
        out_shape=[
            jax.ShapeDtypeStruct((bx, HEAD_SLOTS, p, LANES), BF16),
            jax.ShapeDtypeStruct((bx, HEAD_SLOTS, p, LANES), BF16),
            jax.ShapeDtypeStruct((bx, 1, LANES), F32),
        ],
        scratch_shapes=[pltpu.VMEM((1, LANES), F32)],
        compiler_params=pltpu.CompilerParams(
            dimension_semantics=("arbitrary", "arbitrary"), vmem_limit_bytes=VMEM_LIMIT),
        name="cache_prep",
    )(cache_a_k, lf3, cache_b_k, ltri, pc)


def _in_proj_kernel(x_ref, g_ref, wm_ref, wf_ref, bf_ref, qna_ref, kna_ref, qnb_ref, knb_ref,
                    bd_ref, ltri_ref, pc_ref, cinit_ref,
                    ka_ref, va_ref, logf_ref, kb_ref, vb_ref,
                    qaa_ref, kaa_ref, qba_ref, kba_ref, sig_ref, carry_ref,
                    *, tm, seq, pos_off):
    t = pl.program_id(0)
    carry_mode = seq >= tm
    nb = 1 if carry_mode else tm // seq
    rows = tm if carry_mode else seq
    tpb = max(1, seq // tm)

    x = x_ref[...]
    h = (x * lax.rsqrt(jnp.mean(x * x, axis=-1, keepdims=True) + EPS) * g_ref[...]).astype(BF16)

    def proj(lo, hi):
        return jnp.dot(h, wm_ref[:, lo:hi], preferred_element_type=F32)

    def head_norm(z, w_ref):
        ss = jnp.dot((z * z).astype(BF16), bd_ref[...], preferred_element_type=F32)
        return z * lax.rsqrt(ss * (1.0 / DH_A) + EPS) * w_ref[...]

    zf = jnp.dot(h, wf_ref[...], preferred_element_type=F32) + bf_ref[...]
    lf = jnp.minimum(zf, 0.0) - jnp.log1p(jnp.exp(-jnp.abs(zf)))
    lane = _lane_iota(lf.shape)
    lf = jnp.where(lane < N_CUM_PIECES * HEAD_SLOTS, lf, 0.0)
    logf_ref[...] = lf[:, :H_A]
    if carry_mode:
        @pl.when(t % tpb == 0)
        def _():
            carry_ref[...] = cinit_ref[0]
        base = carry_ref[...]
    else:
        base = cinit_ref[...]
    c, pieces = _cumsum_pieces(lf, ltri_ref, base)
    if carry_mode:
        carry_ref[...] = c[tm - 1:tm, :]

    row = t * tm + _row_iota((tm, LANES))
    pos = (row & (seq - 1)) + pos_off

    qn = head_norm(proj(0, W_A), qna_ref) * (DH_A ** -0.5)
    _emit_queries(qaa_ref, qn, N_CUM_PIECES, nb, rows)
    kn = head_norm(proj(W_A, 2 * W_A), kna_ref)
    ka_ref[...] = kn
    _emit_fox_keys(kaa_ref, kn, pieces, pc_ref, nb, rows)
    va_ref[...] = proj(2 * W_A, 3 * W_A)

    o = 3 * W_A
    qn = head_norm(proj(o, o + W_B), qnb_ref) * (DH_B ** -0.5)
    _emit_queries(qba_ref, qn, N_POS_PIECES, nb, rows)
    kn = head_norm(proj(o + W_B, o + 2 * W_B), knb_ref)
    kb_ref[...] = kn
    _emit_diff_keys(kba_ref, kn, pos, nb, rows)
    vb_ref[...] = proj(o + 2 * W_B, o + 3 * W_B)

    o = 3 * W_A + 3 * W_B
    d = (wm_ref.shape[1] - o) // 2
    for k in range(2):
        sig_ref[:, k * d:(k + 1) * d] = jax.nn.sigmoid(proj(o + k * d, o + (k + 1) * d)).astype(BF16)


def _in_proj(x2, seq, pos_off, cinit, prm, tm):
    n, d = x2.shape
    carry_mode = seq >= tm
    nb = 1 if carry_mode else tm // seq
    rows = tm if carry_mode else seq
    tpb = max(1, seq // tm)
    bx = n // seq
    assert n % tm == 0 and seq & (seq - 1) == 0
    grid = (n // tm,)
    const = lambda a: pl.BlockSpec(a.shape, lambda t: (0,) * a.ndim, pipeline_mode=pl.Buffered(1))
    rowblk = lambda w: pl.BlockSpec((tm, w), lambda t: (t, 0))
    if carry_mode:
        cinit_spec = pl.BlockSpec((1, 1, LANES), lambda t: (t // tpb, 0, 0))
        aug_spec = pl.BlockSpec((1, HEAD_SLOTS, tm, LANES), lambda t: (t // tpb, 0, t % tpb, 0))
    else:
        cinit_spec = rowblk(LANES)
        aug_spec = pl.BlockSpec((nb, HEAD_SLOTS, seq, LANES), lambda t: (t, 0, 0, 0))
    aug_shape = jax.ShapeDtypeStruct((bx, HEAD_SLOTS, seq, LANES), BF16)
    consts = [prm["g_attn"], prm["w_main"], prm["w_f"], prm["b_f"], prm["qn_a"], prm["kn_a"],
              prm["qn_b"], prm["kn_b"], prm["bd"], prm["ltri_in"], prm["pc"]]
    return pl.pallas_call(
        functools.partial(_in_proj_kernel, tm=tm, seq=seq, pos_off=pos_off),
        grid=grid,
        in_specs=[rowblk(d)] + [const(a) for a in consts] + [cinit_spec],
        out_specs=[rowblk(W_A), rowblk(W_A), rowblk(H_A), rowblk(W_B), rowblk(W_B),
                   aug_spec, aug_spec, aug_spec, aug_spec, rowblk(2 * d)],
        out_shape=[
            jax.ShapeDtypeStruct((n, W_A), F32), jax.ShapeDtypeStruct((n, W_A), F32),
            jax.ShapeDtypeStruct((n, H_A), F32),
            jax.ShapeDtypeStruct((n, W_B), F32), jax.ShapeDtypeStruct((n, W_B), F32),
            aug_shape, aug_shape, aug_shape, aug_shape,
            jax.ShapeDtypeStruct((n, 2 * d), BF16),
        ],
        scratch_shapes=[pltpu.VMEM((1, LANES), F32)],
        compiler_params=pltpu.CompilerParams(
            dimension_semantics=("arbitrary",), vmem_limit_bytes=VMEM_LIMIT),
        name="in_proj",
    )(x2, *consts, cinit)


def _softmax_step(q, k, v, carry, bias=None):
    m, l, acc = carry
    s = lax.dot_general(q, k, (((1,), (1,)), ((), ())), preferred_element_type=F32)
    if bias is not None:
        s = s + bias
    m_new = jnp.maximum(m, jnp.max(s, axis=-1, keepdims=True))
    alpha = jnp.exp(m - m_new)
    p = jnp.exp(s - m_new)
    l = alpha * l + jnp.sum(p, axis=-1, keepdims=True)
    acc = alpha * acc + jnp.dot(p.astype(BF16), v, preferred_element_type=F32)
    return m_new, l, acc


def _attend_pair(q_ref, kc_ref, vc_ref, kp_ref, vp_ref, i, diag_bias, *, tq, n_past, tkp):
    qs = (q_ref[0, 0], q_ref[0, 1])
    init = (jnp.full((tq, 1), NEG_BIG, F32), jnp.zeros((tq, 1), F32), jnp.zeros((tq, LANES), F32))

    def step(k_ref, v_ref, start, size, carries, bias=None):
        ks = pl.ds(start, size)
        v = v_ref[0, ks, :].astype(BF16)
        return tuple(_softmax_step(qs[s], k_ref[0, s, ks, :], v, carries[s], bias) for s in range(2))

    carries = (init, init)
    if 0 < n_past <= 2:
        for j in range(n_past):
            carries = step(kp_ref, vp_ref, j * tkp, tkp, carries)
    elif n_past:
        carries = lax.fori_loop(
            0, n_past, lambda j, c: step(kp_ref, vp_ref, pl.multiple_of(j * tkp, tkp), tkp, c), carries)
    carries = lax.fori_loop(
        0, i, lambda j, c: step(kc_ref, vc_ref, pl.multiple_of(j * tq, tq), tq, c), carries)
    carries = step(kc_ref, vc_ref, pl.multiple_of(i * tq, tq), tq, carries, diag_bias)
    return [acc / l for (m, l, acc) in carries]


def _fox_kernel(*refs, tq, n_past, tkp):
    if n_past:
        q_ref, kc_ref, vc_ref, kp_ref, vp_ref, o_ref = refs
    else:
        q_ref, kc_ref, vc_ref, o_ref = refs
        kp_ref = vp_ref = None
    i = pl.program_id(2)
    rowi = _row_iota((tq, tq))
    coli = _lane_iota((tq, tq))
    causal = jnp.where(coli <= rowi, 0.0, NEG_BIG)
    outs = _attend_pair(q_ref, kc_ref, vc_ref, kp_ref, vp_ref, i, causal, tq=tq, n_past=n_past, tkp=tkp)
    lane = _lane_iota((tq, LANES))
    o_ref[0] = jnp.where(lane < 64, outs[0], outs[1]).astype(BF16)


def _diff_kernel(*refs, tq, n_past, tkp):
    if n_past:
        q_ref, kc_ref, vc_ref, kp_ref, vp_ref, lam_ref, sub_ref, o_ref = refs
    else:
        q_ref, kc_ref, vc_ref, lam_ref, sub_ref, o_ref = refs
        kp_ref = vp_ref = None
    hd = pl.program_id(1)
    i = pl.program_id(2)
    slope = jnp.float32(ALIBI_SLOPES[0])
    for k in range(1, H_B):
        slope = jnp.where(hd == k, jnp.float32(ALIBI_SLOPES[k]), slope)
    rowi = _row_iota((tq, tq))
    coli = _lane_iota((tq, tq))
    ahead = jnp.maximum(coli - rowi, 0).astype(F32)
    visible = (coli // CHUNK) <= (rowi // CHUNK)
    diag_bias = jnp.where(visible, -2.0 * slope * ahead, NEG_BIG)
    o1, o2 = _attend_pair(q_ref, kc_ref, vc_ref, kp_ref, vp_ref, i, diag_bias, tq=tq, n_past=n_past, tkp=tkp)
    lq1, lk1, lq2, lk2 = (lam_ref[k:k + 1, :] for k in range(4))
    lam = (jnp.exp(jnp.sum(lq1 * lk1, axis=-1, keepdims=True))
           - jnp.exp(jnp.sum(lq2 * lk2, axis=-1, keepdims=True)) + LAMBDA_INIT)
    o = o1 - lam * o2
    o = o * lax.rsqrt(jnp.mean(o * o, axis=-1, keepdims=True) + EPS) * sub_ref[...] * (1.0 - LAMBDA_INIT)
    o_ref[0] = o.astype(BF16)


def _attention(kind, q_aug, k_aug, v_cur, past, extras, tq, tkp):
    bx, _, seq, _ = q_aug.shape
    groups = HEAD_SLOTS // 2
    grid = (bx, groups, seq // tq)
    n_past = 0
    in_specs = [
        pl.BlockSpec((1, 2, tq, LANES), lambda b, g, i: (b, g, i, 0)),
        pl.BlockSpec((1, 2, seq, LANES), lambda b, g, i: (b, g, 0, 0)),
        pl.BlockSpec((1, seq, LANES), lambda b, g, i: (b, 0, g)),
    ]
    args = [q_aug, k_aug, v_cur]
    if past is not None:
        kp, vp = past
        plen = kp.shape[2]
        n_past = plen // tkp
        in_specs += [
            pl.BlockSpec((1, 2, plen, LANES), lambda b, g, i: (b, g, 0, 0)),
            pl.BlockSpec((1, plen, LANES), lambda b, g, i: (b, 0, g)),
        ]
        args += [kp, vp]
    for a in extras:
        in_specs.append(pl.BlockSpec(a.shape, lambda b, g, i: (0,) * a.ndim))
        args.append(a)
    body = _fox_kernel if kind == "fox" else _diff_kernel
    return pl.pallas_call(
        functools.partial(body, tq=tq, n_past=n_past, tkp=tkp),
        grid=grid,
        in_specs=in_specs,
        out_specs=pl.BlockSpec((1, tq, LANES), lambda b, g, i: (b, i, g)),
        out_shape=jax.ShapeDtypeStruct((bx, seq, groups * LANES), BF16),
        compiler_params=pltpu.CompilerParams(
            dimension_semantics=("arbitrary", "arbitrary", "arbitrary"), vmem_limit_bytes=VMEM_LIMIT),
        name=kind + "_attn",
    )(*args)


FF_CHUNK = 256


def _merge_ffn_kernel(x_ref, oa_ref, ob_ref, sig_ref, woa_ref, wob_ref, wout_ref, gffn_ref,
                      wup_ref, cw_ref, cb_ref, wdown_ref, st_ref,
                      y_ref, ns_ref, carry_ref, *, tm, seq, d_ff):
    t = pl.program_id(0)
    carry_mode = seq >= tm
    nb = 1 if carry_mode else tm // seq
    rows = tm if carry_mode else seq
    tpb = max(1, seq // tm)
    d = x_ref.shape[1]

    ya = jnp.dot(oa_ref[...], woa_ref[...], preferred_element_type=F32)
    yb = jnp.dot(ob_ref[...], wob_ref[...], preferred_element_type=F32)
    m = sig_ref[:, :d].astype(F32) * ya + sig_ref[:, d:].astype(F32) * yb
    x1 = x_ref[...] + jnp.dot(m.astype(BF16), wout_ref[...], preferred_element_type=F32)
    h = (x1 * lax.rsqrt(jnp.mean(x1 * x1, axis=-1, keepdims=True) + EPS) * gffn_ref[...]).astype(BF16)

    if carry_mode:
        @pl.when(t % tpb == 0)
        def _():
            carry_ref[0:2, :] = st_ref[0]

    rowi = _row_iota((rows, FF_CHUNK))

    def conv(u, c0):
        cols = slice(c0, c0 + FF_CHUNK)
        w0, w1, w2 = (cw_ref[k:k + 1, cols] for k in range(3))
        outs = []
        for bi in range(nb):
            useg = u[bi * rows:(bi + 1) * rows, :]
            if carry_mode:
                p2, p1 = carry_ref[0:1, cols], carry_ref[1:2, cols]
            else:
                p2, p1 = st_ref[bi, 0:1, cols], st_ref[bi, 1:2, cols]
            u1 = jnp.where(rowi == 0, p1, pltpu.roll(useg, 1, 0))
            u2 = jnp.where(rowi == 0, p2, jnp.where(rowi == 1, p1, pltpu.roll(useg, 2, 0)))
            outs.append(w0 * u2 + w1 * u1 + w2 * useg + cb_ref[:, cols])
            ns_ref[bi, :, cols] = useg[rows - 2:rows, :]
        if carry_mode:
            carry_ref[0:2, cols] = u[tm - 2:tm, :]
        return outs[0] if nb == 1 else jnp.concatenate(outs, axis=0)

    acc = x1
    for c0 in range(0, d_ff, FF_CHUNK):
        ua = jnp.dot(h, wup_ref[:, c0:c0 + FF_CHUNK], preferred_element_type=F32)
        ub = jnp.dot(h, wup_ref[:, d_ff + c0:d_ff + c0 + FF_CHUNK], preferred_element_type=F32)
        a = conv(ua, c0)
        b = conv(ub, d_ff + c0)
        act = (a * jax.nn.sigmoid(a) * b).astype(BF16)
        acc = acc + jnp.dot(act, wdown_ref[c0:c0 + FF_CHUNK, :], preferred_element_type=F32)
    y_ref[...] = acc


def _merge_ffn(x2, oa, ob, sig, state, seq, prm, tm):
    n, d = x2.shape
    d_ff = prm["w_down"].shape[0]
    assert d_ff % FF_CHUNK == 0 and seq >= 2
    carry_mode = seq >= tm
    nb = 1 if carry_mode else tm // seq
    tpb = max(1, seq // tm)
    bx = n // seq
    grid = (n // tm,)
    const = lambda a: pl.BlockSpec(a.shape, lambda t: (0,) * a.ndim, pipeline_mode=pl.Buffered(1))
    rowblk = lambda w: pl.BlockSpec((tm, w), lambda t: (t, 0))
    if carry_mode:
        st_spec = pl.BlockSpec((1, 2, 2 * d_ff), lambda t: (t // tpb, 0, 0))
    else:
        st_spec = pl.BlockSpec((nb, 2, 2 * d_ff), lambda t: (t, 0, 0))
    consts1 = [prm["w_oa"], prm["w_ob"], prm["w_out"], prm["g_ffn"], prm["w_up"], prm["conv_w"],
               prm["conv_b"], prm["w_down"]]
    return pl.pallas_call(
        functools.partial(_merge_ffn_kernel, tm=tm, seq=seq, d_ff=d_ff),
        grid=grid,
        in_specs=[rowblk(d), rowblk(W_A), rowblk(W_B), rowblk(2 * d)] + [const(a) for a in consts1] + [st_spec],
        out_specs=[rowblk(d), st_spec],
        out_shape=[jax.ShapeDtypeStruct((n, d), F32), jax.ShapeDtypeStruct((bx, 2, 2 * d_ff), F32)],
        scratch_shapes=[pltpu.VMEM((8, 2 * d_ff), F32)],
        compiler_params=pltpu.CompilerParams(
            dimension_semantics=("arbitrary",), vmem_limit_bytes=VMEM_LIMIT),
        name="merge_ffn",
    )(x2, oa, ob, sig, *consts1, state)


def _tri(tm, seg):
    i = np.arange(tm)
    return jnp.asarray((i[None, :] <= i[:, None]) & (i[None, :] // seg == i[:, None] // seg), BF16)


def _piece_placer():
    p = np.zeros((LANES, HEAD_SLOTS * LANES), np.float32)
    for k in range(N_CUM_PIECES):
        for h in range(HEAD_SLOTS):
            p[k * HEAD_SLOTS + h, h * LANES + 64 + k] = -1.0
    return jnp.asarray(p, BF16)


def _block_diag_ones(width, blk):
    i = np.arange(width)
    return jnp.asarray(i[:, None] // blk == i[None, :] // blk, BF16)


def _rep3(a):
    pad = jnp.zeros(a.shape[:-1] + (LANES - N_CUM_PIECES * HEAD_SLOTS,), a.dtype)
    return jnp.concatenate([a] * N_CUM_PIECES + [pad], axis=-1)


def _layer(x, seq_pos_off, cinit, past_a, past_b, state, prm, lam_pack, tm, tq, tkp):
    bx, seq, d = x.shape
    x2 = x.reshape(bx * seq, d)
    ka, va, logf, kb, vb, qaa, kaa, qba, kba, sig = _in_proj(x2, seq, seq_pos_off, cinit, prm, tm)
    va3 = va.reshape(bx, seq, W_A)
    vb3 = vb.reshape(bx, seq, W_B)
    oa = _attention("fox", qaa, kaa, va3, past_a, (), tq, tkp)
    ob = _attention("diff", qba, kba, vb3, past_b, (lam_pack, prm["subln_b"]), tq, tkp)
    y, new_state = _merge_ffn(x2, oa.reshape(bx * seq, W_A), ob.reshape(bx * seq, W_B), sig, state, seq, prm, tm)
    return (y.reshape(bx, seq, d), ka.reshape(bx, seq, H_A, DH_A), va.reshape(bx, seq, H_A, DH_A),
            logf.reshape(bx, seq, H_A), kb.reshape(bx, seq, H_B, 2, DH_B), vb.reshape(bx, seq, H_B, DV_B),
            new_state)


def kernel(x_prompt, x_sample, cache_a_k, cache_a_v, cache_a_logf, cache_b_k, cache_b_v, state_ffn_conv,
           g_attn, w_in, b_f, qn_a, kn_a, qn_b, kn_b, lambda_q1, lambda_k1, lambda_q2, lambda_k2,
           subln_b, w_oa, w_ob, w_out, g_ffn, w_up, conv_w, conv_b, w_down):
    bp, tp_, d = x_prompt.shape
    bs, ts, _ = x_sample.shape
    plen = cache_a_k.shape[1]
    d_ff = w_down.shape[0]
    tm = 512

    f0, f1 = 3 * W_A, 3 * W_A + H_A
    prm = {
        "g_attn": g_attn.reshape(1, d),
        "w_main": jnp.concatenate([w_in[:, :f0], w_in[:, f1:]], axis=1).astype(BF16),
        "w_f": _rep3(w_in[:, f0:f1]).astype(BF16),
        "b_f": _rep3(b_f.reshape(1, H_A)),
        "qn_a": jnp.tile(qn_a, H_A).reshape(1, W_A), "kn_a": jnp.tile(kn_a, H_A).reshape(1, W_A),
        "qn_b": jnp.tile(qn_b, 2 * H_B).reshape(1, W_B), "kn_b": jnp.tile(kn_b, 2 * H_B).reshape(1, W_B),
        "bd": _block_diag_ones(W_A, DH_A),
        "pc": _piece_placer(),
        "subln_b": subln_b.reshape(1, DV_B),
        "w_oa": w_oa.astype(BF16), "w_ob": w_ob.astype(BF16), "w_out": w_out.astype(BF16),
        "g_ffn": g_ffn.reshape(1, d), "w_up": w_up.astype(BF16), "conv_w": conv_w,
        "conv_b": conv_b.reshape(1, 2 * d_ff), "w_down": w_down.astype(BF16),
    }
    lam_pack = jnp.stack([lambda_q1, lambda_k1, lambda_q2, lambda_k2])

    prm_p = dict(prm, ltri_in=_tri(tm, min(tm, tp_)))
    zeros_c = jnp.zeros((bp, 1, LANES), F32)
    zeros_state = jnp.zeros((bp, 2, 2 * d_ff), F32)
    (y_p, ka_p, va_p, lf_p, kb_p, vb_p, st_p) = _layer(
        x_prompt, 0, zeros_c, None, None, zeros_state, prm_p, lam_pack, tm, min(512, tp_), 0)

    tpc = 512
    kpa, kpb, ctot = _cache_prep(
        cache_a_k.reshape(bs, plen, W_A), _rep3(cache_a_logf), cache_b_k.reshape(bs, plen, W_B),
        _tri(tpc, tpc), prm["pc"], tpc)
    prm_s = dict(prm, ltri_in=_tri(tm, min(tm, ts)))
    cinit_rows = jnp.repeat(ctot[:, 0, :], ts, axis=0)
    (y_s, ka_s, va_s, lf_s, kb_s, vb_s, st_s) = _layer(
        x_sample, plen, cinit_rows,
        (kpa, cache_a_v.reshape(bs, plen, W_A)), (kpb, cache_b_v.reshape(bs, plen, W_B)),
        state_ffn_conv, prm_s, lam_pack, tm, ts, min(1024, plen))

    return (y_p, y_s, ka_p, va_p, lf_p, kb_p, vb_p, st_p, ka_s, va_s, lf_s, kb_s, vb_s, st_s)
```

```python
import functools
import math

import jax
import jax.numpy as jnp
import numpy as np
from jax import lax
from jax.experimental import pallas as pl
from jax.experimental.pallas import tpu as pltpu

F32 = jnp.float32
BF16 = jnp.bfloat16

CHUNK = 64
H_A = 8
DH_A = 64
H_B = 4
DH_B = 64
DV_B = 128
W_A = H_A * DH_A
W_B = H_B * DV_B
EPS = 1e-6
LAMBDA_INIT = 0.8 - 0.6 * math.exp(-0.3 * 0)
ALIBI_SLOPES = tuple(2.0 ** (-8.0 * (i + 1) / H_B) for i in range(H_B))

LANES = 128
SUBLANES = 8
HEAD_SLOTS = 8
SLOT_W = 64
N_CUM_PIECES = 3
N_POS_PIECES = 2
NEG_BIG = -1e30
VMEM_LIMIT = 58 * 1024 * 1024
FF_CHUNK = 256


def _lane_iota(shape):
    return lax.broadcasted_iota(jnp.int32, shape, len(shape) - 1)


def _row_iota(shape):
    return lax.broadcasted_iota(jnp.int32, shape, len(shape) - 2)


def _split3(c):
    hi = c.astype(BF16).astype(F32)
    r1 = c - hi
    lo = r1.astype(BF16).astype(F32)
    lo2 = (r1 - lo).astype(BF16).astype(F32)
    return hi, lo, lo2


def _cumsum_pieces(lf, ltri_ref, base):
    hi, lo, lo2 = _split3(lf)
    ltri = ltri_ref[...]
    c = (jnp.dot(ltri, hi.astype(BF16), preferred_element_type=F32)
         + jnp.dot(ltri, lo.astype(BF16), preferred_element_type=F32)
         + jnp.dot(ltri, lo2.astype(BF16), preferred_element_type=F32)) + base
    chi, clo, clo2 = _split3(c)
    lane = _lane_iota(c.shape)
    pieces = jnp.where(lane < HEAD_SLOTS, chi, jnp.where(lane < 2 * HEAD_SLOTS, clo, clo2))
    return c, pieces.astype(BF16)


def _slot(z, j):
    return z[:, SLOT_W * j:SLOT_W * (j + 1)]


def _pos_extra(pos, slope):
    v = pos.astype(F32) * slope
    hi = v.astype(BF16).astype(F32)
    lo = (v - hi).astype(BF16).astype(F32)
    lane = _lane_iota(v.shape)
    return jnp.where(lane == 0, hi, jnp.where(lane == 1, lo, 0.0))


def _store_tile(ref, slot64, extra64, j, nb, rows):
    tile = jnp.concatenate([slot64, extra64], axis=1).astype(BF16)
    for bi in range(nb):
        ref[bi, j, :, :] = tile[bi * rows:(bi + 1) * rows, :]


def _emit_fox_keys(kaa_ref, slots, pieces, pc_ref, nb, rows):
    extra = jnp.dot(pieces, pc_ref[...], preferred_element_type=F32)
    for j in range(HEAD_SLOTS):
        _store_tile(kaa_ref, slots[j], _slot(extra, j), j, nb, rows)


def _emit_diff_keys(kba_ref, slots, pos, nb, rows):
    for h in range(H_B):
        extra = _pos_extra(pos, ALIBI_SLOPES[h])
        for m in range(2):
            _store_tile(kba_ref, slots[2 * h + m], extra, 2 * h + m, nb, rows)


def _emit_queries(q_ref, qn, n_ones, nb, rows):
    lane = _lane_iota((qn.shape[0], SLOT_W))
    ones = jnp.where(lane < n_ones, 1.0, 0.0)
    for j in range(HEAD_SLOTS):
        _store_tile(q_ref, _slot(qn, j), ones, j, nb, rows)


def _strided_rows(ref, lead, j, n, stride):
    return ref[lead + (pl.ds(j, n, stride=stride), slice(None))]


def _cache_prep_kernel(ka_ref, va_ref, lf_ref, kb_ref, vb_ref, ltri_ref, pc_ref,
                       kaa_ref, vpa_ref, kba_ref, vpb_ref, ctot_ref, carry_ref, *, tp):
    t = pl.program_id(1)

    @pl.when(t == 0)
    def _():
        carry_ref[...] = jnp.zeros_like(carry_ref)

    c, pieces = _cumsum_pieces(lf_ref[0], ltri_ref, carry_ref[...])
    carry_ref[...] = c[tp - 1:tp, :]
    ctot_ref[0] = c[tp - 1:tp, :]
    ka = [_strided_rows(ka_ref, (0,), j, tp, HEAD_SLOTS) for j in range(HEAD_SLOTS)]
    _emit_fox_keys(kaa_ref, ka, pieces, pc_ref, 1, tp)
    pos = t * tp + _row_iota((tp, SLOT_W))
    kb = [_strided_rows(kb_ref, (0,), j, tp, HEAD_SLOTS) for j in range(HEAD_SLOTS)]
    _emit_diff_keys(kba_ref, kb, pos, 1, tp)
    for j in range(H_A):
        vpa_ref[0, :, SLOT_W * j:SLOT_W * (j + 1)] = _strided_rows(va_ref, (0,), j, tp, H_A).astype(BF16)
    for h in range(H_B):
        vpb_ref[0, :, DV_B * h:DV_B * (h + 1)] = _strided_rows(vb_ref, (0,), h, tp, H_B).astype(BF16)


def _cache_prep(ka3, va3, lf3, kb3, vb3, ltri, pc, tp):
    bx = ka3.shape[0]
    p = lf3.shape[1]
    grid = (bx, p // tp)
    const = lambda a: pl.BlockSpec(a.shape, lambda b, t: (0,) * a.ndim, pipeline_mode=pl.Buffered(1))
    view = lambda heads, w: pl.BlockSpec((1, tp * heads, w), lambda b, t: (b, t, 0))
    aug = pl.BlockSpec((1, HEAD_SLOTS, tp, LANES), lambda b, t: (b, 0, t, 0))
    vout = pl.BlockSpec((1, tp, W_A), lambda b, t: (b, t, 0))
    return pl.pallas_call(
        functools.partial(_cache_prep_kernel, tp=tp),
        grid=grid,
        in_specs=[view(H_A, DH_A), view(H_A, DH_A), pl.BlockSpec((1, tp, LANES), lambda b, t: (b, t, 0)),
                  view(2 * H_B, DH_B), view(H_B, DV_B), const(ltri), const(pc)],
        out_specs=[aug, vout, aug, vout, pl.BlockSpec((1, 1, LANES), lambda b, t: (b, 0, 0))],
        out_shape=[
            jax.ShapeDtypeStruct((bx, HEAD_SLOTS, p, LANES), BF16),
            jax.ShapeDtypeStruct((bx, p, W_A), BF16),
            jax.ShapeDtypeStruct((bx, HEAD_SLOTS, p, LANES), BF16),
            jax.ShapeDtypeStruct((bx, p, W_B), BF16),
            jax.ShapeDtypeStruct((bx, 1, LANES), F32),
        ],
        scratch_shapes=[pltpu.VMEM((1, LANES), F32)],
        compiler_params=pltpu.CompilerParams(
            dimension_semantics=("arbitrary", "arbitrary"), vmem_limit_bytes=VMEM_LIMIT),
        name="cache_prep",
    )(ka3, va3, lf3, kb3, vb3, ltri, pc)


def _in_proj_kernel(x_ref, g_ref, wm_ref, wf_ref, bf_ref, qna_ref, kna_ref, qnb_ref, knb_ref,
                    bd_ref, ltri_ref, pc_ref, cinit_ref,
                    ka_ref, va_ref, logf_ref, kb_ref, vb_ref,
                    qaa_ref, kaa_ref, qba_ref, kba_ref, vab_ref, vbb_ref, sig_ref, carry_ref,
                    *, tm, seq, pos_off):
    t = pl.program_id(0)
    carry_mode = seq >= tm
    nb = 1 if carry_mode else tm // seq
    rows = tm if carry_mode else seq
    tpb = max(1, seq // tm)

    x = x_ref[...]
    h = (x * lax.rsqrt(jnp.mean(x * x, axis=-1, keepdims=True) + EPS) * g_ref[...]).astype(BF16)

    def proj(lo, hi):
        return jnp.dot(h, wm_ref[:, lo:hi], preferred_element_type=F32)

    def head_norm(z, w_ref):
        ss = jnp.dot((z * z).astype(BF16), bd_ref[...], preferred_element_type=F32)
        return z * lax.rsqrt(ss * (1.0 / DH_A) + EPS) * w_ref[...]

    def store_heads(ref, z, heads):
        w = z.shape[1] // heads
        for j in range(heads):
            ref[pl.ds(j, tm, stride=heads), :] = z[:, w * j:w * (j + 1)]

    zf = jnp.dot(h, wf_ref[...], preferred_element_type=F32) + bf_ref[...]
    lf = jnp.minimum(zf, 0.0) - jnp.log1p(jnp.exp(-jnp.abs(zf)))
    lane = _lane_iota(lf.shape)
    lf = jnp.where(lane < N_CUM_PIECES * HEAD_SLOTS, lf, 0.0)
    logf_ref[...] = lf[:, :H_A]
    if carry_mode:
        @pl.when(t % tpb == 0)
        def _():
            carry_ref[...] = cinit_ref[0]
        base = carry_ref[...]
    else:
        base = cinit_ref[...]
    c, pieces = _cumsum_pieces(lf, ltri_ref, base)
    if carry_mode:
        carry_ref[...] = c[tm - 1:tm, :]

    row = t * tm + _row_iota((tm, SLOT_W))
    pos = (row & (seq - 1)) + pos_off

    qn = head_norm(proj(0, W_A), qna_ref) * (DH_A ** -0.5)
    _emit_queries(qaa_ref, qn, N_CUM_PIECES, nb, rows)
    kn = head_norm(proj(W_A, 2 * W_A), kna_ref)
    store_heads(ka_ref, kn, H_A)
    _emit_fox_keys(kaa_ref, [_slot(kn, j) for j in range(HEAD_SLOTS)], pieces, pc_ref, nb, rows)
    v = proj(2 * W_A, 3 * W_A)
    store_heads(va_ref, v, H_A)
    vab_ref[...] = v.astype(BF16)

    o = 3 * W_A
    qn = head_norm(proj(o, o + W_B), qnb_ref) * (DH_B ** -0.5)
    _emit_queries(qba_ref, qn, N_POS_PIECES, nb, rows)
    kn = head_norm(proj(o + W_B, o + 2 * W_B), knb_ref)
    store_heads(kb_ref, kn, 2 * H_B)
    _emit_diff_keys(kba_ref, [_slot(kn, j) for j in range(HEAD_SLOTS)], pos, nb, rows)
    v = proj(o + 2 * W_B, o + 3 * W_B)
    store_heads(vb_ref, v, H_B)
    vbb_ref[...] = v.astype(BF16)

    o = 3 * W_A + 3 * W_B
    d = (wm_ref.shape[1] - o) // 2
    for k in range(2):
        sig_ref[:, k * d:(k + 1) * d] = jax.nn.sigmoid(proj(o + k * d, o + (k + 1) * d)).astype(BF16)


def _in_proj(x2, seq, pos_off, cinit, prm, tm):
    n, d = x2.shape
    carry_mode = seq >= tm
    nb = 1 if carry_mode else tm // seq
    tpb = max(1, seq // tm)
    bx = n // seq
    assert n % tm == 0 and seq & (seq - 1) == 0
    grid = (n // tm,)
    const = lambda a: pl.BlockSpec(a.shape, lambda t: (0,) * a.ndim, pipeline_mode=pl.Buffered(1))
    rowblk = lambda w, mult=1: pl.BlockSpec((tm * mult, w), lambda t: (t, 0))
    if carry_mode:
        cinit_spec = pl.BlockSpec((1, 1, LANES), lambda t: (t // tpb, 0, 0))
        aug_spec = pl.BlockSpec((1, HEAD_SLOTS, tm, LANES), lambda t: (t // tpb, 0, t % tpb, 0))
    else:
        cinit_spec = rowblk(LANES)
        aug_spec = pl.BlockSpec((nb, HEAD_SLOTS, seq, LANES), lambda t: (t, 0, 0, 0))
    aug_shape = jax.ShapeDtypeStruct((bx, HEAD_SLOTS, seq, LANES), BF16)
    consts = [prm["g_attn"], prm["w_main"], prm["w_f"], prm["b_f"], prm["qn_a"], prm["kn_a"],
              prm["qn_b"], prm["kn_b"], prm["bd"], prm["ltri_in"], prm["pc"]]
    return pl.pallas_call(
        functools.partial(_in_proj_kernel, tm=tm, seq=seq, pos_off=pos_off),
        grid=grid,
        in_specs=[rowblk(d)] + [const(a) for a in consts] + [cinit_spec],
        out_specs=[rowblk(DH_A, H_A), rowblk(DH_A, H_A), rowblk(H_A), rowblk(DH_B, 2 * H_B), rowblk(DV_B, H_B),
                   aug_spec, aug_spec, aug_spec, aug_spec, rowblk(W_A), rowblk(W_B), rowblk(2 * d)],
        out_shape=[
            jax.ShapeDtypeStruct((n * H_A, DH_A), F32), jax.ShapeDtypeStruct((n * H_A, DH_A), F32),
            jax.ShapeDtypeStruct((n, H_A), F32),
            jax.ShapeDtypeStruct((n * 2 * H_B, DH_B), F32), jax.ShapeDtypeStruct((n * H_B, DV_B), F32),
            aug_shape, aug_shape, aug_shape, aug_shape,
            jax.ShapeDtypeStruct((n, W_A), BF16), jax.ShapeDtypeStruct((n, W_B), BF16),
            jax.ShapeDtypeStruct((n, 2 * d), BF16),
        ],
        scratch_shapes=[pltpu.VMEM((1, LANES), F32)],
        compiler_params=pltpu.CompilerParams(
            dimension_semantics=("arbitrary",), vmem_limit_bytes=VMEM_LIMIT),
        name="in_proj",
    )(x2, *consts, cinit)


def _softmax_step(q, k, v, carry, bias=None):
    m, l, acc = carry
    s = lax.dot_general(q, k, (((1,), (1,)), ((), ())), preferred_element_type=F32)
    if bias is not None:
        s = s + bias
    m_new = jnp.maximum(m, jnp.max(s, axis=-1, keepdims=True))
    alpha = jnp.exp(m - m_new)
    p = jnp.exp(s - m_new)
    l = alpha * l + jnp.sum(p, axis=-1, keepdims=True)
    acc = alpha * acc + jnp.dot(p.astype(BF16), v, preferred_element_type=F32)
    return m_new, l, acc


def _attend_pair(q_ref, kc_ref, vc_ref, kp_ref, vp_ref, i, diag_bias, *, tq, n_past, tkp):
    qs = (q_ref[0, 0], q_ref[0, 1])
    init = (jnp.full((tq, 1), NEG_BIG, F32), jnp.zeros((tq, 1), F32), jnp.zeros((tq, LANES), F32))

    def step(k_ref, v_ref, start, size, carries, bias=None):
        ks = pl.ds(start, size)
        v = v_ref[0, ks, :]
        return tuple(_softmax_step(qs[s], k_ref[0, s, ks, :], v, carries[s], bias) for s in range(2))

    carries = (init, init)
    if 0 < n_past <= 2:
        for j in range(n_past):
            carries = step(kp_ref, vp_ref, j * tkp, tkp, carries)
    elif n_past:
        carries = lax.fori_loop(
            0, n_past, lambda j, c: step(kp_ref, vp_ref, pl.multiple_of(j * tkp, tkp), tkp, c), carries)
    carries = lax.fori_loop(
        0, i, lambda j, c: step(kc_ref, vc_ref, pl.multiple_of(j * tq, tq), tq, c), carries)
    carries = step(kc_ref, vc_ref, pl.multiple_of(i * tq, tq), tq, carries, diag_bias)
    return [acc / l for (m, l, acc) in carries]


def _fox_kernel(*refs, tq, n_past, tkp):
    if n_past:
        q_ref, kc_ref, vc_ref, kp_ref, vp_ref, o_ref = refs
    else:
        q_ref, kc_ref, vc_ref, o_ref = refs
        kp_ref = vp_ref = None
    i = pl.program_id(2)
    rowi = _row_iota((tq, tq))
    coli = _lane_iota((tq, tq))
    causal = jnp.where(coli <= rowi, 0.0, NEG_BIG)
    outs = _attend_pair(q_ref, kc_ref, vc_ref, kp_ref, vp_ref, i, causal, tq=tq, n_past=n_past, tkp=tkp)
    lane = _lane_iota((tq, LANES))
    o_ref[0] = jnp.where(lane < SLOT_W, outs[0], outs[1]).astype(BF16)


def _diff_kernel(*refs, tq, n_past, tkp):
    if n_past:
        q_ref, kc_ref, vc_ref, kp_ref, vp_ref, lam_ref, sub_ref, o_ref = refs
    else:
        q_ref, kc_ref, vc_ref, lam_ref, sub_ref, o_ref = refs
        kp_ref = vp_ref = None
    hd = pl.program_id(1)
    i = pl.program_id(2)
    slope = jnp.float32(ALIBI_SLOPES[0])
    for k in range(1, H_B):
        slope = jnp.where(hd == k, jnp.float32(ALIBI_SLOPES[k]), slope)
    rowi = _row_iota((tq, tq))
    coli = _lane_iota((tq, tq))
    ahead = jnp.maximum(coli - rowi, 0).astype(F32)
    visible = (coli // CHUNK) <= (rowi // CHUNK)
    diag_bias = jnp.where(visible, -2.0 * slope * ahead, NEG_BIG)
    o1, o2 = _attend_pair(q_ref, kc_ref, vc_ref, kp_ref, vp_ref, i, diag_bias, tq=tq, n_past=n_past, tkp=tkp)
    lq1, lk1, lq2, lk2 = (lam_ref[k:k + 1, :] for k in range(4))
    lam = (jnp.exp(jnp.sum(lq1 * lk1, axis=-1, keepdims=True))
           - jnp.exp(jnp.sum(lq2 * lk2, axis=-1, keepdims=True)) + LAMBDA_INIT)
    o = o1 - lam * o2
    o = o * lax.rsqrt(jnp.mean(o * o, axis=-1, keepdims=True) + EPS) * sub_ref[...] * (1.0 - LAMBDA_INIT)
    o_ref[0] = o.astype(BF16)


def _attention(kind, q_aug, k_aug, v_cur, past, extras, tq, tkp):
    bx, _, seq, _ = q_aug.shape
    groups = HEAD_SLOTS // 2
    grid = (bx, groups, seq // tq)
    n_past = 0
    in_specs = [
        pl.BlockSpec((1, 2, tq, LANES), lambda b, g, i: (b, g, i, 0)),
        pl.BlockSpec((1, 2, seq, LANES), lambda b, g, i: (b, g, 0, 0)),
        pl.BlockSpec((1, seq, LANES), lambda b, g, i: (b, 0, g)),
    ]
    args = [q_aug, k_aug, v_cur]
    if past is not None:
        kp, vp = past
        plen = kp.shape[2]
        n_past = plen // tkp
        in_specs += [
            pl.BlockSpec((1, 2, plen, LANES), lambda b, g, i: (b, g, 0, 0)),
            pl.BlockSpec((1, plen, LANES), lambda b, g, i: (b, 0, g)),
        ]
        args += [kp, vp]
    for a in extras:
        in_specs.append(pl.BlockSpec(a.shape, lambda b, g, i: (0,) * a.ndim))
        args.append(a)
    body = _fox_kernel if kind == "fox" else _diff_kernel
    return pl.pallas_call(
        functools.partial(body, tq=tq, n_past=n_past, tkp=tkp),
        grid=grid,
        in_specs=in_specs,
        out_specs=pl.BlockSpec((1, tq, LANES), lambda b, g, i: (b, i, g)),
        out_shape=jax.ShapeDtypeStruct((bx, seq, groups * LANES), BF16),
        compiler_params=pltpu.CompilerParams(
            dimension_semantics=("arbitrary", "arbitrary", "arbitrary"), vmem_limit_bytes=VMEM_LIMIT),
        name=kind + "_attn",
    )(*args)


def _merge_ffn_kernel(x_ref, oa_ref, ob_ref, sig_ref, woa_ref, wob_ref, wout_ref, gffn_ref,
                      wup_ref, cw_ref, cb_ref, wdown_ref, st_ref,
                      y_ref, ns_ref, carry_ref, ubuf_ref, *, tm, seq, d_ff):
    t = pl.program_id(0)
    carry_mode = seq >= tm
    nb = 1 if carry_mode else tm // seq
    rows = tm if carry_mode else seq
    tpb = max(1, seq // tm)
    d = x_ref.shape[1]
    seg = rows + SUBLANES

    ya = jnp.dot(oa_ref[...], woa_ref[...], preferred_element_type=F32)
    yb = jnp.dot(ob_ref[...], wob_ref[...], preferred_element_type=F32)
    m = sig_ref[:, :d].astype(F32) * ya + sig_ref[:, d:].astype(F32) * yb
    x1 = x_ref[...] + jnp.dot(m.astype(BF16), wout_ref[...], preferred_element_type=F32)
    h = (x1 * lax.rsqrt(jnp.mean(x1 * x1, axis=-1, keepdims=True) + EPS) * gffn_ref[...]).astype(BF16)

    if carry_mode:
        @pl.when(t % tpb == 0)
        def _():
            carry_ref[0:2, :] = st_ref[0]

    def up(c0):
        return (jnp.dot(h, wup_ref[:, c0:c0 + FF_CHUNK], preferred_element_type=F32),
                jnp.dot(h, wup_ref[:, d_ff + c0:d_ff + c0 + FF_CHUNK], preferred_element_type=F32))

    def conv(u, c0, buf):
        cols = slice(c0, c0 + FF_CHUNK)
        w0, w1, w2 = (cw_ref[k:k + 1, cols] for k in range(3))
        outs = []
        for bi in range(nb):
            useg = u[bi * rows:(bi + 1) * rows, :]
            r0 = bi * seg + SUBLANES
            hist = carry_ref[0:2, cols] if carry_mode else st_ref[bi, :, cols]
            ubuf_ref[buf, r0 - 2:r0, :] = hist
            ubuf_ref[buf, r0:r0 + rows, :] = useg
            ns_ref[bi, :, cols] = useg[rows - 2:rows, :]
            u1 = ubuf_ref[buf, r0 - 1:r0 - 1 + rows, :]
            u2 = ubuf_ref[buf, r0 - 2:r0 - 2 + rows, :]
            outs.append(w0 * u2 + w1 * u1 + w2 * useg + cb_ref[:, cols])
        if carry_mode:
            carry_ref[0:2, cols] = u[tm - 2:tm, :]
        return outs[0] if nb == 1 else jnp.concatenate(outs, axis=0)

    chunks = list(range(0, d_ff, FF_CHUNK))
    acc = x1
    nxt = up(chunks[0])
    for k, c0 in enumerate(chunks):
        ua, ub = nxt
        if k + 1 < len(chunks):
            nxt = up(chunks[k + 1])
        a = conv(ua, c0, 2 * (k % 2))
        b = conv(ub, d_ff + c0, 2 * (k % 2) + 1)
        act = (a * jax.nn.sigmoid(a) * b).astype(BF16)
        acc = acc + jnp.dot(act, wdown_ref[c0:c0 + FF_CHUNK, :], preferred_element_type=F32)
    y_ref[...] = acc


def _merge_ffn(x2, oa, ob, sig, state, seq, prm, tm):
    n, d = x2.shape
    d_ff = prm["w_down"].shape[0]
    assert d_ff % FF_CHUNK == 0 and seq >= 2
    carry_mode = seq >= tm
    nb = 1 if carry_mode else tm // seq
    rows = tm if carry_mode else seq
    tpb = max(1, seq // tm)
    bx = n // seq
    grid = (n // tm,)
    const = lambda a: pl.BlockSpec(a.shape, lambda t: (0,) * a.ndim, pipeline_mode=pl.Buffered(1))
    rowblk = lambda w: pl.BlockSpec((tm, w), lambda t: (t, 0))
    if carry_mode:
        st_spec = pl.BlockSpec((1, 2, 2 * d_ff), lambda t: (t // tpb, 0, 0))
    else:
        st_spec = pl.BlockSpec((nb, 2, 2 * d_ff), lambda t: (t, 0, 0))
    consts1 = [prm["w_oa"], prm["w_ob"], prm["w_out"], prm["g_ffn"], prm["w_up"], prm["conv_w"],
               prm["conv_b"], prm["w_down"]]
    return pl.pallas_call(
        functools.partial(_merge_ffn_kernel, tm=tm, seq=seq, d_ff=d_ff),
        grid=grid,
        in_specs=[rowblk(d), rowblk(W_A), rowblk(W_B), rowblk(2 * d)] + [const(a) for a in consts1] + [st_spec],
        out_specs=[rowblk(d), st_spec],
        out_shape=[jax.ShapeDtypeStruct((n, d), F32), jax.ShapeDtypeStruct((bx, 2, 2 * d_ff), F32)],
        scratch_shapes=[pltpu.VMEM((SUBLANES, 2 * d_ff), F32),
                        pltpu.VMEM((4, nb * (rows + SUBLANES), FF_CHUNK), F32)],
        compiler_params=pltpu.CompilerParams(
            dimension_semantics=("arbitrary",), vmem_limit_bytes=VMEM_LIMIT),
        name="merge_ffn",
    )(x2, oa, ob, sig, *consts1, state)


def _tri(tm, seg):
    i = np.arange(tm)
    return jnp.asarray((i[None, :] <= i[:, None]) & (i[None, :] // seg == i[:, None] // seg), BF16)


def _piece_placer():
    p = np.zeros((LANES, HEAD_SLOTS * SLOT_W), np.float32)
    for k in range(N_CUM_PIECES):
        for h in range(HEAD_SLOTS):
            p[k * HEAD_SLOTS + h, h * SLOT_W + k] = -1.0
    return jnp.asarray(p, BF16)


def _block_diag_ones(width, blk):
    i = np.arange(width)
    return jnp.asarray(i[:, None] // blk == i[None, :] // blk, BF16)


def _rep3(a):
    pad = jnp.zeros(a.shape[:-1] + (LANES - N_CUM_PIECES * HEAD_SLOTS,), a.dtype)
    return jnp.concatenate([a] * N_CUM_PIECES + [pad], axis=-1)


def _tiles(seq, plen):
    return dict(tm=512, tq=min(512, seq), tkp=min(1024, plen) if plen else 0)


def _layer(x, pos_off, cinit, past_a, past_b, state, prm, lam_pack, tiles):
    bx, seq, d = x.shape
    tm, tq, tkp = tiles["tm"], tiles["tq"], tiles["tkp"]
    x2 = x.reshape(bx * seq, d)
    ka, va, logf, kb, vb, qaa, kaa, qba, kba, vab, vbb, sig = _in_proj(x2, seq, pos_off, cinit, prm, tm)
    oa = _attention("fox", qaa, kaa, vab.reshape(bx, seq, W_A), past_a, (), tq, tkp)
    ob = _attention("diff", qba, kba, vbb.reshape(bx, seq, W_B), past_b, (lam_pack, prm["subln_b"]), tq, tkp)
    y, new_state = _merge_ffn(x2, oa.reshape(bx * seq, W_A), ob.reshape(bx * seq, W_B), sig, state, seq, prm, tm)
    return (y.reshape(bx, seq, d), ka.reshape(bx, seq, H_A, DH_A), va.reshape(bx, seq, H_A, DH_A),
            logf.reshape(bx, seq, H_A), kb.reshape(bx, seq, H_B, 2, DH_B), vb.reshape(bx, seq, H_B, DV_B),
            new_state)


def kernel(x_prompt, x_sample, cache_a_k, cache_a_v, cache_a_logf, cache_b_k, cache_b_v, state_ffn_conv,
           g_attn, w_in, b_f, qn_a, kn_a, qn_b, kn_b, lambda_q1, lambda_k1, lambda_q2, lambda_k2,
           subln_b, w_oa, w_ob, w_out, g_ffn, w_up, conv_w, conv_b, w_down):
    bp, tp_, d = x_prompt.shape
    bs, ts, _ = x_sample.shape
    plen = cache_a_k.shape[1]
    d_ff = w_down.shape[0]

    f0, f1 = 3 * W_A, 3 * W_A + H_A
    prm = {
        "g_attn": g_attn.reshape(1, d),
        "w_main": jnp.concatenate([w_in[:, :f0], w_in[:, f1:]], axis=1).astype(BF16),
        "w_f": _rep3(w_in[:, f0:f1]).astype(BF16),
        "b_f": _rep3(b_f.reshape(1, H_A)),
        "qn_a": jnp.tile(qn_a, H_A).reshape(1, W_A), "kn_a": jnp.tile(kn_a, H_A).reshape(1, W_A),
        "qn_b": jnp.tile(qn_b, 2 * H_B).reshape(1, W_B), "kn_b": jnp.tile(kn_b, 2 * H_B).reshape(1, W_B),
        "bd": _block_diag_ones(W_A, DH_A),
        "pc": _piece_placer(),
        "subln_b": subln_b.reshape(1, DV_B),
        "w_oa": w_oa.astype(BF16), "w_ob": w_ob.astype(BF16), "w_out": w_out.astype(BF16),
        "g_ffn": g_ffn.reshape(1, d), "w_up": w_up.astype(BF16), "conv_w": conv_w,
        "conv_b": conv_b.reshape(1, 2 * d_ff), "w_down": w_down.astype(BF16),
    }
    lam_pack = jnp.stack([lambda_q1, lambda_k1, lambda_q2, lambda_k2])

    tl = _tiles(tp_, 0)
    prm_p = dict(prm, ltri_in=_tri(tl["tm"], min(tl["tm"], tp_)))
    zeros_c = jnp.zeros((bp, 1, LANES), F32)
    zeros_state = jnp.zeros((bp, 2, 2 * d_ff), F32)
    (y_p, ka_p, va_p, lf_p, kb_p, vb_p, st_p) = _layer(
        x_prompt, 0, zeros_c, None, None, zeros_state, prm_p, lam_pack, tl)

    tl = _tiles(ts, plen)
    tpc = min(512, plen)
    kpa, vpa, kpb, vpb, ctot = _cache_prep(
        cache_a_k.reshape(bs, plen * H_A, DH_A), cache_a_v.reshape(bs, plen * H_A, DH_A), _rep3(cache_a_logf),
        cache_b_k.reshape(bs, plen * 2 * H_B, DH_B), cache_b_v.reshape(bs, plen * H_B, DV_B),
        _tri(tpc, tpc), prm["pc"], tpc)
    prm_s = dict(prm, ltri_in=_tri(tl["tm"], min(tl["tm"], ts)))
    cinit_rows = jnp.repeat(ctot[:, 0, :], ts, axis=0)
    (y_s, ka_s, va_s, lf_s, kb_s, vb_s, st_s) = _layer(
        x_sample, plen, cinit_rows, (kpa, vpa), (kpb, vpb), state_ffn_conv, prm_s, lam_pack, tl)

    return (y_p, y_s, ka_p, va_p, lf_p, kb_p, vb_p, st_p, ka_s, va_s, lf_s, kb_s, vb_s, st_s)
```

```python
import functools
import math

import jax
import jax.numpy as jnp
import numpy as np
from jax import lax
from jax.experimental import pallas as pl
from jax.experimental.pallas import tpu as pltpu

F32 = jnp.float32
BF16 = jnp.bfloat16

CHUNK = 64
H_A = 8
DH_A = 64
H_B = 4
DH_B = 64
DV_B = 128
W_A = H_A * DH_A
W_B = H_B * DV_B
EPS = 1e-6
LAMBDA_INIT = 0.8 - 0.6 * math.exp(-0.3 * 0)
ALIBI_SLOPES = tuple(2.0 ** (-8.0 * (i + 1) / H_B) for i in range(H_B))

LANES = 128
SUBLANES = 8
HEAD_SLOTS = 8
SLOT_W = 64
N_CUM_PIECES = 3
N_POS_PIECES = 2
NEG_BIG = -1e30
VMEM_LIMIT = 58 * 1024 * 1024
FF_CHUNK = 256


def _lane_iota(shape):
    return lax.broadcasted_iota(jnp.int32, shape, len(shape) - 1)


def _row_iota(shape):
    return lax.broadcasted_iota(jnp.int32, shape, len(shape) - 2)


def _split3(c):
    hi = c.astype(BF16).astype(F32)
    r1 = c - hi
    lo = r1.astype(BF16).astype(F32)
    lo2 = (r1 - lo).astype(BF16).astype(F32)
    return hi, lo, lo2


def _cumsum_pieces(lf, ltri_ref, base):
    hi, lo, lo2 = _split3(lf)
    ltri = ltri_ref[...]
    c = (jnp.dot(ltri, hi.astype(BF16), preferred_element_type=F32)
         + jnp.dot(ltri, lo.astype(BF16), preferred_element_type=F32)
         + jnp.dot(ltri, lo2.astype(BF16), preferred_element_type=F32)) + base
    chi, clo, clo2 = _split3(c)
    lane = _lane_iota(c.shape)
    pieces = jnp.where(lane < HEAD_SLOTS, chi, jnp.where(lane < 2 * HEAD_SLOTS, clo, clo2))
    return c, pieces.astype(BF16)


def _slot(z, j):
    return z[:, SLOT_W * j:SLOT_W * (j + 1)]


def _pos_extra(pos, slope):
    v = pos.astype(F32) * slope
    hi = v.astype(BF16).astype(F32)
    lo = (v - hi).astype(BF16).astype(F32)
    lane = _lane_iota(v.shape)
    return jnp.where(lane == 0, hi, jnp.where(lane == 1, lo, 0.0))


def _store_tile(ref, slot64, extra64, j, nb, rows):
    tile = jnp.concatenate([slot64, extra64], axis=1).astype(BF16)
    for bi in range(nb):
        ref[bi, j, :, :] = tile[bi * rows:(bi + 1) * rows, :]


def _emit_fox_keys(kaa_ref, slots, pieces, pc_ref, nb, rows):
    extra = jnp.dot(pieces, pc_ref[...], preferred_element_type=F32)
    for j in range(HEAD_SLOTS):
        _store_tile(kaa_ref, slots[j], _slot(extra, j), j, nb, rows)


def _emit_diff_keys(kba_ref, slots, pos, nb, rows):
    for h in range(H_B):
        extra = _pos_extra(pos, ALIBI_SLOPES[h])
        for m in range(2):
            _store_tile(kba_ref, slots[2 * h + m], extra, 2 * h + m, nb, rows)


def _emit_queries(q_ref, qn, n_ones, nb, rows):
    lane = _lane_iota((qn.shape[0], SLOT_W))
    ones = jnp.where(lane < n_ones, 1.0, 0.0)
    for j in range(HEAD_SLOTS):
        _store_tile(q_ref, _slot(qn, j), ones, j, nb, rows)


def _cache_cumsum_kernel(lf_ref, utri_ref, c_ref, tot_ref, carry_ref, *, tb):
    t = pl.program_id(0)

    @pl.when(t == 0)
    def _():
        carry_ref[...] = jnp.zeros_like(carry_ref)

    hi, lo, lo2 = _split3(lf_ref[...])
    u = utri_ref[...]
    c = (jnp.dot(hi.astype(BF16), u, preferred_element_type=F32)
         + jnp.dot(lo.astype(BF16), u, preferred_element_type=F32)
         + jnp.dot(lo2.astype(BF16), u, preferred_element_type=F32)) + carry_ref[...]
    c_ref[...] = c
    last = c[:, tb - 1:tb]
    carry_ref[...] = last
    tot_ref[...] = jnp.broadcast_to(last, tot_ref.shape)


def _cache_cumsum(lf_t, tb):
    r, p = lf_t.shape
    i = np.arange(tb)
    utri = jnp.asarray(i[:, None] <= i[None, :], BF16)
    return pl.pallas_call(
        functools.partial(_cache_cumsum_kernel, tb=tb),
        grid=(p // tb,),
        in_specs=[pl.BlockSpec((r, tb), lambda t: (0, t)),
                  pl.BlockSpec((tb, tb), lambda t: (0, 0), pipeline_mode=pl.Buffered(1))],
        out_specs=[pl.BlockSpec((r, tb), lambda t: (0, t)), pl.BlockSpec((r, LANES), lambda t: (0, 0))],
        out_shape=[jax.ShapeDtypeStruct((r, p), F32), jax.ShapeDtypeStruct((r, LANES), F32)],
        scratch_shapes=[pltpu.VMEM((r, 1), F32)],
        compiler_params=pltpu.CompilerParams(
            dimension_semantics=("arbitrary",), vmem_limit_bytes=VMEM_LIMIT),
        name="cache_cumsum",
    )(lf_t, utri)


def _in_proj_kernel(x_ref, g_ref, wm_ref, wf_ref, bf_ref, qna_ref, kna_ref, qnb_ref, knb_ref,
                    bd_ref, ltri_ref, pc_ref, cinit_ref,
                    ka_ref, va_ref, logf_ref, kb_ref, vb_ref,
                    qaa_ref, kaa_ref, qba_ref, kba_ref, vab_ref, vbb_ref, sig_ref, carry_ref,
                    *, tm, seq, pos_off):
    t = pl.program_id(0)
    carry_mode = seq >= tm
    nb = 1 if carry_mode else tm // seq
    rows = tm if carry_mode else seq
    tpb = max(1, seq // tm)

    x = x_ref[...]
    h = (x * lax.rsqrt(jnp.mean(x * x, axis=-1, keepdims=True) + EPS) * g_ref[...]).astype(BF16)

    def proj(lo, hi):
        return jnp.dot(h, wm_ref[:, lo:hi], preferred_element_type=F32)

    def head_norm(z, w_ref):
        ss = jnp.dot((z * z).astype(BF16), bd_ref[...], preferred_element_type=F32)
        return z * lax.rsqrt(ss * (1.0 / DH_A) + EPS) * w_ref[...]

    def store_heads(ref, z, heads):
        w = z.shape[1] // heads
        for j in range(heads):
            ref[pl.ds(j, tm, stride=heads), :] = z[:, w * j:w * (j + 1)]

    def store_time_minor(ref, z):
        for bi in range(nb):
            zt = z[bi * rows:(bi + 1) * rows, :].T
            ref[bi] = zt.reshape(z.shape[1] // SLOT_W, SLOT_W, rows)

    zf = jnp.dot(h, wf_ref[...], preferred_element_type=F32) + bf_ref[...]
    lf = jnp.minimum(zf, 0.0) - jnp.log1p(jnp.exp(-jnp.abs(zf)))
    lane = _lane_iota(lf.shape)
    lf = jnp.where(lane < N_CUM_PIECES * HEAD_SLOTS, lf, 0.0)
    for bi in range(nb):
        logf_ref[bi] = lf[bi * rows:(bi + 1) * rows, :].T[:H_A, :]
    if carry_mode:
        @pl.when(t % tpb == 0)
        def _():
            carry_ref[...] = cinit_ref[0]
        base = carry_ref[...]
    else:
        base = cinit_ref[...]
    c, pieces = _cumsum_pieces(lf, ltri_ref, base)
    if carry_mode:
        carry_ref[...] = c[tm - 1:tm, :]

    row = t * tm + _row_iota((tm, SLOT_W))
    pos = (row & (seq - 1)) + pos_off

    qn = head_norm(proj(0, W_A), qna_ref) * (DH_A ** -0.5)
    _emit_queries(qaa_ref, qn, N_CUM_PIECES, nb, rows)
    kn = head_norm(proj(W_A, 2 * W_A), kna_ref)
    store_time_minor(ka_ref, kn)
    _emit_fox_keys(kaa_ref, [_slot(kn, j) for j in range(HEAD_SLOTS)], pieces, pc_ref, nb, rows)
    v = proj(2 * W_A, 3 * W_A)
    store_time_minor(va_ref, v)
    vab_ref[...] = v.astype(BF16)

    o = 3 * W_A
    qn = head_norm(proj(o, o + W_B), qnb_ref) * (DH_B ** -0.5)
    _emit_queries(qba_ref, qn, N_POS_PIECES, nb, rows)
    kn = head_norm(proj(o + W_B, o + 2 * W_B), knb_ref)
    store_time_minor(kb_ref, kn)
    _emit_diff_keys(kba_ref, [_slot(kn, j) for j in range(HEAD_SLOTS)], pos, nb, rows)
    v = proj(o + 2 * W_B, o + 3 * W_B)
    store_heads(vb_ref, v, H_B)
    vbb_ref[...] = v.astype(BF16)

    o = 3 * W_A + 3 * W_B
    d = (wm_ref.shape[1] - o) // 2
    for k in range(2):
        sig_ref[:, k * d:(k + 1) * d] = jax.nn.sigmoid(proj(o + k * d, o + (k + 1) * d)).astype(BF16)


def _in_proj(x2, seq, pos_off, cinit, prm, tm):
    n, d = x2.shape
    carry_mode = seq >= tm
    nb = 1 if carry_mode else tm // seq
    tpb = max(1, seq // tm)
    bx = n // seq
    assert n % tm == 0 and seq & (seq - 1) == 0
    grid = (n // tm,)
    const = lambda a: pl.BlockSpec(a.shape, lambda t: (0,) * a.ndim, pipeline_mode=pl.Buffered(1))
    rowblk = lambda w, mult=1: pl.BlockSpec((tm * mult, w), lambda t: (t, 0))
    if carry_mode:
        cinit_spec = pl.BlockSpec((1, 1, LANES), lambda t: (t // tpb, 0, 0))
        aug_spec = pl.BlockSpec((1, HEAD_SLOTS, tm, LANES), lambda t: (t // tpb, 0, t % tpb, 0))
    else:
        cinit_spec = rowblk(LANES)
        aug_spec = pl.BlockSpec((nb, HEAD_SLOTS, seq, LANES), lambda t: (t, 0, 0, 0))
    aug_shape = jax.ShapeDtypeStruct((bx, HEAD_SLOTS, seq, LANES), BF16)
    if carry_mode:
        tmin_spec = pl.BlockSpec((1, HEAD_SLOTS, SLOT_W, tm), lambda t: (t // tpb, 0, 0, t % tpb))
        lf_spec = pl.BlockSpec((1, H_A, tm), lambda t: (t // tpb, 0, t % tpb))
    else:
        tmin_spec = pl.BlockSpec((nb, HEAD_SLOTS, SLOT_W, seq), lambda t: (t, 0, 0, 0))
        lf_spec = pl.BlockSpec((nb, H_A, seq), lambda t: (t, 0, 0))
    tmin_shape = jax.ShapeDtypeStruct((bx, HEAD_SLOTS, SLOT_W, seq), F32)
    consts = [prm["g_attn"], prm["w_main"], prm["w_f"], prm["b_f"], prm["qn_a"], prm["kn_a"],
              prm["qn_b"], prm["kn_b"], prm["bd"], prm["ltri_in"], prm["pc"]]
    return pl.pallas_call(
        functools.partial(_in_proj_kernel, tm=tm, seq=seq, pos_off=pos_off),
        grid=grid,
        in_specs=[rowblk(d)] + [const(a) for a in consts] + [cinit_spec],
        out_specs=[tmin_spec, tmin_spec, lf_spec, tmin_spec, rowblk(DV_B, H_B),
                   aug_spec, aug_spec, aug_spec, aug_spec, rowblk(W_A), rowblk(W_B), rowblk(2 * d)],
        out_shape=[
            tmin_shape, tmin_shape, jax.ShapeDtypeStruct((bx, H_A, seq), F32),
            tmin_shape, jax.ShapeDtypeStruct((n * H_B, DV_B), F32),
            aug_shape, aug_shape, aug_shape, aug_shape,
            jax.ShapeDtypeStruct((n, W_A), BF16), jax.ShapeDtypeStruct((n, W_B), BF16),
            jax.ShapeDtypeStruct((n, 2 * d), BF16),
        ],
        scratch_shapes=[pltpu.VMEM((1, LANES), F32)],
        compiler_params=pltpu.CompilerParams(
            dimension_semantics=("arbitrary",), vmem_limit_bytes=VMEM_LIMIT),
        name="in_proj",
    )(x2, *consts, cinit)


_CONTRACT_LAST = (((1,), (1,)), ((), ()))


def _softmax_step(q, k, v, carry, bias=None, k_time_minor=False, v_time_minor=False):
    m, l, acc = carry
    if k_time_minor:
        s = jnp.dot(q, k, preferred_element_type=F32)
    else:
        s = lax.dot_general(q, k, _CONTRACT_LAST, preferred_element_type=F32)
    if bias is not None:
        s = s + bias
    m_new = jnp.maximum(m, jnp.max(s, axis=-1, keepdims=True))
    alpha = jnp.exp(m - m_new)
    p = jnp.exp(s - m_new)
    l = alpha * l + jnp.sum(p, axis=-1, keepdims=True)
    if v_time_minor:
        pv = lax.dot_general(p.astype(BF16), v, _CONTRACT_LAST, preferred_element_type=F32)
    else:
        pv = jnp.dot(p.astype(BF16), v, preferred_element_type=F32)
    return m_new, l, alpha * acc + pv


def _softmax_init(tq):
    return (jnp.full((tq, 1), NEG_BIG, F32), jnp.zeros((tq, 1), F32), jnp.zeros((tq, LANES), F32))


def _attend_pair(q_ref, k_ref, v_ref, i, diag_bias, *, tq):
    qs = (q_ref[0, 0], q_ref[0, 1])

    def step(j, carries, bias=None):
        ks = pl.ds(pl.multiple_of(j * tq, tq), tq)
        v = v_ref[0, ks, :]
        return tuple(_softmax_step(qs[s], k_ref[0, s, ks, :], v, carries[s], bias) for s in range(2))

    carries = lax.fori_loop(0, i, step, (_softmax_init(tq),) * 2)
    carries = step(i, carries, diag_bias)
    return [acc / l for (m, l, acc) in carries]


def _attend_history(q, kt_past, v_past, k_cur, v_cur, diag_bias, *, tkp, v_time_minor):
    plen = kt_past.shape[1]
    carry = _softmax_init(q.shape[0])
    for j in range(plen // tkp):
        sl = slice(j * tkp, (j + 1) * tkp)
        v = v_past[:, sl] if v_time_minor else v_past[sl, :]
        carry = _softmax_step(q, kt_past[:, sl], v, carry, k_time_minor=True, v_time_minor=v_time_minor)
    m, l, acc = _softmax_step(q, k_cur, v_cur, carry, diag_bias)
    return acc / l


def _causal_bias(tq):
    return jnp.where(_lane_iota((tq, tq)) <= _row_iota((tq, tq)), 0.0, NEG_BIG)


def _diff_diag_bias(tq, slope):
    rowi = _row_iota((tq, tq))
    coli = _lane_iota((tq, tq))
    ahead = jnp.maximum(coli - rowi, 0).astype(F32)
    visible = (coli // CHUNK) <= (rowi // CHUNK)
    return jnp.where(visible, -2.0 * slope * ahead, NEG_BIG)


def _diff_finish(o1, o2, lam_ref, sub_ref):
    lq1, lk1, lq2, lk2 = (lam_ref[k:k + 1, :] for k in range(4))
    lam = (jnp.exp(jnp.sum(lq1 * lk1, axis=-1, keepdims=True))
           - jnp.exp(jnp.sum(lq2 * lk2, axis=-1, keepdims=True)) + LAMBDA_INIT)
    o = o1 - lam * o2
    o = o * lax.rsqrt(jnp.mean(o * o, axis=-1, keepdims=True) + EPS) * sub_ref[...] * (1.0 - LAMBDA_INIT)
    return o.astype(BF16)


def _bias_rows(pieces):
    row = _row_iota((SLOT_W, pieces[0].shape[1]))
    out = jnp.zeros(row.shape, F32)
    for k, piece in enumerate(pieces):
        out = jnp.where(row == k, piece, out)
    return out.astype(BF16)


def _fox_kernel(q_ref, k_ref, v_ref, o_ref, *, tq):
    i = pl.program_id(2)
    outs = _attend_pair(q_ref, k_ref, v_ref, i, _causal_bias(tq), tq=tq)
    lane = _lane_iota((tq, LANES))
    o_ref[0] = jnp.where(lane < SLOT_W, outs[0], outs[1]).astype(BF16)


def _fox_history_kernel(q_ref, kc_ref, vc_ref, kp_ref, vp_ref, c_ref, o_ref, *, ts, tkp):
    causal = _causal_bias(ts)
    lane = _lane_iota((ts, LANES))
    for g in range(H_A // 2):
        vt = jnp.concatenate([vp_ref[0, 2 * g], vp_ref[0, 2 * g + 1]], axis=0).astype(BF16)
        vc = vc_ref[0, :, LANES * g:LANES * (g + 1)]
        outs = []
        for h in (2 * g, 2 * g + 1):
            chi, clo, clo2 = _split3(c_ref[0, h:h + 1, :])
            kt = jnp.concatenate([kp_ref[0, h].astype(BF16), _bias_rows([-chi, -clo, -clo2])], axis=0)
            outs.append(_attend_history(q_ref[0, h], kt, vt, kc_ref[0, h], vc, causal,
                                        tkp=tkp, v_time_minor=True))
        o_ref[0, :, LANES * g:LANES * (g + 1)] = jnp.where(lane < SLOT_W, outs[0], outs[1]).astype(BF16)


def _diff_kernel(q_ref, k_ref, v_ref, lam_ref, sub_ref, o_ref, *, tq):
    hd = pl.program_id(1)
    i = pl.program_id(2)
    slope = jnp.float32(ALIBI_SLOPES[0])
    for k in range(1, H_B):
        slope = jnp.where(hd == k, jnp.float32(ALIBI_SLOPES[k]), slope)
    o1, o2 = _attend_pair(q_ref, k_ref, v_ref, i, _diff_diag_bias(tq, slope), tq=tq)
    o_ref[0] = _diff_finish(o1, o2, lam_ref, sub_ref)


def _diff_history_kernel(q_ref, kc_ref, vc_ref, kp_ref, vp_ref, lam_ref, sub_ref, o_ref, *, ts, tkp):
    plen = kp_ref.shape[3]
    pos = _lane_iota((1, plen)).astype(F32)
    for h in range(H_B):
        sp = pos * ALIBI_SLOPES[h]
        hi = sp.astype(BF16).astype(F32)
        lo = (sp - hi).astype(BF16).astype(F32)
        rows = _bias_rows([hi, lo])
        vp = vp_ref[0, pl.ds(h, plen, stride=H_B), :].astype(BF16)
        vc = vc_ref[0, :, DV_B * h:DV_B * (h + 1)]
        diag_bias = _diff_diag_bias(ts, ALIBI_SLOPES[h])
        outs = []
        for j in (2 * h, 2 * h + 1):
            kt = jnp.concatenate([kp_ref[0, j].astype(BF16), rows], axis=0)
            outs.append(_attend_history(q_ref[0, j], kt, vp, kc_ref[0, j], vc, diag_bias,
                                        tkp=tkp, v_time_minor=False))
        o_ref[0, :, DV_B * h:DV_B * (h + 1)] = _diff_finish(outs[0], outs[1], lam_ref, sub_ref)


def _attention(kind, q_aug, k_aug, v_cur, extras, tq):
    bx, _, seq, _ = q_aug.shape
    groups = HEAD_SLOTS // 2
    grid = (bx, groups, seq // tq)
    in_specs = [
        pl.BlockSpec((1, 2, tq, LANES), lambda b, g, i: (b, g, i, 0)),
        pl.BlockSpec((1, 2, seq, LANES), lambda b, g, i: (b, g, 0, 0)),
        pl.BlockSpec((1, seq, LANES), lambda b, g, i: (b, 0, g)),
    ] + [pl.BlockSpec(a.shape, lambda b, g, i: (0,) * a.ndim) for a in extras]
    body = _fox_kernel if kind == "fox" else _diff_kernel
    return pl.pallas_call(
        functools.partial(body, tq=tq),
        grid=grid,
        in_specs=in_specs,
        out_specs=pl.BlockSpec((1, tq, LANES), lambda b, g, i: (b, i, g)),
        out_shape=jax.ShapeDtypeStruct((bx, seq, groups * LANES), BF16),
        compiler_params=pltpu.CompilerParams(
            dimension_semantics=("arbitrary", "arbitrary", "arbitrary"), vmem_limit_bytes=VMEM_LIMIT),
        name=kind + "_attn",
    )(q_aug, k_aug, v_cur, *extras)


def _history_attention(kind, q_aug, k_aug, v_cur, kt_past, v_past, extras, tkp):
    bx, _, ts, _ = q_aug.shape
    whole = lambda a: pl.BlockSpec((1,) + a.shape[1:], lambda b: (b,) + (0,) * (a.ndim - 1))
    shared = lambda a: pl.BlockSpec(a.shape, lambda b: (0,) * a.ndim)
    args = [q_aug, k_aug, v_cur, kt_past, v_past]
    in_specs = [whole(a) for a in args] + [whole(a) if kind == "fox" else shared(a) for a in extras]
    body = _fox_history_kernel if kind == "fox" else _diff_history_kernel
    return pl.pallas_call(
        functools.partial(body, ts=ts, tkp=tkp),
        grid=(bx,),
        in_specs=in_specs,
        out_specs=pl.BlockSpec((1, ts, v_cur.shape[2]), lambda b: (b, 0, 0)),
        out_shape=jax.ShapeDtypeStruct((bx, ts, v_cur.shape[2]), BF16),
        compiler_params=pltpu.CompilerParams(
            dimension_semantics=("arbitrary",), vmem_limit_bytes=VMEM_LIMIT),
        name=kind + "_history_attn",
    )(*args, *extras)


def _merge_ffn_kernel(x_ref, oa_ref, ob_ref, sig_ref, woa_ref, wob_ref, wout_ref, gffn_ref,
                      wup_ref, cw_ref, cb_ref, wdown_ref, st_ref,
                      y_ref, ns_ref, carry_ref, ubuf_ref, *, tm, seq, d_ff):
    t = pl.program_id(0)
    carry_mode = seq >= tm
    nb = 1 if carry_mode else tm // seq
    rows = tm if carry_mode else seq
    tpb = max(1, seq // tm)
    d = x_ref.shape[1]
    seg = rows + SUBLANES

    ya = jnp.dot(oa_ref[...], woa_ref[...], preferred_element_type=F32)
    yb = jnp.dot(ob_ref[...], wob_ref[...], preferred_element_type=F32)
    m = sig_ref[:, :d].astype(F32) * ya + sig_ref[:, d:].astype(F32) * yb
    x1 = x_ref[...] + jnp.dot(m.astype(BF16), wout_ref[...], preferred_element_type=F32)
    h = (x1 * lax.rsqrt(jnp.mean(x1 * x1, axis=-1, keepdims=True) + EPS) * gffn_ref[...]).astype(BF16)

    if carry_mode:
        @pl.when(t % tpb == 0)
        def _():
            carry_ref[0:2, :] = st_ref[0]

    def up(c0):
        return (jnp.dot(h, wup_ref[:, c0:c0 + FF_CHUNK], preferred_element_type=F32),
                jnp.dot(h, wup_ref[:, d_ff + c0:d_ff + c0 + FF_CHUNK], preferred_element_type=F32))

    def conv(u, c0, buf):
        cols = slice(c0, c0 + FF_CHUNK)
        w0, w1, w2 = (cw_ref[k:k + 1, cols] for k in range(3))
        outs = []
        for bi in range(nb):
            useg = u[bi * rows:(bi + 1) * rows, :]
            r0 = bi * seg + SUBLANES
            hist = carry_ref[0:2, cols] if carry_mode else st_ref[bi, :, cols]
            ubuf_ref[buf, r0 - 2:r0, :] = hist
            ubuf_ref[buf, r0:r0 + rows, :] = useg
            ns_ref[bi, :, cols] = useg[rows - 2:rows, :]
            u1 = ubuf_ref[buf, r0 - 1:r0 - 1 + rows, :]
            u2 = ubuf_ref[buf, r0 - 2:r0 - 2 + rows, :]
            outs.append(w0 * u2 + w1 * u1 + w2 * useg + cb_ref[:, cols])
        if carry_mode:
            carry_ref[0:2, cols] = u[tm - 2:tm, :]
        return outs[0] if nb == 1 else jnp.concatenate(outs, axis=0)

    chunks = list(range(0, d_ff, FF_CHUNK))
    acc = x1
    nxt = up(chunks[0])
    for k, c0 in enumerate(chunks):
        ua, ub = nxt
        if k + 1 < len(chunks):
            nxt = up(chunks[k + 1])
        a = conv(ua, c0, 2 * (k % 2))
        b = conv(ub, d_ff + c0, 2 * (k % 2) + 1)
        act = (a * jax.nn.sigmoid(a) * b).astype(BF16)
        acc = acc + jnp.dot(act, wdown_ref[c0:c0 + FF_CHUNK, :], preferred_element_type=F32)
    y_ref[...] = acc


def _merge_ffn(x2, oa, ob, sig, state, seq, prm, tm):
    n, d = x2.shape
    d_ff = prm["w_down"].shape[0]
    assert d_ff % FF_CHUNK == 0 and seq >= 2
    carry_mode = seq >= tm
    nb = 1 if carry_mode else tm // seq
    rows = tm if carry_mode else seq
    tpb = max(1, seq // tm)
    bx = n // seq
    grid = (n // tm,)
    const = lambda a: pl.BlockSpec(a.shape, lambda t: (0,) * a.ndim, pipeline_mode=pl.Buffered(1))
    rowblk = lambda w: pl.BlockSpec((tm, w), lambda t: (t, 0))
    if carry_mode:
        st_spec = pl.BlockSpec((1, 2, 2 * d_ff), lambda t: (t // tpb, 0, 0))
    else:
        st_spec = pl.BlockSpec((nb, 2, 2 * d_ff), lambda t: (t, 0, 0))
    consts1 = [prm["w_oa"], prm["w_ob"], prm["w_out"], prm["g_ffn"], prm["w_up"], prm["conv_w"],
               prm["conv_b"], prm["w_down"]]
    return pl.pallas_call(
        functools.partial(_merge_ffn_kernel, tm=tm, seq=seq, d_ff=d_ff),
        grid=grid,
        in_specs=[rowblk(d), rowblk(W_A), rowblk(W_B), rowblk(2 * d)] + [const(a) for a in consts1] + [st_spec],
        out_specs=[rowblk(d), st_spec],
        out_shape=[jax.ShapeDtypeStruct((n, d), F32), jax.ShapeDtypeStruct((bx, 2, 2 * d_ff), F32)],
        scratch_shapes=[pltpu.VMEM((SUBLANES, 2 * d_ff), F32),
                        pltpu.VMEM((4, nb * (rows + SUBLANES), FF_CHUNK), F32)],
        compiler_params=pltpu.CompilerParams(
            dimension_semantics=("arbitrary",), vmem_limit_bytes=VMEM_LIMIT),
        name="merge_ffn",
    )(x2, oa, ob, sig, *consts1, state)


def _tri(tm, seg):
    i = np.arange(tm)
    return jnp.asarray((i[None, :] <= i[:, None]) & (i[None, :] // seg == i[:, None] // seg), BF16)


def _piece_placer():
    p = np.zeros((LANES, HEAD_SLOTS * SLOT_W), np.float32)
    for k in range(N_CUM_PIECES):
        for h in range(HEAD_SLOTS):
            p[k * HEAD_SLOTS + h, h * SLOT_W + k] = -1.0
    return jnp.asarray(p, BF16)


def _block_diag_ones(width, blk):
    i = np.arange(width)
    return jnp.asarray(i[:, None] // blk == i[None, :] // blk, BF16)


def _rep3(a):
    pad = jnp.zeros(a.shape[:-1] + (LANES - N_CUM_PIECES * HEAD_SLOTS,), a.dtype)
    return jnp.concatenate([a] * N_CUM_PIECES + [pad], axis=-1)


def _tiles(seq, plen):
    return dict(tm=512, tq=min(512, seq), tkp=min(1024, plen) if plen else 0)


def _time_minor(a):
    return jnp.moveaxis(a, 1, -1)


def _time_major(a):
    return jnp.moveaxis(a, -1, 1)


def _layer(x, pos_off, cinit, past, state, prm, lam_pack, tiles):
    bx, seq, d = x.shape
    tm, tq, tkp = tiles["tm"], tiles["tq"], tiles["tkp"]
    x2 = x.reshape(bx * seq, d)
    ka, va, logf, kb, vb, qaa, kaa, qba, kba, vab, vbb, sig = _in_proj(x2, seq, pos_off, cinit, prm, tm)
    vab = vab.reshape(bx, seq, W_A)
    vbb = vbb.reshape(bx, seq, W_B)
    diff_prm = (lam_pack, prm["subln_b"])
    if past is None:
        oa = _attention("fox", qaa, kaa, vab, (), tq)
        ob = _attention("diff", qba, kba, vbb, diff_prm, tq)
    else:
        oa = _history_attention("fox", qaa, kaa, vab, past["kt_a"], past["vt_a"], (past["c"],), tkp)
        ob = _history_attention("diff", qba, kba, vbb, past["kt_b"], past["v_b"], diff_prm, tkp)
    y, new_state = _merge_ffn(x2, oa.reshape(bx * seq, W_A), ob.reshape(bx * seq, W_B), sig, state, seq, prm, tm)
    return (y.reshape(bx, seq, d), _time_major(ka), _time_major(va), _time_major(logf),
            _time_major(kb.reshape(bx, H_B, 2, DH_B, seq)), vb.reshape(bx, seq, H_B, DV_B), new_state)


def kernel(x_prompt, x_sample, cache_a_k, cache_a_v, cache_a_logf, cache_b_k, cache_b_v, state_ffn_conv,
           g_attn, w_in, b_f, qn_a, kn_a, qn_b, kn_b, lambda_q1, lambda_k1, lambda_q2, lambda_k2,
           subln_b, w_oa, w_ob, w_out, g_ffn, w_up, conv_w, conv_b, w_down):
    bp, tp_, d = x_prompt.shape
    bs, ts, _ = x_sample.shape
    plen = cache_a_k.shape[1]
    d_ff = w_down.shape[0]

    f0, f1 = 3 * W_A, 3 * W_A + H_A
    prm = {
        "g_attn": g_attn.reshape(1, d),
        "w_main": jnp.concatenate([w_in[:, :f0], w_in[:, f1:]], axis=1).astype(BF16),
        "w_f": _rep3(w_in[:, f0:f1]).astype(BF16),
        "b_f": _rep3(b_f.reshape(1, H_A)),
        "qn_a": jnp.tile(qn_a, H_A).reshape(1, W_A), "kn_a": jnp.tile(kn_a, H_A).reshape(1, W_A),
        "qn_b": jnp.tile(qn_b, 2 * H_B).reshape(1, W_B), "kn_b": jnp.tile(kn_b, 2 * H_B).reshape(1, W_B),
        "bd": _block_diag_ones(W_A, DH_A),
        "pc": _piece_placer(),
        "subln_b": subln_b.reshape(1, DV_B),
        "w_oa": w_oa.astype(BF16), "w_ob": w_ob.astype(BF16), "w_out": w_out.astype(BF16),
        "g_ffn": g_ffn.reshape(1, d), "w_up": w_up.astype(BF16), "conv_w": conv_w,
        "conv_b": conv_b.reshape(1, 2 * d_ff), "w_down": w_down.astype(BF16),
    }
    lam_pack = jnp.stack([lambda_q1, lambda_k1, lambda_q2, lambda_k2])

    tl = _tiles(tp_, 0)
    prm_p = dict(prm, ltri_in=_tri(tl["tm"], min(tl["tm"], tp_)))
    zeros_c = jnp.zeros((bp, 1, LANES), F32)
    zeros_state = jnp.zeros((bp, 2, 2 * d_ff), F32)
    (y_p, ka_p, va_p, lf_p, kb_p, vb_p, st_p) = _layer(
        x_prompt, 0, zeros_c, None, zeros_state, prm_p, lam_pack, tl)

    tl = _tiles(ts, plen)
    c_past, c_tot = _cache_cumsum(_time_minor(cache_a_logf).reshape(bs * H_A, plen), min(512, plen))
    past = {
        "kt_a": _time_minor(cache_a_k), "vt_a": _time_minor(cache_a_v), "c": c_past.reshape(bs, H_A, plen),
        "kt_b": _time_minor(cache_b_k).reshape(bs, HEAD_SLOTS, DH_B, plen),
        "v_b": cache_b_v.reshape(bs, plen * H_B, DV_B),
    }
    prm_s = dict(prm, ltri_in=_tri(tl["tm"], min(tl["tm"], ts)))
    cinit_rows = jnp.repeat(_rep3(c_tot[:, 0].reshape(bs, H_A)), ts, axis=0)
    (y_s, ka_s, va_s, lf_s, kb_s, vb_s, st_s) = _layer(
        x_sample, plen, cinit_rows, past, state_ffn_conv, prm_s, lam_pack, tl)

    return (y_p, y_s, ka_p, va_p, lf_p, kb_p, vb_p, st_p, ka_s, va_s, lf_s, kb_s, vb_s, st_s)
```

```python
import functools
import math

import jax
import jax.numpy as jnp
import numpy as np
from jax import lax
from jax.experimental import pallas as pl
from jax.experimental.pallas import tpu as pltpu

F32 = jnp.float32
BF16 = jnp.bfloat16

CHUNK = 64
H_A = 8
DH_A = 64
H_B = 4
DH_B = 64
DV_B = 128
W_A = H_A * DH_A
W_B = H_B * DV_B
EPS = 1e-6
LAMBDA_INIT = 0.8 - 0.6 * math.exp(-0.3 * 0)
ALIBI_SLOPES = tuple(2.0 ** (-8.0 * (i + 1) / H_B) for i in range(H_B))

LANES = 128
SUBLANES = 8
HEAD_SLOTS = 8
SLOT_W = 64
N_CUM_PIECES = 3
N_POS_PIECES = 2
NEG_BIG = -1e30
VMEM_LIMIT = 58 * 1024 * 1024
FF_CHUNK = 256


def _lane_iota(shape):
    return lax.broadcasted_iota(jnp.int32, shape, len(shape) - 1)


def _row_iota(shape):
    return lax.broadcasted_iota(jnp.int32, shape, len(shape) - 2)


def _split3(c):
    hi = c.astype(BF16).astype(F32)
    r1 = c - hi
    lo = r1.astype(BF16).astype(F32)
    lo2 = (r1 - lo).astype(BF16).astype(F32)
    return hi, lo, lo2


def _cumsum_pieces(lf, ltri_ref, base):
    hi, lo, lo2 = _split3(lf)
    ltri = ltri_ref[...]
    c = (jnp.dot(ltri, hi.astype(BF16), preferred_element_type=F32)
         + jnp.dot(ltri, lo.astype(BF16), preferred_element_type=F32)
         + jnp.dot(ltri, lo2.astype(BF16), preferred_element_type=F32)) + base
    chi, clo, clo2 = _split3(c)
    lane = _lane_iota(c.shape)
    pieces = jnp.where(lane < HEAD_SLOTS, chi, jnp.where(lane < 2 * HEAD_SLOTS, clo, clo2))
    return c, pieces.astype(BF16)


def _slot(z, j):
    return z[:, SLOT_W * j:SLOT_W * (j + 1)]


def _pos_extra(pos, slope):
    v = pos.astype(F32) * slope
    hi = v.astype(BF16).astype(F32)
    lo = (v - hi).astype(BF16).astype(F32)
    lane = _lane_iota(v.shape)
    return jnp.where(lane == 0, hi, jnp.where(lane == 1, lo, 0.0))


def _store_tile(ref, slot64, extra64, j, nb, rows, time_minor=False):
    tile = jnp.concatenate([slot64, extra64], axis=1)
    if time_minor:
        tile_t = tile.T.astype(BF16)
        for bi in range(nb):
            ref[bi, j, :, :] = tile_t[:, bi * rows:(bi + 1) * rows]
    else:
        tile = tile.astype(BF16)
        for bi in range(nb):
            ref[bi, j, :, :] = tile[bi * rows:(bi + 1) * rows, :]


def _emit_fox_keys(kaa_ref, slots, pieces, pc_ref, nb, rows):
    extra = jnp.dot(pieces, pc_ref[...], preferred_element_type=F32)
    for j in range(HEAD_SLOTS):
        _store_tile(kaa_ref, slots[j], _slot(extra, j), j, nb, rows)


def _emit_diff_keys(kba_ref, slots, pos, nb, rows):
    for h in range(H_B):
        extra = _pos_extra(pos, ALIBI_SLOPES[h])
        for m in range(2):
            _store_tile(kba_ref, slots[2 * h + m], extra, 2 * h + m, nb, rows)


def _emit_queries(q_ref, qn, n_ones, nb, rows, time_minor):
    lane = _lane_iota((qn.shape[0], SLOT_W))
    ones = jnp.where(lane < n_ones, 1.0, 0.0)
    for j in range(HEAD_SLOTS):
        _store_tile(q_ref, _slot(qn, j), ones, j, nb, rows, time_minor)


def _cache_cumsum_kernel(lf_ref, utri_ref, c_ref, tot_ref, carry_ref, *, tb):
    t = pl.program_id(0)

    @pl.when(t == 0)
    def _():
        carry_ref[...] = jnp.zeros_like(carry_ref)

    hi, lo, lo2 = _split3(lf_ref[...])
    u = utri_ref[...]
    c = (jnp.dot(hi.astype(BF16), u, preferred_element_type=F32)
         + jnp.dot(lo.astype(BF16), u, preferred_element_type=F32)
         + jnp.dot(lo2.astype(BF16), u, preferred_element_type=F32)) + carry_ref[...]
    c_ref[...] = c
    last = c[:, tb - 1:tb]
    carry_ref[...] = last
    tot_ref[...] = jnp.broadcast_to(last, tot_ref.shape)


def _cache_cumsum(lf_t, tb):
    r, p = lf_t.shape
    i = np.arange(tb)
    utri = jnp.asarray(i[:, None] <= i[None, :], BF16)
    return pl.pallas_call(
        functools.partial(_cache_cumsum_kernel, tb=tb),
        grid=(p // tb,),
        in_specs=[pl.BlockSpec((r, tb), lambda t: (0, t)),
                  pl.BlockSpec((tb, tb), lambda t: (0, 0), pipeline_mode=pl.Buffered(1))],
        out_specs=[pl.BlockSpec((r, tb), lambda t: (0, t)), pl.BlockSpec((r, LANES), lambda t: (0, 0))],
        out_shape=[jax.ShapeDtypeStruct((r, p), F32), jax.ShapeDtypeStruct((r, LANES), F32)],
        scratch_shapes=[pltpu.VMEM((r, 1), F32)],
        compiler_params=pltpu.CompilerParams(
            dimension_semantics=("arbitrary",), vmem_limit_bytes=VMEM_LIMIT),
        name="cache_cumsum",
    )(lf_t, utri)


def _in_proj_kernel(x_ref, g_ref, wm_ref, wf_ref, bf_ref, qna_ref, kna_ref, qnb_ref, knb_ref,
                    bd_ref, ltri_ref, pc_ref, cinit_ref,
                    ka_ref, va_ref, logf_ref, kb_ref, vb_ref,
                    qaa_ref, kaa_ref, qba_ref, kba_ref, vab_ref, vbb_ref, sig_ref, carry_ref,
                    *, tm, seq, pos_off, key_major):
    t = pl.program_id(0)
    carry_mode = seq >= tm
    nb = 1 if carry_mode else tm // seq
    rows = tm if carry_mode else seq
    tpb = max(1, seq // tm)

    x = x_ref[...]
    h = (x * lax.rsqrt(jnp.mean(x * x, axis=-1, keepdims=True) + EPS) * g_ref[...]).astype(BF16)

    def proj(lo, hi):
        return jnp.dot(h, wm_ref[:, lo:hi], preferred_element_type=F32)

    def head_norm(z, w_ref):
        ss = jnp.dot((z * z).astype(BF16), bd_ref[...], preferred_element_type=F32)
        return z * lax.rsqrt(ss * (1.0 / DH_A) + EPS) * w_ref[...]

    def store_heads(ref, z, heads):
        w = z.shape[1] // heads
        for j in range(heads):
            ref[pl.ds(j, tm, stride=heads), :] = z[:, w * j:w * (j + 1)]

    def store_time_minor(ref, z):
        zts = [z[bi * rows:(bi + 1) * rows, :].T for bi in range(nb)]
        for bi, zt in enumerate(zts):
            ref[bi] = zt.reshape(z.shape[1] // SLOT_W, SLOT_W, rows)
        return zts

    zf = jnp.dot(h, wf_ref[...], preferred_element_type=F32) + bf_ref[...]
    lf = jnp.minimum(zf, 0.0) - jnp.log1p(jnp.exp(-jnp.abs(zf)))
    lane = _lane_iota(lf.shape)
    lf = jnp.where(lane < N_CUM_PIECES * HEAD_SLOTS, lf, 0.0)
    for bi in range(nb):
        logf_ref[bi] = lf[bi * rows:(bi + 1) * rows, :].T[:H_A, :]
    if carry_mode:
        @pl.when(t % tpb == 0)
        def _():
            carry_ref[...] = cinit_ref[0]
        base = carry_ref[...]
    else:
        base = cinit_ref[...]
    c, pieces = _cumsum_pieces(lf, ltri_ref, base)
    if carry_mode:
        carry_ref[...] = c[tm - 1:tm, :]

    row = t * tm + _row_iota((tm, SLOT_W))
    pos = (row & (seq - 1)) + pos_off

    qn = head_norm(proj(0, W_A), qna_ref) * (DH_A ** -0.5)
    _emit_queries(qaa_ref, qn, N_CUM_PIECES, nb, rows, key_major)
    kn = head_norm(proj(W_A, 2 * W_A), kna_ref)
    store_time_minor(ka_ref, kn)
    _emit_fox_keys(kaa_ref, [_slot(kn, j) for j in range(HEAD_SLOTS)], pieces, pc_ref, nb, rows)
    v = proj(2 * W_A, 3 * W_A)
    vts = store_time_minor(va_ref, v)
    if key_major:
        vab_ref[0, 0] = vts[0].astype(BF16)
    else:
        vab_ref[...] = v.astype(BF16)

    o = 3 * W_A
    qn = head_norm(proj(o, o + W_B), qnb_ref) * (DH_B ** -0.5)
    _emit_queries(qba_ref, qn, N_POS_PIECES, nb, rows, key_major)
    kn = head_norm(proj(o + W_B, o + 2 * W_B), knb_ref)
    store_time_minor(kb_ref, kn)
    _emit_diff_keys(kba_ref, [_slot(kn, j) for j in range(HEAD_SLOTS)], pos, nb, rows)
    v = proj(o + 2 * W_B, o + 3 * W_B)
    store_heads(vb_ref, v, H_B)
    if key_major:
        vbb_ref[0, 0] = v.T.astype(BF16)
    else:
        vbb_ref[...] = v.astype(BF16)

    o = 3 * W_A + 3 * W_B
    d = (wm_ref.shape[1] - o) // 2
    for k in range(2):
        sig_ref[:, k * d:(k + 1) * d] = jax.nn.sigmoid(proj(o + k * d, o + (k + 1) * d)).astype(BF16)


def _in_proj(x2, seq, pos_off, cinit, prm, tm, key_major):
    n, d = x2.shape
    carry_mode = seq >= tm
    nb = 1 if carry_mode else tm // seq
    tpb = max(1, seq // tm)
    bx = n // seq
    assert n % tm == 0 and seq & (seq - 1) == 0
    grid = (n // tm,)
    const = lambda a: pl.BlockSpec(a.shape, lambda t: (0,) * a.ndim, pipeline_mode=pl.Buffered(1))
    rowblk = lambda w, mult=1: pl.BlockSpec((tm * mult, w), lambda t: (t, 0))
    if carry_mode:
        cinit_spec = pl.BlockSpec((1, 1, LANES), lambda t: (t // tpb, 0, 0))
        aug_spec = pl.BlockSpec((1, HEAD_SLOTS, tm, LANES), lambda t: (t // tpb, 0, t % tpb, 0))
    else:
        cinit_spec = rowblk(LANES)
        aug_spec = pl.BlockSpec((nb, HEAD_SLOTS, seq, LANES), lambda t: (t, 0, 0, 0))
    aug_shape = jax.ShapeDtypeStruct((bx, HEAD_SLOTS, seq, LANES), BF16)
    if carry_mode:
        tmin_spec = pl.BlockSpec((1, HEAD_SLOTS, SLOT_W, tm), lambda t: (t // tpb, 0, 0, t % tpb))
        lf_spec = pl.BlockSpec((1, H_A, tm), lambda t: (t // tpb, 0, t % tpb))
    else:
        tmin_spec = pl.BlockSpec((nb, HEAD_SLOTS, SLOT_W, seq), lambda t: (t, 0, 0, 0))
        lf_spec = pl.BlockSpec((nb, H_A, seq), lambda t: (t, 0, 0))
    tmin_shape = jax.ShapeDtypeStruct((bx, HEAD_SLOTS, SLOT_W, seq), F32)
    if key_major:
        assert carry_mode
        q_spec = pl.BlockSpec((1, HEAD_SLOTS, LANES, tm), lambda t: (t // tpb, 0, 0, t % tpb))
        q_shape = jax.ShapeDtypeStruct((bx, HEAD_SLOTS, LANES, seq), BF16)
        v_spec = pl.BlockSpec((1, 1, W_A, tm), lambda t: (t // tpb, t % tpb, 0, 0))
        v_shape = jax.ShapeDtypeStruct((bx, tpb, W_A, tm), BF16)
    else:
        q_spec, q_shape = aug_spec, aug_shape
        v_spec = rowblk(W_A)
        v_shape = jax.ShapeDtypeStruct((n, W_A), BF16)
    consts = [prm["g_attn"], prm["w_main"], prm["w_f"], prm["b_f"], prm["qn_a"], prm["kn_a"],
              prm["qn_b"], prm["kn_b"], prm["bd"], prm["ltri_in"], prm["pc"]]
    return pl.pallas_call(
        functools.partial(_in_proj_kernel, tm=tm, seq=seq, pos_off=pos_off, key_major=key_major),
        grid=grid,
        in_specs=[rowblk(d)] + [const(a) for a in consts] + [cinit_spec],
        out_specs=[tmin_spec, tmin_spec, lf_spec, tmin_spec, rowblk(DV_B, H_B),
                   q_spec, aug_spec, q_spec, aug_spec, v_spec, v_spec, rowblk(2 * d)],
        out_shape=[
            tmin_shape, tmin_shape, jax.ShapeDtypeStruct((bx, H_A, seq), F32),
            tmin_shape, jax.ShapeDtypeStruct((n * H_B, DV_B), F32),
            q_shape, aug_shape, q_shape, aug_shape, v_shape, v_shape,
            jax.ShapeDtypeStruct((n, 2 * d), BF16),
        ],
        scratch_shapes=[pltpu.VMEM((1, LANES), F32)],
        compiler_params=pltpu.CompilerParams(
            dimension_semantics=("arbitrary",), vmem_limit_bytes=VMEM_LIMIT),
        name="in_proj",
    )(x2, *consts, cinit)


_CONTRACT_LAST = (((1,), (1,)), ((), ()))


def _softmax_step(q, k, v, carry, bias=None, k_time_minor=False, v_time_minor=False):
    m, l, acc = carry
    if k_time_minor:
        s = jnp.dot(q, k, preferred_element_type=F32)
    else:
        s = lax.dot_general(q, k, _CONTRACT_LAST, preferred_element_type=F32)
    if bias is not None:
        s = s + bias
    m_new = jnp.maximum(m, jnp.max(s, axis=-1, keepdims=True))
    alpha = jnp.exp(m - m_new)
    p = jnp.exp(s - m_new)
    l = alpha * l + jnp.sum(p, axis=-1, keepdims=True)
    if v_time_minor:
        pv = lax.dot_general(p.astype(BF16), v, _CONTRACT_LAST, preferred_element_type=F32)
    else:
        pv = jnp.dot(p.astype(BF16), v, preferred_element_type=F32)
    return m_new, l, alpha * acc + pv


def _softmax_init(tq):
    return (jnp.full((tq, 1), NEG_BIG, F32), jnp.zeros((tq, 1), F32), jnp.zeros((tq, LANES), F32))


def _softmax_step_key_major(qt, k, vt, carry, bias=None):
    m, l, acc = carry
    s = jnp.dot(k, qt, preferred_element_type=F32)
    if bias is not None:
        s = s + bias
    m_new = jnp.maximum(m, jnp.max(s, axis=0, keepdims=True))
    alpha = jnp.exp(m - m_new)
    p = jnp.exp(s - m_new)
    l = alpha * l + jnp.sum(p, axis=0, keepdims=True)
    return m_new, l, alpha * acc + jnp.dot(vt, p.astype(BF16), preferred_element_type=F32)


def _attend_pair(q_ref, k_ref, v_ref, i, diag_bias, *, tq):
    qs = (q_ref[0, 0], q_ref[0, 1])
    init = (jnp.full((1, tq), NEG_BIG, F32), jnp.zeros((1, tq), F32), jnp.zeros((LANES, tq), F32))

    def run(n):
        carries = (init, init)
        for j in range(n + 1):
            ks = slice(j * tq, (j + 1) * tq)
            vt = v_ref[0, j, 0]
            bias = diag_bias if j == n else None
            carries = tuple(_softmax_step_key_major(qs[s], k_ref[0, s, ks, :], vt, carries[s], bias)
                            for s in range(2))
        return [(acc / l).T for (m, l, acc) in carries]

    return lax.switch(i, [functools.partial(run, n) for n in range(v_ref.shape[1])])


def _attend_history(q, kt_past, v_past, k_cur, v_cur, diag_bias, *, tkp, v_time_minor):
    plen = kt_past.shape[1]
    carry = _softmax_init(q.shape[0])
    for j in range(plen // tkp):
        sl = slice(j * tkp, (j + 1) * tkp)
        v = v_past[:, sl] if v_time_minor else v_past[sl, :]
        carry = _softmax_step(q, kt_past[:, sl], v, carry, k_time_minor=True, v_time_minor=v_time_minor)
    m, l, acc = _softmax_step(q, k_cur, v_cur, carry, diag_bias)
    return acc / l


def _query_key_iotas(tq, key_major):
    rowi, coli = _row_iota((tq, tq)), _lane_iota((tq, tq))
    return (coli, rowi) if key_major else (rowi, coli)


def _causal_bias(tq, key_major=False):
    qi, ki = _query_key_iotas(tq, key_major)
    return jnp.where(ki <= qi, 0.0, NEG_BIG)


def _diff_diag_bias(tq, slope, key_major=False):
    qi, ki = _query_key_iotas(tq, key_major)
    ahead = jnp.maximum(ki - qi, 0).astype(F32)
    visible = (ki // CHUNK) <= (qi // CHUNK)
    return jnp.where(visible, -2.0 * slope * ahead, NEG_BIG)


def _diff_finish(o1, o2, lam_ref, sub_ref):
    lq1, lk1, lq2, lk2 = (lam_ref[k:k + 1, :] for k in range(4))
    lam = (jnp.exp(jnp.sum(lq1 * lk1, axis=-1, keepdims=True))
           - jnp.exp(jnp.sum(lq2 * lk2, axis=-1, keepdims=True)) + LAMBDA_INIT)
    o = o1 - lam * o2
    o = o * lax.rsqrt(jnp.mean(o * o, axis=-1, keepdims=True) + EPS) * sub_ref[...] * (1.0 - LAMBDA_INIT)
    return o.astype(BF16)


def _bias_rows(pieces):
    row = _row_iota((SLOT_W, pieces[0].shape[1]))
    out = jnp.zeros(row.shape, F32)
    for k, piece in enumerate(pieces):
        out = jnp.where(row == k, piece, out)
    return out.astype(BF16)


def _fox_kernel(q_ref, k_ref, v_ref, o_ref, *, tq):
    i = pl.program_id(2)
    outs = _attend_pair(q_ref, k_ref, v_ref, i, _causal_bias(tq, key_major=True), tq=tq)
    lane = _lane_iota((tq, LANES))
    o_ref[0] = jnp.where(lane < SLOT_W, outs[0], outs[1]).astype(BF16)


def _fox_history_kernel(q_ref, kc_ref, vc_ref, kp_ref, vp_ref, c_ref, o_ref, *, ts, tkp):
    causal = _causal_bias(ts)
    lane = _lane_iota((ts, LANES))
    for g in range(H_A // 2):
        vt = jnp.concatenate([vp_ref[0, 2 * g], vp_ref[0, 2 * g + 1]], axis=0).astype(BF16)
        vc = vc_ref[0, :, LANES * g:LANES * (g + 1)]
        outs = []
        for h in (2 * g, 2 * g + 1):
            chi, clo, clo2 = _split3(c_ref[0, h:h + 1, :])
            kt = jnp.concatenate([kp_ref[0, h].astype(BF16), _bias_rows([-chi, -clo, -clo2])], axis=0)
            outs.append(_attend_history(q_ref[0, h], kt, vt, kc_ref[0, h], vc, causal,
                                        tkp=tkp, v_time_minor=True))
        o_ref[0, :, LANES * g:LANES * (g + 1)] = jnp.where(lane < SLOT_W, outs[0], outs[1]).astype(BF16)


def _diff_kernel(q_ref, k_ref, v_ref, lam_ref, sub_ref, o_ref, *, tq):
    hd = pl.program_id(1)
    i = pl.program_id(2)
    slope = jnp.float32(ALIBI_SLOPES[0])
    for k in range(1, H_B):
        slope = jnp.where(hd == k, jnp.float32(ALIBI_SLOPES[k]), slope)
    o1, o2 = _attend_pair(q_ref, k_ref, v_ref, i, _diff_diag_bias(tq, slope, key_major=True), tq=tq)
    o_ref[0] = _diff_finish(o1, o2, lam_ref, sub_ref)


def _diff_history_kernel(q_ref, kc_ref, vc_ref, kp_ref, vp_ref, lam_ref, sub_ref, o_ref, *, ts, tkp):
    plen = kp_ref.shape[3]
    pos = _lane_iota((1, plen)).astype(F32)
    for h in range(H_B):
        sp = pos * ALIBI_SLOPES[h]
        hi = sp.astype(BF16).astype(F32)
        lo = (sp - hi).astype(BF16).astype(F32)
        rows = _bias_rows([hi, lo])
        vp = vp_ref[0, pl.ds(h, plen, stride=H_B), :].astype(BF16)
        vc = vc_ref[0, :, DV_B * h:DV_B * (h + 1)]
        diag_bias = _diff_diag_bias(ts, ALIBI_SLOPES[h])
        outs = []
        for j in (2 * h, 2 * h + 1):
            kt = jnp.concatenate([kp_ref[0, j].astype(BF16), rows], axis=0)
            outs.append(_attend_history(q_ref[0, j], kt, vp, kc_ref[0, j], vc, diag_bias,
                                        tkp=tkp, v_time_minor=False))
        o_ref[0, :, DV_B * h:DV_B * (h + 1)] = _diff_finish(outs[0], outs[1], lam_ref, sub_ref)


def _attention(kind, q_aug, k_aug, v_cur, extras, tq):
    bx, _, seq, _ = k_aug.shape
    groups = HEAD_SLOTS // 2
    nblk = seq // tq
    assert v_cur.shape == (bx, nblk, groups * LANES, tq)
    v_cur = v_cur.reshape(bx, nblk, groups, LANES, tq)
    grid = (bx, groups, nblk)
    in_specs = [
        pl.BlockSpec((1, 2, LANES, tq), lambda b, g, i: (b, g, 0, i)),
        pl.BlockSpec((1, 2, seq, LANES), lambda b, g, i: (b, g, 0, 0)),
        pl.BlockSpec((1, nblk, 1, LANES, tq), lambda b, g, i: (b, 0, g, 0, 0)),
    ] + [pl.BlockSpec(a.shape, lambda b, g, i: (0,) * a.ndim) for a in extras]
    body = _fox_kernel if kind == "fox" else _diff_kernel
    return pl.pallas_call(
        functools.partial(body, tq=tq),
        grid=grid,
        in_specs=in_specs,
        out_specs=pl.BlockSpec((1, tq, LANES), lambda b, g, i: (b, i, g)),
        out_shape=jax.ShapeDtypeStruct((bx, seq, groups * LANES), BF16),
        compiler_params=pltpu.CompilerParams(
            dimension_semantics=("arbitrary", "arbitrary", "arbitrary"), vmem_limit_bytes=VMEM_LIMIT),
        name=kind + "_attn",
    )(q_aug, k_aug, v_cur, *extras)


def _history_attention(kind, q_aug, k_aug, v_cur, kt_past, v_past, extras, tkp):
    bx, _, ts, _ = q_aug.shape
    whole = lambda a: pl.BlockSpec((1,) + a.shape[1:], lambda b: (b,) + (0,) * (a.ndim - 1))
    shared = lambda a: pl.BlockSpec(a.shape, lambda b: (0,) * a.ndim)
    args = [q_aug, k_aug, v_cur, kt_past, v_past]
    in_specs = [whole(a) for a in args] + [whole(a) if kind == "fox" else shared(a) for a in extras]
    body = _fox_history_kernel if kind == "fox" else _diff_history_kernel
    return pl.pallas_call(
        functools.partial(body, ts=ts, tkp=tkp),
        grid=(bx,),
        in_specs=in_specs,
        out_specs=pl.BlockSpec((1, ts, v_cur.shape[2]), lambda b: (b, 0, 0)),
        out_shape=jax.ShapeDtypeStruct((bx, ts, v_cur.shape[2]), BF16),
        compiler_params=pltpu.CompilerParams(
            dimension_semantics=("arbitrary",), vmem_limit_bytes=VMEM_LIMIT),
        name=kind + "_history_attn",
    )(*args, *extras)


def _merge_ffn_kernel(x_ref, oa_ref, ob_ref, sig_ref, woa_ref, wob_ref, wout_ref, gffn_ref,
                      wup_ref, cw_ref, cb_ref, wdown_ref, st_ref,
                      y_ref, ns_ref, carry_ref, ubuf_ref, act_ref, *, tm, seq, d_ff):
    t = pl.program_id(0)
    carry_mode = seq >= tm
    nb = 1 if carry_mode else tm // seq
    rows = tm if carry_mode else seq
    tpb = max(1, seq // tm)
    d = x_ref.shape[1]
    seg = rows + SUBLANES

    ya = jnp.dot(oa_ref[...], woa_ref[...], preferred_element_type=F32)
    yb = jnp.dot(ob_ref[...], wob_ref[...], preferred_element_type=F32)
    m = sig_ref[:, :d].astype(F32) * ya + sig_ref[:, d:].astype(F32) * yb
    x1 = x_ref[...] + jnp.dot(m.astype(BF16), wout_ref[...], preferred_element_type=F32)
    h = (x1 * lax.rsqrt(jnp.mean(x1 * x1, axis=-1, keepdims=True) + EPS) * gffn_ref[...]).astype(BF16)

    if carry_mode:
        @pl.when(t % tpb == 0)
        def _():
            carry_ref[0:2, :] = st_ref[0]

    def up(c0):
        return (jnp.dot(h, wup_ref[:, c0:c0 + FF_CHUNK], preferred_element_type=F32),
                jnp.dot(h, wup_ref[:, d_ff + c0:d_ff + c0 + FF_CHUNK], preferred_element_type=F32))

    def conv(u, c0, buf):
        cols = slice(c0, c0 + FF_CHUNK)
        w0, w1, w2 = (cw_ref[k:k + 1, cols] for k in range(3))
        outs = []
        for bi in range(nb):
            useg = u[bi * rows:(bi + 1) * rows, :]
            r0 = bi * seg + SUBLANES
            hist = carry_ref[0:2, cols] if carry_mode else st_ref[bi, :, cols]
            ubuf_ref[buf, r0 - 2:r0, :] = hist
            ubuf_ref[buf, r0:r0 + rows, :] = useg
            ns_ref[bi, :, cols] = useg[rows - 2:rows, :]
            u1 = ubuf_ref[buf, r0 - 1:r0 - 1 + rows, :]
            u2 = ubuf_ref[buf, r0 - 2:r0 - 2 + rows, :]
            outs.append(w0 * u2 + w1 * u1 + w2 * useg + cb_ref[:, cols])
        if carry_mode:
            carry_ref[0:2, cols] = u[tm - 2:tm, :]
        return outs[0] if nb == 1 else jnp.concatenate(outs, axis=0)

    chunks = list(range(0, d_ff, FF_CHUNK))
    nxt = up(chunks[0])
    for k, c0 in enumerate(chunks):
        ua, ub = nxt
        if k + 1 < len(chunks):
            nxt = up(chunks[k + 1])
        a = conv(ua, c0, 2 * (k % 2))
        b = conv(ub, d_ff + c0, 2 * (k % 2) + 1)
        act_ref[:, c0:c0 + FF_CHUNK] = (a * jax.nn.sigmoid(a) * b).astype(BF16)
    y_ref[...] = x1 + jnp.dot(act_ref[...], wdown_ref[...], preferred_element_type=F32)


def _merge_ffn(x2, oa, ob, sig, state, seq, prm, tm):
    n, d = x2.shape
    d_ff = prm["w_down"].shape[0]
    assert d_ff % FF_CHUNK == 0 and seq >= 2
    carry_mode = seq >= tm
    nb = 1 if carry_mode else tm // seq
    rows = tm if carry_mode else seq
    tpb = max(1, seq // tm)
    bx = n // seq
    grid = (n // tm,)
    const = lambda a: pl.BlockSpec(a.shape, lambda t: (0,) * a.ndim, pipeline_mode=pl.Buffered(1))
    rowblk = lambda w: pl.BlockSpec((tm, w), lambda t: (t, 0))
    if carry_mode:
        st_spec = pl.BlockSpec((1, 2, 2 * d_ff), lambda t: (t // tpb, 0, 0))
    else:
        st_spec = pl.BlockSpec((nb, 2, 2 * d_ff), lambda t: (t, 0, 0))
    consts1 = [prm["w_oa"], prm["w_ob"], prm["w_out"], prm["g_ffn"], prm["w_up"], prm["conv_w"],
               prm["conv_b"], prm["w_down"]]
    return pl.pallas_call(
        functools.partial(_merge_ffn_kernel, tm=tm, seq=seq, d_ff=d_ff),
        grid=grid,
        in_specs=[rowblk(d), rowblk(W_A), rowblk(W_B), rowblk(2 * d)] + [const(a) for a in consts1] + [st_spec],
        out_specs=[rowblk(d), st_spec],
        out_shape=[jax.ShapeDtypeStruct((n, d), F32), jax.ShapeDtypeStruct((bx, 2, 2 * d_ff), F32)],
        scratch_shapes=[pltpu.VMEM((SUBLANES, 2 * d_ff), F32),
                        pltpu.VMEM((4, nb * (rows + SUBLANES), FF_CHUNK), F32),
                        pltpu.VMEM((tm, d_ff), BF16)],
        compiler_params=pltpu.CompilerParams(
            dimension_semantics=("arbitrary",), vmem_limit_bytes=VMEM_LIMIT),
        name="merge_ffn",
    )(x2, oa, ob, sig, *consts1, state)


def _tri(tm, seg):
    i = np.arange(tm)
    return jnp.asarray((i[None, :] <= i[:, None]) & (i[None, :] // seg == i[:, None] // seg), BF16)


def _piece_placer():
    p = np.zeros((LANES, HEAD_SLOTS * SLOT_W), np.float32)
    for k in range(N_CUM_PIECES):
        for h in range(HEAD_SLOTS):
            p[k * HEAD_SLOTS + h, h * SLOT_W + k] = -1.0
    return jnp.asarray(p, BF16)


def _block_diag_ones(width, blk):
    i = np.arange(width)
    return jnp.asarray(i[:, None] // blk == i[None, :] // blk, BF16)


def _rep3(a):
    pad = jnp.zeros(a.shape[:-1] + (LANES - N_CUM_PIECES * HEAD_SLOTS,), a.dtype)
    return jnp.concatenate([a] * N_CUM_PIECES + [pad], axis=-1)


def _tiles(seq, plen):
    return dict(tm=512, tq=min(512, seq), tkp=min(1024, plen) if plen else 0)


def _time_minor(a):
    return jnp.moveaxis(a, 1, -1)


def _time_major(a):
    return jnp.moveaxis(a, -1, 1)


def _layer(x, pos_off, cinit, past, state, prm, lam_pack, tiles):
    bx, seq, d = x.shape
    tm, tq, tkp = tiles["tm"], tiles["tq"], tiles["tkp"]
    x2 = x.reshape(bx * seq, d)
    ka, va, logf, kb, vb, qaa, kaa, qba, kba, vab, vbb, sig = _in_proj(
        x2, seq, pos_off, cinit, prm, tm, key_major=past is None)
    diff_prm = (lam_pack, prm["subln_b"])
    if past is None:
        assert tq == tm
        oa = _attention("fox", qaa, kaa, vab, (), tq)
        ob = _attention("diff", qba, kba, vbb, diff_prm, tq)
    else:
        vab = vab.reshape(bx, seq, W_A)
        vbb = vbb.reshape(bx, seq, W_B)
        oa = _history_attention("fox", qaa, kaa, vab, past["kt_a"], past["vt_a"], (past["c"],), tkp)
        ob = _history_attention("diff", qba, kba, vbb, past["kt_b"], past["v_b"], diff_prm, tkp)
    y, new_state = _merge_ffn(x2, oa.reshape(bx * seq, W_A), ob.reshape(bx * seq, W_B), sig, state, seq, prm, tm)
    return (y.reshape(bx, seq, d), _time_major(ka), _time_major(va), _time_major(logf),
            _time_major(kb.reshape(bx, H_B, 2, DH_B, seq)), vb.reshape(bx, seq, H_B, DV_B), new_state)


def kernel(x_prompt, x_sample, cache_a_k, cache_a_v, cache_a_logf, cache_b_k, cache_b_v, state_ffn_conv,
           g_attn, w_in, b_f, qn_a, kn_a, qn_b, kn_b, lambda_q1, lambda_k1, lambda_q2, lambda_k2,
           subln_b, w_oa, w_ob, w_out, g_ffn, w_up, conv_w, conv_b, w_down):
    bp, tp_, d = x_prompt.shape
    bs, ts, _ = x_sample.shape
    plen = cache_a_k.shape[1]
    d_ff = w_down.shape[0]

    f0, f1 = 3 * W_A, 3 * W_A + H_A
    prm = {
        "g_attn": g_attn.reshape(1, d),
        "w_main": jnp.concatenate([w_in[:, :f0], w_in[:, f1:]], axis=1).astype(BF16),
        "w_f": _rep3(w_in[:, f0:f1]).astype(BF16),
        "b_f": _rep3(b_f.reshape(1, H_A)),
        "qn_a": jnp.tile(qn_a, H_A).reshape(1, W_A), "kn_a": jnp.tile(kn_a, H_A).reshape(1, W_A),
        "qn_b": jnp.tile(qn_b, 2 * H_B).reshape(1, W_B), "kn_b": jnp.tile(kn_b, 2 * H_B).reshape(1, W_B),
        "bd": _block_diag_ones(W_A, DH_A),
        "pc": _piece_placer(),
        "subln_b": subln_b.reshape(1, DV_B),
        "w_oa": w_oa.astype(BF16), "w_ob": w_ob.astype(BF16), "w_out": w_out.astype(BF16),
        "g_ffn": g_ffn.reshape(1, d), "w_up": w_up.astype(BF16), "conv_w": conv_w,
        "conv_b": conv_b.reshape(1, 2 * d_ff), "w_down": w_down.astype(BF16),
    }
    lam_pack = jnp.stack([lambda_q1, lambda_k1, lambda_q2, lambda_k2])

    tl = _tiles(tp_, 0)
    prm_p = dict(prm, ltri_in=_tri(tl["tm"], min(tl["tm"], tp_)))
    zeros_c = jnp.zeros((bp, 1, LANES), F32)
    zeros_state = jnp.zeros((bp, 2, 2 * d_ff), F32)
    (y_p, ka_p, va_p, lf_p, kb_p, vb_p, st_p) = _layer(
        x_prompt, 0, zeros_c, None, zeros_state, prm_p, lam_pack, tl)

    tl = _tiles(ts, plen)
    c_past, c_tot = _cache_cumsum(_time_minor(cache_a_logf).reshape(bs * H_A, plen), min(512, plen))
    past = {
        "kt_a": _time_minor(cache_a_k), "vt_a": _time_minor(cache_a_v), "c": c_past.reshape(bs, H_A, plen),
        "kt_b": _time_minor(cache_b_k).reshape(bs, HEAD_SLOTS, DH_B, plen),
        "v_b": cache_b_v.reshape(bs, plen * H_B, DV_B),
    }
    prm_s = dict(prm, ltri_in=_tri(tl["tm"], min(tl["tm"], ts)))
    cinit_rows = jnp.repeat(_rep3(c_tot[:, 0].reshape(bs, H_A)), ts, axis=0)
    (y_s, ka_s, va_s, lf_s, kb_s, vb_s, st_s) = _layer(
        x_sample, plen, cinit_rows, past, state_ffn_conv, prm_s, lam_pack, tl)

    return (y_p, y_s, ka_p, va_p, lf_p, kb_p, vb_p, st_p, ka_s, va_s, lf_s, kb_s, vb_s, st_s)
```

```python
import functools
import math

import jax
import jax.numpy as jnp
import numpy as np
from jax import lax
from jax.experimental import pallas as pl
from jax.experimental.pallas import tpu as pltpu

F32 = jnp.float32
BF16 = jnp.bfloat16

CHUNK = 64
H_A = 8
DH_A = 64
H_B = 4
DH_B = 64
DV_B = 128
W_A = H_A * DH_A
W_B = H_B * DV_B
EPS = 1e-6
LAMBDA_INIT = 0.8 - 0.6 * math.exp(-0.3 * 0)
ALIBI_SLOPES = tuple(2.0 ** (-8.0 * (i + 1) / H_B) for i in range(H_B))

LANES = 128
SUBLANES = 8
HEAD_SLOTS = 8
SLOT_W = 64
N_CUM_PIECES = 3
N_POS_PIECES = 2
NEG_BIG = -1e30
VMEM_LIMIT = 58 * 1024 * 1024
FF_CHUNK = 256


def _lane_iota(shape):
    return lax.broadcasted_iota(jnp.int32, shape, len(shape) - 1)


def _row_iota(shape):
    return lax.broadcasted_iota(jnp.int32, shape, len(shape) - 2)


def _split3(c):
    hi = c.astype(BF16).astype(F32)
    r1 = c - hi
    lo = r1.astype(BF16).astype(F32)
    lo2 = (r1 - lo).astype(BF16).astype(F32)
    return hi, lo, lo2


def _cumsum_pieces(lf, ltri_ref, base):
    hi, lo, lo2 = _split3(lf)
    ltri = ltri_ref[...]
    c = (jnp.dot(ltri, hi.astype(BF16), preferred_element_type=F32)
         + jnp.dot(ltri, lo.astype(BF16), preferred_element_type=F32)
         + jnp.dot(ltri, lo2.astype(BF16), preferred_element_type=F32)) + base
    chi, clo, clo2 = _split3(c)
    lane = _lane_iota(c.shape)
    pieces = jnp.where(lane < HEAD_SLOTS, chi, jnp.where(lane < 2 * HEAD_SLOTS, clo, clo2))
    return c, pieces.astype(BF16)


def _slot(z, j):
    return z[:, SLOT_W * j:SLOT_W * (j + 1)]


def _pos_extra(pos, slope):
    v = pos.astype(F32) * slope
    hi = v.astype(BF16).astype(F32)
    lo = (v - hi).astype(BF16).astype(F32)
    lane = _lane_iota(v.shape)
    return jnp.where(lane == 0, hi, jnp.where(lane == 1, lo, 0.0))


def _store_tile(ref, slot64, extra64, j, nb, rows, time_minor=False):
    tile = jnp.concatenate([slot64, extra64], axis=1)
    if time_minor:
        tile_t = tile.T.astype(BF16)
        for bi in range(nb):
            ref[bi, j, :, :] = tile_t[:, bi * rows:(bi + 1) * rows]
    else:
        tile = tile.astype(BF16)
        for bi in range(nb):
            ref[bi, j, :, :] = tile[bi * rows:(bi + 1) * rows, :]


def _emit_fox_keys(kaa_ref, slots, pieces, pc_ref, nb, rows):
    extra = jnp.dot(pieces, pc_ref[...], preferred_element_type=F32)
    for j in range(HEAD_SLOTS):
        _store_tile(kaa_ref, slots[j], _slot(extra, j), j, nb, rows)


def _emit_diff_keys(kba_ref, slots, pos, nb, rows):
    for h in range(H_B):
        extra = _pos_extra(pos, ALIBI_SLOPES[h])
        for m in range(2):
            _store_tile(kba_ref, slots[2 * h + m], extra, 2 * h + m, nb, rows)


def _emit_queries(q_ref, qn, n_ones, nb, rows, time_minor):
    lane = _lane_iota((qn.shape[0], SLOT_W))
    ones = jnp.where(lane < n_ones, 1.0, 0.0)
    for j in range(HEAD_SLOTS):
        _store_tile(q_ref, _slot(qn, j), ones, j, nb, rows, time_minor)


def _cache_cumsum_kernel(lf_ref, utri_ref, c_ref, tot_ref, carry_ref, *, tb):
    t = pl.program_id(0)

    @pl.when(t == 0)
    def _():
        carry_ref[...] = jnp.zeros_like(carry_ref)

    hi, lo, lo2 = _split3(lf_ref[...])
    u = utri_ref[...]
    c = (jnp.dot(hi.astype(BF16), u, preferred_element_type=F32)
         + jnp.dot(lo.astype(BF16), u, preferred_element_type=F32)
         + jnp.dot(lo2.astype(BF16), u, preferred_element_type=F32)) + carry_ref[...]
    c_ref[...] = c
    last = c[:, tb - 1:tb]
    carry_ref[...] = last
    tot_ref[...] = jnp.broadcast_to(last, tot_ref.shape)


def _cache_cumsum(lf_t, tb):
    r, p = lf_t.shape
    i = np.arange(tb)
    utri = jnp.asarray(i[:, None] <= i[None, :], BF16)
    return pl.pallas_call(
        functools.partial(_cache_cumsum_kernel, tb=tb),
        grid=(p // tb,),
        in_specs=[pl.BlockSpec((r, tb), lambda t: (0, t)),
                  pl.BlockSpec((tb, tb), lambda t: (0, 0), pipeline_mode=pl.Buffered(1))],
        out_specs=[pl.BlockSpec((r, tb), lambda t: (0, t)), pl.BlockSpec((r, LANES), lambda t: (0, 0))],
        out_shape=[jax.ShapeDtypeStruct((r, p), F32), jax.ShapeDtypeStruct((r, LANES), F32)],
        scratch_shapes=[pltpu.VMEM((r, 1), F32)],
        compiler_params=pltpu.CompilerParams(
            dimension_semantics=("arbitrary",), vmem_limit_bytes=VMEM_LIMIT),
        name="cache_cumsum",
    )(lf_t, utri)


def _in_proj_kernel(x_ref, g_ref, wm_ref, wf_ref, bf_ref, qna_ref, kna_ref, qnb_ref, knb_ref,
                    bd_ref, ltri_ref, pc_ref, cinit_ref,
                    ka_ref, va_ref, logf_ref, kb_ref, vb_ref,
                    qaa_ref, kaa_ref, qba_ref, kba_ref, vab_ref, vbb_ref, sig_ref, carry_ref,
                    *, tm, seq, pos_off, key_major):
    t = pl.program_id(0)
    carry_mode = seq >= tm
    nb = 1 if carry_mode else tm // seq
    rows = tm if carry_mode else seq
    tpb = max(1, seq // tm)

    x = x_ref[...]
    h = (x * lax.rsqrt(jnp.mean(x * x, axis=-1, keepdims=True) + EPS) * g_ref[...]).astype(BF16)

    def proj(lo, hi):
        return jnp.dot(h, wm_ref[:, lo:hi], preferred_element_type=F32)

    def head_norm(z, w_ref):
        ss = jnp.dot((z * z).astype(BF16), bd_ref[...], preferred_element_type=F32)
        return z * lax.rsqrt(ss * (1.0 / DH_A) + EPS) * w_ref[...]

    def store_heads(ref, z, heads):
        w = z.shape[1] // heads
        for j in range(heads):
            ref[pl.ds(j, tm, stride=heads), :] = z[:, w * j:w * (j + 1)]

    def store_time_minor(ref, z):
        zts = [z[bi * rows:(bi + 1) * rows, :].T for bi in range(nb)]
        for bi, zt in enumerate(zts):
            ref[bi] = zt.reshape(z.shape[1] // SLOT_W, SLOT_W, rows)
        return zts

    zf = jnp.dot(h, wf_ref[...], preferred_element_type=F32) + bf_ref[...]
    lf = jnp.minimum(zf, 0.0) - jnp.log1p(jnp.exp(-jnp.abs(zf)))
    lane = _lane_iota(lf.shape)
    lf = jnp.where(lane < N_CUM_PIECES * HEAD_SLOTS, lf, 0.0)
    for bi in range(nb):
        logf_ref[bi] = lf[bi * rows:(bi + 1) * rows, :].T[:H_A, :]
    if carry_mode:
        @pl.when(t % tpb == 0)
        def _():
            carry_ref[...] = cinit_ref[0]
        base = carry_ref[...]
    else:
        base = cinit_ref[...]
    c, pieces = _cumsum_pieces(lf, ltri_ref, base)
    if carry_mode:
        carry_ref[...] = c[tm - 1:tm, :]

    row = t * tm + _row_iota((tm, SLOT_W))
    pos = (row & (seq - 1)) + pos_off

    qn = head_norm(proj(0, W_A), qna_ref) * (DH_A ** -0.5)
    _emit_queries(qaa_ref, qn, N_CUM_PIECES, nb, rows, key_major)
    kn = head_norm(proj(W_A, 2 * W_A), kna_ref)
    store_time_minor(ka_ref, kn)
    _emit_fox_keys(kaa_ref, [_slot(kn, j) for j in range(HEAD_SLOTS)], pieces, pc_ref, nb, rows)
    v = proj(2 * W_A, 3 * W_A)
    vts = store_time_minor(va_ref, v)
    if key_major:
        vab_ref[0, 0] = vts[0].astype(BF16)
    else:
        vab_ref[...] = v.astype(BF16)

    o = 3 * W_A
    qn = head_norm(proj(o, o + W_B), qnb_ref) * (DH_B ** -0.5)
    _emit_queries(qba_ref, qn, N_POS_PIECES, nb, rows, key_major)
    kn = head_norm(proj(o + W_B, o + 2 * W_B), knb_ref)
    store_time_minor(kb_ref, kn)
    _emit_diff_keys(kba_ref, [_slot(kn, j) for j in range(HEAD_SLOTS)], pos, nb, rows)
    v = proj(o + 2 * W_B, o + 3 * W_B)
    store_heads(vb_ref, v, H_B)
    if key_major:
        vbb_ref[0, 0] = v.T.astype(BF16)
    else:
        vbb_ref[...] = v.astype(BF16)

    o = 3 * W_A + 3 * W_B
    d = (wm_ref.shape[1] - o) // 2
    for k in range(2):
        sig_ref[:, k * d:(k + 1) * d] = jax.nn.sigmoid(proj(o + k * d, o + (k + 1) * d)).astype(BF16)


def _in_proj(x2, seq, pos_off, cinit, prm, tm, key_major):
    n, d = x2.shape
    carry_mode = seq >= tm
    nb = 1 if carry_mode else tm // seq
    tpb = max(1, seq // tm)
    bx = n // seq
    assert n % tm == 0 and seq & (seq - 1) == 0
    grid = (n // tm,)
    const = lambda a: pl.BlockSpec(a.shape, lambda t: (0,) * a.ndim, pipeline_mode=pl.Buffered(1))
    rowblk = lambda w, mult=1: pl.BlockSpec((tm * mult, w), lambda t: (t, 0))
    if carry_mode:
        cinit_spec = pl.BlockSpec((1, 1, LANES), lambda t: (t // tpb, 0, 0))
        aug_spec = pl.BlockSpec((1, HEAD_SLOTS, tm, LANES), lambda t: (t // tpb, 0, t % tpb, 0))
    else:
        cinit_spec = rowblk(LANES)
        aug_spec = pl.BlockSpec((nb, HEAD_SLOTS, seq, LANES), lambda t: (t, 0, 0, 0))
    aug_shape = jax.ShapeDtypeStruct((bx, HEAD_SLOTS, seq, LANES), BF16)
    if carry_mode:
        tmin_spec = pl.BlockSpec((1, HEAD_SLOTS, SLOT_W, tm), lambda t: (t // tpb, 0, 0, t % tpb))
        lf_spec = pl.BlockSpec((1, H_A, tm), lambda t: (t // tpb, 0, t % tpb))
    else:
        tmin_spec = pl.BlockSpec((nb, HEAD_SLOTS, SLOT_W, seq), lambda t: (t, 0, 0, 0))
        lf_spec = pl.BlockSpec((nb, H_A, seq), lambda t: (t, 0, 0))
    tmin_shape = jax.ShapeDtypeStruct((bx, HEAD_SLOTS, SLOT_W, seq), F32)
    if key_major:
        assert carry_mode
        q_spec = pl.BlockSpec((1, HEAD_SLOTS, LANES, tm), lambda t: (t // tpb, 0, 0, t % tpb))
        q_shape = jax.ShapeDtypeStruct((bx, HEAD_SLOTS, LANES, seq), BF16)
        v_spec = pl.BlockSpec((1, 1, W_A, tm), lambda t: (t // tpb, t % tpb, 0, 0))
        v_shape = jax.ShapeDtypeStruct((bx, tpb, W_A, tm), BF16)
    else:
        q_spec, q_shape = aug_spec, aug_shape
        v_spec = rowblk(W_A)
        v_shape = jax.ShapeDtypeStruct((n, W_A), BF16)
    consts = [prm["g_attn"], prm["w_main"], prm["w_f"], prm["b_f"], prm["qn_a"], prm["kn_a"],
              prm["qn_b"], prm["kn_b"], prm["bd"], prm["ltri_in"], prm["pc"]]
    return pl.pallas_call(
        functools.partial(_in_proj_kernel, tm=tm, seq=seq, pos_off=pos_off, key_major=key_major),
        grid=grid,
        in_specs=[rowblk(d)] + [const(a) for a in consts] + [cinit_spec],
        out_specs=[tmin_spec, tmin_spec, lf_spec, tmin_spec, rowblk(DV_B, H_B),
                   q_spec, aug_spec, q_spec, aug_spec, v_spec, v_spec, rowblk(2 * d)],
        out_shape=[
            tmin_shape, tmin_shape, jax.ShapeDtypeStruct((bx, H_A, seq), F32),
            tmin_shape, jax.ShapeDtypeStruct((n * H_B, DV_B), F32),
            q_shape, aug_shape, q_shape, aug_shape, v_shape, v_shape,
            jax.ShapeDtypeStruct((n, 2 * d), BF16),
        ],
        scratch_shapes=[pltpu.VMEM((1, LANES), F32)],
        compiler_params=pltpu.CompilerParams(
            dimension_semantics=("arbitrary",), vmem_limit_bytes=VMEM_LIMIT),
        name="in_proj",
    )(x2, *consts, cinit)


_CONTRACT_LAST = (((1,), (1,)), ((), ()))


def _softmax_absorb(s, v, carry, v_time_minor=False):
    m, l, acc = carry
    m_new = jnp.maximum(m, jnp.max(s, axis=-1, keepdims=True))
    alpha = jnp.exp(m - m_new)
    p = jnp.exp(s - m_new)
    l = alpha * l + jnp.sum(p, axis=-1, keepdims=True)
    if v_time_minor:
        pv = lax.dot_general(p.astype(BF16), v, _CONTRACT_LAST, preferred_element_type=F32)
    else:
        pv = jnp.dot(p.astype(BF16), v, preferred_element_type=F32)
    return m_new, l, alpha * acc + pv


def _attend_pair(q_ref, k_ref, v_ref, i, diag_bias, *, tq):
    qs = (q_ref[0, 0], q_ref[0, 1])

    def scores(c, j, n):
        s = jnp.dot(k_ref[0, c, j * tq:(j + 1) * tq, :], qs[c], preferred_element_type=F32)
        return s + diag_bias if j == n else s

    def run(n):
        m = [jnp.full((1, tq), NEG_BIG, F32)] * 2
        l = [jnp.zeros((1, tq), F32)] * 2
        acc = [jnp.zeros((LANES, tq), F32)] * 2
        s = [scores(c, 0, n) for c in range(2)]
        for j in range(n + 1):
            s_next = [scores(c, j + 1, n) for c in range(2)] if j < n else None
            vt = v_ref[0, j, 0]
            for c in range(2):
                m_new = jnp.maximum(m[c], jnp.max(s[c], axis=0, keepdims=True))
                alpha = jnp.exp(m[c] - m_new)
                p = jnp.exp(s[c] - m_new)
                l[c] = alpha * l[c] + jnp.sum(p, axis=0, keepdims=True)
                acc[c] = alpha * acc[c] + jnp.dot(vt, p.astype(BF16), preferred_element_type=F32)
                m[c] = m_new
            s = s_next
        return [(acc[c] / l[c]).T for c in range(2)]

    return lax.switch(i, [functools.partial(run, n) for n in range(v_ref.shape[1])])


def _attend_history(q, kt_past, bias_row, v_past, k_cur, v_cur, diag_bias, *, v_time_minor):
    ts = q.shape[0]
    init = (jnp.full((ts, 1), NEG_BIG, F32), jnp.zeros((ts, 1), F32), jnp.zeros((ts, LANES), F32))
    s = jnp.dot(q[:, :SLOT_W], kt_past, preferred_element_type=F32) + bias_row
    carry = _softmax_absorb(s, v_past, init, v_time_minor)
    s = lax.dot_general(q, k_cur, _CONTRACT_LAST, preferred_element_type=F32) + diag_bias
    m, l, acc = _softmax_absorb(s, v_cur, carry)
    return acc / l


def _diag_indices(nk, nq, q0, key_major):
    shape = (nk, nq) if key_major else (nq, nk)
    rowi, coli = _row_iota(shape), _lane_iota(shape)
    return (coli + q0, rowi) if key_major else (rowi + q0, coli)


def _causal_bias(qi, ki):
    return jnp.where(ki <= qi, 0.0, NEG_BIG)


def _diff_diag_bias(qi, ki, slope):
    ahead = jnp.maximum(ki - qi, 0).astype(F32)
    visible = (ki // CHUNK) <= (qi // CHUNK)
    return jnp.where(visible, -2.0 * slope * ahead, NEG_BIG)


def _diff_finish(o1, o2, lam_ref, sub_ref):
    lq1, lk1, lq2, lk2 = (lam_ref[k:k + 1, :] for k in range(4))
    lam = (jnp.exp(jnp.sum(lq1 * lk1, axis=-1, keepdims=True))
           - jnp.exp(jnp.sum(lq2 * lk2, axis=-1, keepdims=True)) + LAMBDA_INIT)
    o = o1 - lam * o2
    o = o * lax.rsqrt(jnp.mean(o * o, axis=-1, keepdims=True) + EPS) * sub_ref[...] * (1.0 - LAMBDA_INIT)
    return o.astype(BF16)


def _fox_kernel(q_ref, k_ref, v_ref, o_ref, *, tq):
    i = pl.program_id(2)
    bias = _causal_bias(*_diag_indices(tq, tq, 0, True))
    outs = _attend_pair(q_ref, k_ref, v_ref, i, bias, tq=tq)
    lane = _lane_iota((tq, LANES))
    o_ref[0] = jnp.where(lane < SLOT_W, outs[0], outs[1]).astype(BF16)


def _fox_history_kernel(q_ref, kc_ref, vc_ref, kp_ref, vp_ref, c_ref, o_ref, *, ts):
    causal = _causal_bias(*_diag_indices(ts, ts, 0, False))
    lane = _lane_iota((ts, LANES))
    for g in range(H_A // 2):
        vt = jnp.concatenate([vp_ref[0, 2 * g], vp_ref[0, 2 * g + 1]], axis=0).astype(BF16)
        vc = vc_ref[0, :, LANES * g:LANES * (g + 1)]
        outs = [_attend_history(q_ref[0, h], kp_ref[0, h].astype(BF16), -c_ref[0, h:h + 1, :], vt,
                                kc_ref[0, h], vc, causal, v_time_minor=True) for h in (2 * g, 2 * g + 1)]
        o_ref[0, :, LANES * g:LANES * (g + 1)] = jnp.where(lane < SLOT_W, outs[0], outs[1]).astype(BF16)


def _diff_kernel(q_ref, k_ref, v_ref, lam_ref, sub_ref, o_ref, *, tq):
    hd = pl.program_id(1)
    i = pl.program_id(2)
    slope = jnp.float32(ALIBI_SLOPES[0])
    for k in range(1, H_B):
        slope = jnp.where(hd == k, jnp.float32(ALIBI_SLOPES[k]), slope)
    bias = _diff_diag_bias(*_diag_indices(tq, tq, 0, True), slope)
    o1, o2 = _attend_pair(q_ref, k_ref, v_ref, i, bias, tq=tq)
    o_ref[0] = _diff_finish(o1, o2, lam_ref, sub_ref)


def _diff_history_kernel(q_ref, kc_ref, vc_ref, kp_ref, vp_ref, lam_ref, sub_ref, o_ref, *, ts):
    plen = kp_ref.shape[3]
    pos = _lane_iota((1, plen)).astype(F32)
    qi, ki = _diag_indices(ts, ts, 0, False)
    for h in range(H_B):
        vp = vp_ref[0, pl.ds(h, plen, stride=H_B), :].astype(BF16)
        vc = vc_ref[0, :, DV_B * h:DV_B * (h + 1)]
        diag_bias = _diff_diag_bias(qi, ki, ALIBI_SLOPES[h])
        outs = [_attend_history(q_ref[0, j], kp_ref[0, j].astype(BF16), pos * ALIBI_SLOPES[h], vp,
                                kc_ref[0, j], vc, diag_bias, v_time_minor=False) for j in (2 * h, 2 * h + 1)]
        o_ref[0, :, DV_B * h:DV_B * (h + 1)] = _diff_finish(outs[0], outs[1], lam_ref, sub_ref)


def _attention(kind, q_aug, k_aug, v_cur, extras, tq):
    bx, _, seq, _ = k_aug.shape
    groups = HEAD_SLOTS // 2
    nblk = seq // tq
    assert v_cur.shape == (bx, nblk, groups * LANES, tq)
    v_cur = v_cur.reshape(bx, nblk, groups, LANES, tq)
    grid = (bx, groups, nblk)
    in_specs = [
        pl.BlockSpec((1, 2, LANES, tq), lambda b, g, i: (b, g, 0, i)),
        pl.BlockSpec((1, 2, seq, LANES), lambda b, g, i: (b, g, 0, 0)),
        pl.BlockSpec((1, nblk, 1, LANES, tq), lambda b, g, i: (b, 0, g, 0, 0)),
    ] + [pl.BlockSpec(a.shape, lambda b, g, i: (0,) * a.ndim) for a in extras]
    body = _fox_kernel if kind == "fox" else _diff_kernel
    return pl.pallas_call(
        functools.partial(body, tq=tq),
        grid=grid,
        in_specs=in_specs,
        out_specs=pl.BlockSpec((1, tq, LANES), lambda b, g, i: (b, i, g)),
        out_shape=jax.ShapeDtypeStruct((bx, seq, groups * LANES), BF16),
        compiler_params=pltpu.CompilerParams(
            dimension_semantics=("arbitrary", "arbitrary", "arbitrary"), vmem_limit_bytes=VMEM_LIMIT),
        name=kind + "_attn",
    )(q_aug, k_aug, v_cur, *extras)


def _history_attention(kind, q_aug, k_aug, v_cur, kt_past, v_past, extras):
    bx, _, ts, _ = q_aug.shape
    whole = lambda a: pl.BlockSpec((1,) + a.shape[1:], lambda b: (b,) + (0,) * (a.ndim - 1))
    shared = lambda a: pl.BlockSpec(a.shape, lambda b: (0,) * a.ndim)
    args = [q_aug, k_aug, v_cur, kt_past, v_past]
    in_specs = [whole(a) for a in args] + [whole(a) if kind == "fox" else shared(a) for a in extras]
    body = _fox_history_kernel if kind == "fox" else _diff_history_kernel
    return pl.pallas_call(
        functools.partial(body, ts=ts),
        grid=(bx,),
        in_specs=in_specs,
        out_specs=pl.BlockSpec((1, ts, v_cur.shape[2]), lambda b: (b, 0, 0)),
        out_shape=jax.ShapeDtypeStruct((bx, ts, v_cur.shape[2]), BF16),
        compiler_params=pltpu.CompilerParams(
            dimension_semantics=("arbitrary",), vmem_limit_bytes=VMEM_LIMIT),
        name=kind + "_history_attn",
    )(*args, *extras)


def _merge_ffn_kernel(x_ref, oa_ref, ob_ref, sig_ref, woa_ref, wob_ref, wout_ref, gffn_ref,
                      wup_ref, cw_ref, cb_ref, wdown_ref, st_ref,
                      y_ref, ns_ref, carry_ref, ubuf_ref, act_ref, *, tm, seq, d_ff):
    t = pl.program_id(0)
    carry_mode = seq >= tm
    nb = 1 if carry_mode else tm // seq
    rows = tm if carry_mode else seq
    tpb = max(1, seq // tm)
    d = x_ref.shape[1]
    seg = rows + SUBLANES

    ya = jnp.dot(oa_ref[...], woa_ref[...], preferred_element_type=F32)
    yb = jnp.dot(ob_ref[...], wob_ref[...], preferred_element_type=F32)
    m = sig_ref[:, :d].astype(F32) * ya + sig_ref[:, d:].astype(F32) * yb
    x1 = x_ref[...] + jnp.dot(m.astype(BF16), wout_ref[...], preferred_element_type=F32)
    h = (x1 * lax.rsqrt(jnp.mean(x1 * x1, axis=-1, keepdims=True) + EPS) * gffn_ref[...]).astype(BF16)

    if carry_mode:
        @pl.when(t % tpb == 0)
        def _():
            carry_ref[0:2, :] = st_ref[0]

    def up(c0):
        return (jnp.dot(h, wup_ref[:, c0:c0 + FF_CHUNK], preferred_element_type=F32),
                jnp.dot(h, wup_ref[:, d_ff + c0:d_ff + c0 + FF_CHUNK], preferred_element_type=F32))

    def conv(u, c0, buf):
        cols = slice(c0, c0 + FF_CHUNK)
        w0, w1, w2 = (cw_ref[k:k + 1, cols] for k in range(3))
        outs = []
        for bi in range(nb):
            useg = u[bi * rows:(bi + 1) * rows, :]
            r0 = bi * seg + SUBLANES
            hist = carry_ref[0:2, cols] if carry_mode else st_ref[bi, :, cols]
            ubuf_ref[buf, r0 - 2:r0, :] = hist
            ubuf_ref[buf, r0:r0 + rows, :] = useg
            ns_ref[bi, :, cols] = useg[rows - 2:rows, :]
            u1 = ubuf_ref[buf, r0 - 1:r0 - 1 + rows, :]
            u2 = ubuf_ref[buf, r0 - 2:r0 - 2 + rows, :]
            outs.append(w0 * u2 + w1 * u1 + w2 * useg + cb_ref[:, cols])
        if carry_mode:
            carry_ref[0:2, cols] = u[tm - 2:tm, :]
        return outs[0] if nb == 1 else jnp.concatenate(outs, axis=0)

    chunks = list(range(0, d_ff, FF_CHUNK))
    nxt = up(chunks[0])
    for k, c0 in enumerate(chunks):
        ua, ub = nxt
        if k + 1 < len(chunks):
            nxt = up(chunks[k + 1])
        a = conv(ua, c0, 2 * (k % 2))
        b = conv(ub, d_ff + c0, 2 * (k % 2) + 1)
        act_ref[:, c0:c0 + FF_CHUNK] = (a * jax.nn.sigmoid(a) * b).astype(BF16)
    y_ref[...] = x1 + jnp.dot(act_ref[...], wdown_ref[...], preferred_element_type=F32)


def _merge_ffn(x2, oa, ob, sig, state, seq, prm, tm):
    n, d = x2.shape
    d_ff = prm["w_down"].shape[0]
    assert d_ff % FF_CHUNK == 0 and seq >= 2
    carry_mode = seq >= tm
    nb = 1 if carry_mode else tm // seq
    rows = tm if carry_mode else seq
    tpb = max(1, seq // tm)
    bx = n // seq
    grid = (n // tm,)
    const = lambda a: pl.BlockSpec(a.shape, lambda t: (0,) * a.ndim, pipeline_mode=pl.Buffered(1))
    rowblk = lambda w: pl.BlockSpec((tm, w), lambda t: (t, 0))
    if carry_mode:
        st_spec = pl.BlockSpec((1, 2, 2 * d_ff), lambda t: (t // tpb, 0, 0))
    else:
        st_spec = pl.BlockSpec((nb, 2, 2 * d_ff), lambda t: (t, 0, 0))
    consts1 = [prm["w_oa"], prm["w_ob"], prm["w_out"], prm["g_ffn"], prm["w_up"], prm["conv_w"],
               prm["conv_b"], prm["w_down"]]
    return pl.pallas_call(
        functools.partial(_merge_ffn_kernel, tm=tm, seq=seq, d_ff=d_ff),
        grid=grid,
        in_specs=[rowblk(d), rowblk(W_A), rowblk(W_B), rowblk(2 * d)] + [const(a) for a in consts1] + [st_spec],
        out_specs=[rowblk(d), st_spec],
        out_shape=[jax.ShapeDtypeStruct((n, d), F32), jax.ShapeDtypeStruct((bx, 2, 2 * d_ff), F32)],
        scratch_shapes=[pltpu.VMEM((SUBLANES, 2 * d_ff), F32),
                        pltpu.VMEM((4, nb * (rows + SUBLANES), FF_CHUNK), F32),
                        pltpu.VMEM((tm, d_ff), BF16)],
        compiler_params=pltpu.CompilerParams(
            dimension_semantics=("arbitrary",), vmem_limit_bytes=VMEM_LIMIT),
        name="merge_ffn",
    )(x2, oa, ob, sig, *consts1, state)


def _tri(tm, seg):
    i = np.arange(tm)
    return jnp.asarray((i[None, :] <= i[:, None]) & (i[None, :] // seg == i[:, None] // seg), BF16)


def _piece_placer():
    p = np.zeros((LANES, HEAD_SLOTS * SLOT_W), np.float32)
    for k in range(N_CUM_PIECES):
        for h in range(HEAD_SLOTS):
            p[k * HEAD_SLOTS + h, h * SLOT_W + k] = -1.0
    return jnp.asarray(p, BF16)


def _block_diag_ones(width, blk):
    i = np.arange(width)
    return jnp.asarray(i[:, None] // blk == i[None, :] // blk, BF16)


def _rep3(a):
    pad = jnp.zeros(a.shape[:-1] + (LANES - N_CUM_PIECES * HEAD_SLOTS,), a.dtype)
    return jnp.concatenate([a] * N_CUM_PIECES + [pad], axis=-1)


def _tiles(seq):
    return dict(tm=512, tq=min(512, seq))


def _time_minor(a):
    return jnp.moveaxis(a, 1, -1)


def _time_major(a):
    return jnp.moveaxis(a, -1, 1)


def _layer(x, pos_off, cinit, past, state, prm, lam_pack, tiles):
    bx, seq, d = x.shape
    tm, tq = tiles["tm"], tiles["tq"]
    x2 = x.reshape(bx * seq, d)
    ka, va, logf, kb, vb, qaa, kaa, qba, kba, vab, vbb, sig = _in_proj(
        x2, seq, pos_off, cinit, prm, tm, key_major=past is None)
    diff_prm = (lam_pack, prm["subln_b"])
    if past is None:
        assert tq == tm
        oa = _attention("fox", qaa, kaa, vab, (), tq)
        ob = _attention("diff", qba, kba, vbb, diff_prm, tq)
    else:
        vab = vab.reshape(bx, seq, W_A)
        vbb = vbb.reshape(bx, seq, W_B)
        oa = _history_attention("fox", qaa, kaa, vab, past["kt_a"], past["vt_a"], (past["c"],))
        ob = _history_attention("diff", qba, kba, vbb, past["kt_b"], past["v_b"], diff_prm)
    y, new_state = _merge_ffn(x2, oa.reshape(bx * seq, W_A), ob.reshape(bx * seq, W_B), sig, state, seq, prm, tm)
    return (y.reshape(bx, seq, d), _time_major(ka), _time_major(va), _time_major(logf),
            _time_major(kb.reshape(bx, H_B, 2, DH_B, seq)), vb.reshape(bx, seq, H_B, DV_B), new_state)


def kernel(x_prompt, x_sample, cache_a_k, cache_a_v, cache_a_logf, cache_b_k, cache_b_v, state_ffn_conv,
           g_attn, w_in, b_f, qn_a, kn_a, qn_b, kn_b, lambda_q1, lambda_k1, lambda_q2, lambda_k2,
           subln_b, w_oa, w_ob, w_out, g_ffn, w_up, conv_w, conv_b, w_down):
    bp, tp_, d = x_prompt.shape
    bs, ts, _ = x_sample.shape
    plen = cache_a_k.shape[1]
    d_ff = w_down.shape[0]

    f0, f1 = 3 * W_A, 3 * W_A + H_A
    prm = {
        "g_attn": g_attn.reshape(1, d),
        "w_main": jnp.concatenate([w_in[:, :f0], w_in[:, f1:]], axis=1).astype(BF16),
        "w_f": _rep3(w_in[:, f0:f1]).astype(BF16),
        "b_f": _rep3(b_f.reshape(1, H_A)),
        "qn_a": jnp.tile(qn_a, H_A).reshape(1, W_A), "kn_a": jnp.tile(kn_a, H_A).reshape(1, W_A),
        "qn_b": jnp.tile(qn_b, 2 * H_B).reshape(1, W_B), "kn_b": jnp.tile(kn_b, 2 * H_B).reshape(1, W_B),
        "bd": _block_diag_ones(W_A, DH_A),
        "pc": _piece_placer(),
        "subln_b": subln_b.reshape(1, DV_B),
        "w_oa": w_oa.astype(BF16), "w_ob": w_ob.astype(BF16), "w_out": w_out.astype(BF16),
        "g_ffn": g_ffn.reshape(1, d), "w_up": w_up.astype(BF16), "conv_w": conv_w,
        "conv_b": conv_b.reshape(1, 2 * d_ff), "w_down": w_down.astype(BF16),
    }
    lam_pack = jnp.stack([lambda_q1, lambda_k1, lambda_q2, lambda_k2])

    tl = _tiles(tp_)
    prm_p = dict(prm, ltri_in=_tri(tl["tm"], min(tl["tm"], tp_)))
    zeros_c = jnp.zeros((bp, 1, LANES), F32)
    zeros_state = jnp.zeros((bp, 2, 2 * d_ff), F32)
    (y_p, ka_p, va_p, lf_p, kb_p, vb_p, st_p) = _layer(
        x_prompt, 0, zeros_c, None, zeros_state, prm_p, lam_pack, tl)

    tl = _tiles(ts)
    c_past, c_tot = _cache_cumsum(_time_minor(cache_a_logf).reshape(bs * H_A, plen), min(512, plen))
    past = {
        "kt_a": _time_minor(cache_a_k), "vt_a": _time_minor(cache_a_v), "c": c_past.reshape(bs, H_A, plen),
        "kt_b": _time_minor(cache_b_k).reshape(bs, HEAD_SLOTS, DH_B, plen),
        "v_b": cache_b_v.reshape(bs, plen * H_B, DV_B),
    }
    prm_s = dict(prm, ltri_in=_tri(tl["tm"], min(tl["tm"], ts)))
    cinit_rows = jnp.repeat(_rep3(c_tot[:, 0].reshape(bs, H_A)), ts, axis=0)
    (y_s, ka_s, va_s, lf_s, kb_s, vb_s, st_s) = _layer(
        x_sample, plen, cinit_rows, past, state_ffn_conv, prm_s, lam_pack, tl)

    return (y_p, y_s, ka_p, va_p, lf_p, kb_p, vb_p, st_p, ka_s, va_s, lf_s, kb_s, vb_s, st_s)
```

```python
import functools
import math

import jax
import jax.numpy as jnp
import numpy as np
from jax import lax
from jax.experimental import pallas as pl
from jax.experimental.pallas import tpu as pltpu

F32 = jnp.float32
BF16 = jnp.bfloat16

CHUNK = 64
H_A = 8
DH_A = 64
H_B = 4
DH_B = 64
DV_B = 128
W_A = H_A * DH_A
W_B = H_B * DV_B
EPS = 1e-6
LAMBDA_INIT = 0.8 - 0.6 * math.exp(-0.3 * 0)
ALIBI_SLOPES = tuple(2.0 ** (-8.0 * (i + 1) / H_B) for i in range(H_B))

LANES = 128
SUBLANES = 8
HEAD_SLOTS = 8
SLOT_W = 64
N_CUM_PIECES = 3
N_POS_PIECES = 3
NEG_BIG = -1e30
LOG2E = math.log2(math.e)
ONES_ROWS = 16
VMEM_LIMIT = 58 * 1024 * 1024
FF_CHUNK = 256


def _lane_iota(shape):
    return lax.broadcasted_iota(jnp.int32, shape, len(shape) - 1)


def _row_iota(shape):
    return lax.broadcasted_iota(jnp.int32, shape, len(shape) - 2)


def _split3(c):
    hi = c.astype(BF16).astype(F32)
    r1 = c - hi
    lo = r1.astype(BF16).astype(F32)
    lo2 = (r1 - lo).astype(BF16).astype(F32)
    return hi, lo, lo2


def _cumsum_pieces(lf, ltri_ref, base):
    hi, lo, lo2 = _split3(lf)
    ltri = ltri_ref[...]
    c = (jnp.dot(ltri, hi.astype(BF16), preferred_element_type=F32)
         + jnp.dot(ltri, lo.astype(BF16), preferred_element_type=F32)
         + jnp.dot(ltri, lo2.astype(BF16), preferred_element_type=F32)) + base
    chi, clo, clo2 = _split3(c * LOG2E)
    lane = _lane_iota(c.shape)
    pieces = jnp.where(lane < HEAD_SLOTS, chi, jnp.where(lane < 2 * HEAD_SLOTS, clo, clo2))
    return c, pieces.astype(BF16)


def _slot(z, j):
    return z[:, SLOT_W * j:SLOT_W * (j + 1)]


def _pos_extra(pos, slope):
    hi, lo, lo2 = _split3(pos.astype(F32) * (slope * LOG2E))
    lane = _lane_iota(hi.shape)
    return jnp.where(lane == 0, hi, jnp.where(lane == 1, lo, jnp.where(lane == 2, lo2, 0.0)))


def _store_tile(ref, slot64, extra64, j, nb, rows, time_minor=False):
    tile = jnp.concatenate([slot64, extra64], axis=1)
    if time_minor:
        tile_t = tile.T.astype(BF16)
        for bi in range(nb):
            ref[bi, j, :, :] = tile_t[:, bi * rows:(bi + 1) * rows]
    else:
        tile = tile.astype(BF16)
        for bi in range(nb):
            ref[bi, j, :, :] = tile[bi * rows:(bi + 1) * rows, :]


def _emit_fox_keys(kaa_ref, slots, pieces, pc_ref, nb, rows):
    extra = jnp.dot(pieces, pc_ref[...], preferred_element_type=F32)
    for j in range(HEAD_SLOTS):
        _store_tile(kaa_ref, slots[j], _slot(extra, j), j, nb, rows)


def _emit_diff_keys(kba_ref, slots, pos, nb, rows):
    for h in range(H_B):
        extra = _pos_extra(pos, ALIBI_SLOPES[h])
        for m in range(2):
            _store_tile(kba_ref, slots[2 * h + m], extra, 2 * h + m, nb, rows)


def _emit_queries(q_ref, qn, n_ones, nb, rows, time_minor):
    lane = _lane_iota((qn.shape[0], SLOT_W))
    ones = jnp.where(lane < n_ones, 1.0, 0.0)
    for j in range(HEAD_SLOTS):
        _store_tile(q_ref, _slot(qn, j), ones, j, nb, rows, time_minor)


def _cache_cumsum_kernel(lf_ref, utri_ref, c_ref, tot_ref, carry_ref, *, tb):
    t = pl.program_id(0)

    @pl.when(t == 0)
    def _():
        carry_ref[...] = jnp.zeros_like(carry_ref)

    hi, lo, lo2 = _split3(lf_ref[...])
    u = utri_ref[...]
    c = (jnp.dot(hi.astype(BF16), u, preferred_element_type=F32)
         + jnp.dot(lo.astype(BF16), u, preferred_element_type=F32)
         + jnp.dot(lo2.astype(BF16), u, preferred_element_type=F32)) + carry_ref[...]
    c_ref[...] = c
    last = c[:, tb - 1:tb]
    carry_ref[...] = last
    tot_ref[...] = jnp.broadcast_to(last, tot_ref.shape)


def _cache_cumsum(lf_t, tb):
    r, p = lf_t.shape
    i = np.arange(tb)
    utri = jnp.asarray(i[:, None] <= i[None, :], BF16)
    return pl.pallas_call(
        functools.partial(_cache_cumsum_kernel, tb=tb),
        grid=(p // tb,),
        in_specs=[pl.BlockSpec((r, tb), lambda t: (0, t)),
                  pl.BlockSpec((tb, tb), lambda t: (0, 0), pipeline_mode=pl.Buffered(1))],
        out_specs=[pl.BlockSpec((r, tb), lambda t: (0, t)), pl.BlockSpec((r, LANES), lambda t: (0, 0))],
        out_shape=[jax.ShapeDtypeStruct((r, p), F32), jax.ShapeDtypeStruct((r, LANES), F32)],
        scratch_shapes=[pltpu.VMEM((r, 1), F32)],
        compiler_params=pltpu.CompilerParams(
            dimension_semantics=("arbitrary",), vmem_limit_bytes=VMEM_LIMIT),
        name="cache_cumsum",
    )(lf_t, utri)


def _in_proj_kernel(x_ref, g_ref, wm_ref, wf_ref, bf_ref, qna_ref, kna_ref, qnb_ref, knb_ref,
                    bd_ref, ltri_ref, pc_ref, cinit_ref,
                    ka_ref, va_ref, logf_ref, kb_ref, vb_ref,
                    qaa_ref, kaa_ref, qba_ref, kba_ref, vab_ref, vbb_ref, sig_ref, carry_ref,
                    *, tm, seq, pos_off, key_major):
    t = pl.program_id(0)
    carry_mode = seq >= tm
    nb = 1 if carry_mode else tm // seq
    rows = tm if carry_mode else seq
    tpb = max(1, seq // tm)

    x = x_ref[...]
    h = (x * lax.rsqrt(jnp.mean(x * x, axis=-1, keepdims=True) + EPS) * g_ref[...]).astype(BF16)

    def proj(lo, hi):
        return jnp.dot(h, wm_ref[:, lo:hi], preferred_element_type=F32)

    def head_norm(z, w_ref):
        ss = jnp.dot((z * z).astype(BF16), bd_ref[...], preferred_element_type=F32)
        return z * lax.rsqrt(ss * (1.0 / DH_A) + EPS) * w_ref[...]

    def store_heads(ref, z, heads):
        w = z.shape[1] // heads
        for j in range(heads):
            ref[pl.ds(j, tm, stride=heads), :] = z[:, w * j:w * (j + 1)]

    def store_time_minor(ref, z):
        zts = [z[bi * rows:(bi + 1) * rows, :].T for bi in range(nb)]
        for bi, zt in enumerate(zts):
            ref[bi] = zt.reshape(z.shape[1] // SLOT_W, SLOT_W, rows)
        return zts

    def store_value_slabs(ref, vt):
        ones = jnp.ones((ONES_ROWS, tm), BF16)
        for g in range(W_A // LANES):
            ref[0, 0, g, 0:LANES, :] = vt[LANES * g:LANES * (g + 1), :].astype(BF16)
            ref[0, 0, g, LANES:LANES + ONES_ROWS, :] = ones

    zf = jnp.dot(h, wf_ref[...], preferred_element_type=F32) + bf_ref[...]
    lf = jnp.minimum(zf, 0.0) - jnp.log1p(jnp.exp(-jnp.abs(zf)))
    lane = _lane_iota(lf.shape)
    lf = jnp.where(lane < N_CUM_PIECES * HEAD_SLOTS, lf, 0.0)
    for bi in range(nb):
        logf_ref[bi] = lf[bi * rows:(bi + 1) * rows, :].T[:H_A, :]
    if carry_mode:
        @pl.when(t % tpb == 0)
        def _():
            carry_ref[...] = cinit_ref[0]
        base = carry_ref[...]
    else:
        base = cinit_ref[...]
    c, pieces = _cumsum_pieces(lf, ltri_ref, base)
    if carry_mode:
        carry_ref[...] = c[tm - 1:tm, :]

    row = t * tm + _row_iota((tm, SLOT_W))
    pos = (row & (seq - 1)) + pos_off

    qn = head_norm(proj(0, W_A), qna_ref) * (DH_A ** -0.5 * LOG2E)
    _emit_queries(qaa_ref, qn, N_CUM_PIECES, nb, rows, key_major)
    kn = head_norm(proj(W_A, 2 * W_A), kna_ref)
    store_time_minor(ka_ref, kn)
    _emit_fox_keys(kaa_ref, [_slot(kn, j) for j in range(HEAD_SLOTS)], pieces, pc_ref, nb, rows)
    v = proj(2 * W_A, 3 * W_A)
    vts = store_time_minor(va_ref, v)
    if key_major:
        store_value_slabs(vab_ref, vts[0])
    else:
        vab_ref[...] = v.astype(BF16)

    o = 3 * W_A
    qn = head_norm(proj(o, o + W_B), qnb_ref) * (DH_B ** -0.5 * LOG2E)
    _emit_queries(qba_ref, qn, N_POS_PIECES, nb, rows, key_major)
    kn = head_norm(proj(o + W_B, o + 2 * W_B), knb_ref)
    store_time_minor(kb_ref, kn)
    _emit_diff_keys(kba_ref, [_slot(kn, j) for j in range(HEAD_SLOTS)], pos, nb, rows)
    v = proj(o + 2 * W_B, o + 3 * W_B)
    store_heads(vb_ref, v, H_B)
    if key_major:
        store_value_slabs(vbb_ref, v.T)
    else:
        vbb_ref[...] = v.astype(BF16)

    o = 3 * W_A + 3 * W_B
    d = (wm_ref.shape[1] - o) // 2
    for k in range(2):
        sig_ref[:, k * d:(k + 1) * d] = jax.nn.sigmoid(proj(o + k * d, o + (k + 1) * d)).astype(BF16)


def _in_proj(x2, seq, pos_off, cinit, prm, tm, key_major):
    n, d = x2.shape
    carry_mode = seq >= tm
    nb = 1 if carry_mode else tm // seq
    tpb = max(1, seq // tm)
    bx = n // seq
    assert n % tm == 0 and seq & (seq - 1) == 0
    grid = (n // tm,)
    const = lambda a: pl.BlockSpec(a.shape, lambda t: (0,) * a.ndim, pipeline_mode=pl.Buffered(1))
    rowblk = lambda w, mult=1: pl.BlockSpec((tm * mult, w), lambda t: (t, 0))
    if carry_mode:
        cinit_spec = pl.BlockSpec((1, 1, LANES), lambda t: (t // tpb, 0, 0))
        aug_spec = pl.BlockSpec((1, HEAD_SLOTS, tm, LANES), lambda t: (t // tpb, 0, t % tpb, 0))
    else:
        cinit_spec = rowblk(LANES)
        aug_spec = pl.BlockSpec((nb, HEAD_SLOTS, seq, LANES), lambda t: (t, 0, 0, 0))
    aug_shape = jax.ShapeDtypeStruct((bx, HEAD_SLOTS, seq, LANES), BF16)
    if carry_mode:
        tmin_spec = pl.BlockSpec((1, HEAD_SLOTS, SLOT_W, tm), lambda t: (t // tpb, 0, 0, t % tpb))
        lf_spec = pl.BlockSpec((1, H_A, tm), lambda t: (t // tpb, 0, t % tpb))
    else:
        tmin_spec = pl.BlockSpec((nb, HEAD_SLOTS, SLOT_W, seq), lambda t: (t, 0, 0, 0))
        lf_spec = pl.BlockSpec((nb, H_A, seq), lambda t: (t, 0, 0))
    tmin_shape = jax.ShapeDtypeStruct((bx, HEAD_SLOTS, SLOT_W, seq), F32)
    if key_major:
        assert carry_mode
        q_spec = pl.BlockSpec((1, HEAD_SLOTS, LANES, tm), lambda t: (t // tpb, 0, 0, t % tpb))
        q_shape = jax.ShapeDtypeStruct((bx, HEAD_SLOTS, LANES, seq), BF16)
        slab = (W_A // LANES, LANES + ONES_ROWS, tm)
        v_spec = pl.BlockSpec((1, 1) + slab, lambda t: (t // tpb, t % tpb, 0, 0, 0))
        v_shape = jax.ShapeDtypeStruct((bx, tpb) + slab, BF16)
    else:
        q_spec, q_shape = aug_spec, aug_shape
        v_spec = rowblk(W_A)
        v_shape = jax.ShapeDtypeStruct((n, W_A), BF16)
    consts = [prm["g_attn"], prm["w_main"], prm["w_f"], prm["b_f"], prm["qn_a"], prm["kn_a"],
              prm["qn_b"], prm["kn_b"], prm["bd"], prm["ltri_in"], prm["pc"]]
    return pl.pallas_call(
        functools.partial(_in_proj_kernel, tm=tm, seq=seq, pos_off=pos_off, key_major=key_major),
        grid=grid,
        in_specs=[rowblk(d)] + [const(a) for a in consts] + [cinit_spec],
        out_specs=[tmin_spec, tmin_spec, lf_spec, tmin_spec, rowblk(DV_B, H_B),
                   q_spec, aug_spec, q_spec, aug_spec, v_spec, v_spec, rowblk(2 * d)],
        out_shape=[
            tmin_shape, tmin_shape, jax.ShapeDtypeStruct((bx, H_A, seq), F32),
            tmin_shape, jax.ShapeDtypeStruct((n * H_B, DV_B), F32),
            q_shape, aug_shape, q_shape, aug_shape, v_shape, v_shape,
            jax.ShapeDtypeStruct((n, 2 * d), BF16),
        ],
        scratch_shapes=[pltpu.VMEM((1, LANES), F32)],
        compiler_params=pltpu.CompilerParams(
            dimension_semantics=("arbitrary",), vmem_limit_bytes=VMEM_LIMIT),
        name="in_proj",
    )(x2, *consts, cinit)


_CONTRACT_LAST = (((1,), (1,)), ((), ()))


def _softmax_absorb(s, v, carry, v_time_minor=False):
    m, l, acc = carry
    m_new = jnp.maximum(m, jnp.max(s, axis=-1, keepdims=True))
    alpha = jnp.exp2(m - m_new)
    p = jnp.exp2(s - m_new)
    l = alpha * l + jnp.sum(p, axis=-1, keepdims=True)
    if v_time_minor:
        pv = lax.dot_general(p.astype(BF16), v, _CONTRACT_LAST, preferred_element_type=F32)
    else:
        pv = jnp.dot(p.astype(BF16), v, preferred_element_type=F32)
    return m_new, l, alpha * acc + pv


def _attend_pair(q_ref, k_ref, v_ref, i, diag_bias, *, tq):
    qs = (q_ref[0, 0], q_ref[0, 1])
    half = tq // 2
    bias_top, bias_bot = diag_bias

    def scores(c, j, n):
        k0 = j * tq
        if j < n:
            return [(jnp.dot(k_ref[0, c, k0:k0 + tq, :], qs[c], preferred_element_type=F32), 0, 0)]
        top = jnp.dot(k_ref[0, c, k0:k0 + half, :], qs[c], preferred_element_type=F32) + bias_top
        bot = jnp.dot(k_ref[0, c, k0 + half:k0 + tq, :], qs[c][:, half:], preferred_element_type=F32) + bias_bot
        return [(top, 0, 0), (bot, half, half)]

    def absorb(m, acc, piece, vt):
        s, key0, q0 = piece
        m_old, acc_old = m[:, q0:], acc[:, q0:]
        m_new = jnp.maximum(m_old, jnp.max(s, axis=0, keepdims=True))
        p = jnp.exp2(s - m_new).astype(BF16)
        pv = jnp.dot(vt[:, key0:key0 + s.shape[0]], p, preferred_element_type=F32)
        acc_new = jnp.exp2(m_old - m_new) * acc_old + pv
        if q0:
            m_new = jnp.concatenate([m[:, :q0], m_new], axis=1)
            acc_new = jnp.concatenate([acc[:, :q0], acc_new], axis=1)
        return m_new, acc_new

    def run(n):
        m = [jnp.full((1, tq), NEG_BIG, F32)] * 2
        acc = [jnp.zeros((LANES + ONES_ROWS, tq), F32)] * 2
        s = [scores(c, 0, n) for c in range(2)]
        for j in range(n + 1):
            s_next = [scores(c, j + 1, n) for c in range(2)] if j < n else None
            vt = v_ref[0, j, 0]
            for c in range(2):
                for piece in s[c]:
                    m[c], acc[c] = absorb(m[c], acc[c], piece, vt)
            s = s_next
        return [(acc[c][:LANES] / acc[c][LANES:LANES + 1]).T for c in range(2)]

    return lax.switch(i, [functools.partial(run, n) for n in range(v_ref.shape[1])])


def _attend_history(q, kt_past, bias_row, v_past, k_cur, v_cur, diag_bias, *, v_time_minor):
    ts = q.shape[0]
    init = (jnp.full((ts, 1), NEG_BIG, F32), jnp.zeros((ts, 1), F32), jnp.zeros((ts, LANES), F32))
    s = jnp.dot(q[:, :SLOT_W], kt_past, preferred_element_type=F32) + bias_row
    carry = _softmax_absorb(s, v_past, init, v_time_minor)
    s = lax.dot_general(q, k_cur, _CONTRACT_LAST, preferred_element_type=F32) + diag_bias
    m, l, acc = _softmax_absorb(s, v_cur, carry)
    return acc / l


def _diag_indices(nk, nq, k0, q0, key_major):
    shape = (nk, nq) if key_major else (nq, nk)
    rowi, coli = _row_iota(shape), _lane_iota(shape)
    return (coli + q0, rowi + k0) if key_major else (rowi + q0, coli + k0)


def _diag_pieces(tq):
    half = tq // 2
    return _diag_indices(half, tq, 0, 0, True), _diag_indices(half, half, half, half, True)


def _causal_bias(qi, ki):
    return jnp.where(ki <= qi, 0.0, NEG_BIG)


def _diff_diag_bias(qi, ki, slope):
    ahead = jnp.maximum(ki - qi, 0).astype(F32)
    visible = (ki // CHUNK) <= (qi // CHUNK)
    return jnp.where(visible, (-2.0 * LOG2E) * slope * ahead, NEG_BIG)


def _diff_finish(o1, o2, lam_ref, sub_ref):
    lq1, lk1, lq2, lk2 = (lam_ref[k:k + 1, :] for k in range(4))
    lam = (jnp.exp(jnp.sum(lq1 * lk1, axis=-1, keepdims=True))
           - jnp.exp(jnp.sum(lq2 * lk2, axis=-1, keepdims=True)) + LAMBDA_INIT)
    o = o1 - lam * o2
    o = o * lax.rsqrt(jnp.mean(o * o, axis=-1, keepdims=True) + EPS) * sub_ref[...] * (1.0 - LAMBDA_INIT)
    return o.astype(BF16)


def _fox_kernel(q_ref, k_ref, v_ref, o_ref, *, tq):
    i = pl.program_id(2)
    bias = tuple(_causal_bias(qi, ki) for qi, ki in _diag_pieces(tq))
    outs = _attend_pair(q_ref, k_ref, v_ref, i, bias, tq=tq)
    lane = _lane_iota((tq, LANES))
    o_ref[0] = jnp.where(lane < SLOT_W, outs[0], outs[1]).astype(BF16)


def _fox_history_kernel(q_ref, kc_ref, vc_ref, kp_ref, vp_ref, c_ref, o_ref, *, ts):
    causal = _causal_bias(*_diag_indices(ts, ts, 0, 0, False))
    lane = _lane_iota((ts, LANES))
    for g in range(H_A // 2):
        vt = jnp.concatenate([vp_ref[0, 2 * g], vp_ref[0, 2 * g + 1]], axis=0).astype(BF16)
        vc = vc_ref[0, :, LANES * g:LANES * (g + 1)]
        outs = [_attend_history(q_ref[0, h], kp_ref[0, h].astype(BF16), c_ref[0, h:h + 1, :] * -LOG2E, vt,
                                kc_ref[0, h], vc, causal, v_time_minor=True) for h in (2 * g, 2 * g + 1)]
        o_ref[0, :, LANES * g:LANES * (g + 1)] = jnp.where(lane < SLOT_W, outs[0], outs[1]).astype(BF16)


def _diff_kernel(q_ref, k_ref, v_ref, lam_ref, sub_ref, o_ref, *, tq):
    hd = pl.program_id(1)
    i = pl.program_id(2)
    slope = jnp.float32(ALIBI_SLOPES[0])
    for k in range(1, H_B):
        slope = jnp.where(hd == k, jnp.float32(ALIBI_SLOPES[k]), slope)
    bias = tuple(_diff_diag_bias(qi, ki, slope) for qi, ki in _diag_pieces(tq))
    o1, o2 = _attend_pair(q_ref, k_ref, v_ref, i, bias, tq=tq)
    o_ref[0] = _diff_finish(o1, o2, lam_ref, sub_ref)


def _diff_history_kernel(q_ref, kc_ref, vc_ref, kp_ref, vp_ref, lam_ref, sub_ref, o_ref, *, ts):
    plen = kp_ref.shape[3]
    pos = _lane_iota((1, plen)).astype(F32)
    qi, ki = _diag_indices(ts, ts, 0, 0, False)
    for h in range(H_B):
        vp = vp_ref[0, pl.ds(h, plen, stride=H_B), :].astype(BF16)
        vc = vc_ref[0, :, DV_B * h:DV_B * (h + 1)]
        diag_bias = _diff_diag_bias(qi, ki, ALIBI_SLOPES[h])
        outs = [_attend_history(q_ref[0, j], kp_ref[0, j].astype(BF16), pos * (ALIBI_SLOPES[h] * LOG2E), vp,
                                kc_ref[0, j], vc, diag_bias, v_time_minor=False) for j in (2 * h, 2 * h + 1)]
        o_ref[0, :, DV_B * h:DV_B * (h + 1)] = _diff_finish(outs[0], outs[1], lam_ref, sub_ref)


def _attention(kind, q_aug, k_aug, v_cur, extras, tq):
    bx, _, seq, _ = k_aug.shape
    groups = HEAD_SLOTS // 2
    nblk = seq // tq
    slab_rows = LANES + ONES_ROWS
    assert v_cur.shape == (bx, nblk, groups, slab_rows, tq)
    grid = (bx, groups, nblk)
    in_specs = [
        pl.BlockSpec((1, 2, LANES, tq), lambda b, g, i: (b, g, 0, i)),
        pl.BlockSpec((1, 2, seq, LANES), lambda b, g, i: (b, g, 0, 0)),
        pl.BlockSpec((1, nblk, 1, slab_rows, tq), lambda b, g, i: (b, 0, g, 0, 0)),
    ] + [pl.BlockSpec(a.shape, lambda b, g, i: (0,) * a.ndim) for a in extras]
    body = _fox_kernel if kind == "fox" else _diff_kernel
    return pl.pallas_call(
        functools.partial(body, tq=tq),
        grid=grid,
        in_specs=in_specs,
        out_specs=pl.BlockSpec((1, tq, LANES), lambda b, g, i: (b, i, g)),
        out_shape=jax.ShapeDtypeStruct((bx, seq, groups * LANES), BF16),
        compiler_params=pltpu.CompilerParams(
            dimension_semantics=("arbitrary", "arbitrary", "arbitrary"), vmem_limit_bytes=VMEM_LIMIT),
        name=kind + "_attn",
    )(q_aug, k_aug, v_cur, *extras)


def _history_attention(kind, q_aug, k_aug, v_cur, kt_past, v_past, extras):
    bx, _, ts, _ = q_aug.shape
    whole = lambda a: pl.BlockSpec((1,) + a.shape[1:], lambda b: (b,) + (0,) * (a.ndim - 1))
    shared = lambda a: pl.BlockSpec(a.shape, lambda b: (0,) * a.ndim)
    args = [q_aug, k_aug, v_cur, kt_past, v_past]
    in_specs = [whole(a) for a in args] + [whole(a) if kind == "fox" else shared(a) for a in extras]
    body = _fox_history_kernel if kind == "fox" else _diff_history_kernel
    return pl.pallas_call(
        functools.partial(body, ts=ts),
        grid=(bx,),
        in_specs=in_specs,
        out_specs=pl.BlockSpec((1, ts, v_cur.shape[2]), lambda b: (b, 0, 0)),
        out_shape=jax.ShapeDtypeStruct((bx, ts, v_cur.shape[2]), BF16),
        compiler_params=pltpu.CompilerParams(
            dimension_semantics=("arbitrary",), vmem_limit_bytes=VMEM_LIMIT),
        name=kind + "_history_attn",
    )(*args, *extras)


def _merge_ffn_kernel(x_ref, oa_ref, ob_ref, sig_ref, woa_ref, wob_ref, wout_ref, gffn_ref,
                      wup_ref, cw_ref, cb_ref, wdown_ref, st_ref,
                      y_ref, ns_ref, carry_ref, ubuf_ref, act_ref, *, tm, seq, d_ff):
    t = pl.program_id(0)
    carry_mode = seq >= tm
    nb = 1 if carry_mode else tm // seq
    rows = tm if carry_mode else seq
    tpb = max(1, seq // tm)
    d = x_ref.shape[1]
    seg = rows + SUBLANES

    ya = jnp.dot(oa_ref[...], woa_ref[...], preferred_element_type=F32)
    yb = jnp.dot(ob_ref[...], wob_ref[...], preferred_element_type=F32)
    m = sig_ref[:, :d].astype(F32) * ya + sig_ref[:, d:].astype(F32) * yb
    x1 = x_ref[...] + jnp.dot(m.astype(BF16), wout_ref[...], preferred_element_type=F32)
    h = (x1 * lax.rsqrt(jnp.mean(x1 * x1, axis=-1, keepdims=True) + EPS) * gffn_ref[...]).astype(BF16)

    if carry_mode:
        @pl.when(t % tpb == 0)
        def _():
            carry_ref[0:2, :] = st_ref[0]

    def up(c0):
        return (jnp.dot(h, wup_ref[:, c0:c0 + FF_CHUNK], preferred_element_type=F32),
                jnp.dot(h, wup_ref[:, d_ff + c0:d_ff + c0 + FF_CHUNK], preferred_element_type=F32))

    def conv(u, c0, buf):
        cols = slice(c0, c0 + FF_CHUNK)
        w0, w1, w2 = (cw_ref[k:k + 1, cols] for k in range(3))
        outs = []
        for bi in range(nb):
            useg = u[bi * rows:(bi + 1) * rows, :]
            r0 = bi * seg + SUBLANES
            hist = carry_ref[0:2, cols] if carry_mode else st_ref[bi, :, cols]
            ubuf_ref[buf, r0 - 2:r0, :] = hist
            ubuf_ref[buf, r0:r0 + rows, :] = useg
            ns_ref[bi, :, cols] = useg[rows - 2:rows, :]
            u1 = ubuf_ref[buf, r0 - 1:r0 - 1 + rows, :]
            u2 = ubuf_ref[buf, r0 - 2:r0 - 2 + rows, :]
            outs.append(w0 * u2 + w1 * u1 + w2 * useg + cb_ref[:, cols])
        if carry_mode:
            carry_ref[0:2, cols] = u[tm - 2:tm, :]
        return outs[0] if nb == 1 else jnp.concatenate(outs, axis=0)

    chunks = list(range(0, d_ff, FF_CHUNK))
    nxt = up(chunks[0])
    for k, c0 in enumerate(chunks):
        ua, ub = nxt
        if k + 1 < len(chunks):
            nxt = up(chunks[k + 1])
        a = conv(ua, c0, 2 * (k % 2))
        b = conv(ub, d_ff + c0, 2 * (k % 2) + 1)
        act_ref[:, c0:c0 + FF_CHUNK] = (a * jax.nn.sigmoid(a) * b).astype(BF16)
    y_ref[...] = x1 + jnp.dot(act_ref[...], wdown_ref[...], preferred_element_type=F32)


def _merge_ffn(x2, oa, ob, sig, state, seq, prm, tm):
    n, d = x2.shape
    d_ff = prm["w_down"].shape[0]
    assert d_ff % FF_CHUNK == 0 and seq >= 2
    carry_mode = seq >= tm
    nb = 1 if carry_mode else tm // seq
    rows = tm if carry_mode else seq
    tpb = max(1, seq // tm)
    bx = n // seq
    grid = (n // tm,)
    const = lambda a: pl.BlockSpec(a.shape, lambda t: (0,) * a.ndim, pipeline_mode=pl.Buffered(1))
    rowblk = lambda w: pl.BlockSpec((tm, w), lambda t: (t, 0))
    if carry_mode:
        st_spec = pl.BlockSpec((1, 2, 2 * d_ff), lambda t: (t // tpb, 0, 0))
    else:
        st_spec = pl.BlockSpec((nb, 2, 2 * d_ff), lambda t: (t, 0, 0))
    consts1 = [prm["w_oa"], prm["w_ob"], prm["w_out"], prm["g_ffn"], prm["w_up"], prm["conv_w"],
               prm["conv_b"], prm["w_down"]]
    return pl.pallas_call(
        functools.partial(_merge_ffn_kernel, tm=tm, seq=seq, d_ff=d_ff),
        grid=grid,
        in_specs=[rowblk(d), rowblk(W_A), rowblk(W_B), rowblk(2 * d)] + [const(a) for a in consts1] + [st_spec],
        out_specs=[rowblk(d), st_spec],
        out_shape=[jax.ShapeDtypeStruct((n, d), F32), jax.ShapeDtypeStruct((bx, 2, 2 * d_ff), F32)],
        scratch_shapes=[pltpu.VMEM((SUBLANES, 2 * d_ff), F32),
                        pltpu.VMEM((4, nb * (rows + SUBLANES), FF_CHUNK), F32),
                        pltpu.VMEM((tm, d_ff), BF16)],
        compiler_params=pltpu.CompilerParams(
            dimension_semantics=("arbitrary",), vmem_limit_bytes=VMEM_LIMIT),
        name="merge_ffn",
    )(x2, oa, ob, sig, *consts1, state)


def _tri(tm, seg):
    i = np.arange(tm)
    return jnp.asarray((i[None, :] <= i[:, None]) & (i[None, :] // seg == i[:, None] // seg), BF16)


def _piece_placer():
    p = np.zeros((LANES, HEAD_SLOTS * SLOT_W), np.float32)
    for k in range(N_CUM_PIECES):
        for h in range(HEAD_SLOTS):
            p[k * HEAD_SLOTS + h, h * SLOT_W + k] = -1.0
    return jnp.asarray(p, BF16)


def _block_diag_ones(width, blk):
    i = np.arange(width)
    return jnp.asarray(i[:, None] // blk == i[None, :] // blk, BF16)


def _rep3(a):
    pad = jnp.zeros(a.shape[:-1] + (LANES - N_CUM_PIECES * HEAD_SLOTS,), a.dtype)
    return jnp.concatenate([a] * N_CUM_PIECES + [pad], axis=-1)


def _tiles(seq):
    return dict(tm=512, tq=min(512, seq))


def _time_minor(a):
    return jnp.moveaxis(a, 1, -1)


def _time_major(a):
    return jnp.moveaxis(a, -1, 1)


def _layer(x, pos_off, cinit, past, state, prm, lam_pack, tiles):
    bx, seq, d = x.shape
    tm, tq = tiles["tm"], tiles["tq"]
    x2 = x.reshape(bx * seq, d)
    ka, va, logf, kb, vb, qaa, kaa, qba, kba, vab, vbb, sig = _in_proj(
        x2, seq, pos_off, cinit, prm, tm, key_major=past is None)
    diff_prm = (lam_pack, prm["subln_b"])
    if past is None:
        assert tq == tm
        oa = _attention("fox", qaa, kaa, vab, (), tq)
        ob = _attention("diff", qba, kba, vbb, diff_prm, tq)
    else:
        vab = vab.reshape(bx, seq, W_A)
        vbb = vbb.reshape(bx, seq, W_B)
        oa = _history_attention("fox", qaa, kaa, vab, past["kt_a"], past["vt_a"], (past["c"],))
        ob = _history_attention("diff", qba, kba, vbb, past["kt_b"], past["v_b"], diff_prm)
    y, new_state = _merge_ffn(x2, oa.reshape(bx * seq, W_A), ob.reshape(bx * seq, W_B), sig, state, seq, prm, tm)
    return (y.reshape(bx, seq, d), _time_major(ka), _time_major(va), _time_major(logf),
            _time_major(kb.reshape(bx, H_B, 2, DH_B, seq)), vb.reshape(bx, seq, H_B, DV_B), new_state)


def kernel(x_prompt, x_sample, cache_a_k, cache_a_v, cache_a_logf, cache_b_k, cache_b_v, state_ffn_conv,
           g_attn, w_in, b_f, qn_a, kn_a, qn_b, kn_b, lambda_q1, lambda_k1, lambda_q2, lambda_k2,
           subln_b, w_oa, w_ob, w_out, g_ffn, w_up, conv_w, conv_b, w_down):
    bp, tp_, d = x_prompt.shape
    bs, ts, _ = x_sample.shape
    plen = cache_a_k.shape[1]
    d_ff = w_down.shape[0]

    f0, f1 = 3 * W_A, 3 * W_A + H_A
    prm = {
        "g_attn": g_attn.reshape(1, d),
        "w_main": jnp.concatenate([w_in[:, :f0], w_in[:, f1:]], axis=1).astype(BF16),
        "w_f": _rep3(w_in[:, f0:f1]).astype(BF16),
        "b_f": _rep3(b_f.reshape(1, H_A)),
        "qn_a": jnp.tile(qn_a, H_A).reshape(1, W_A), "kn_a": jnp.tile(kn_a, H_A).reshape(1, W_A),
        "qn_b": jnp.tile(qn_b, 2 * H_B).reshape(1, W_B), "kn_b": jnp.tile(kn_b, 2 * H_B).reshape(1, W_B),
        "bd": _block_diag_ones(W_A, DH_A),
        "pc": _piece_placer(),
        "subln_b": subln_b.reshape(1, DV_B),
        "w_oa": w_oa.astype(BF16), "w_ob": w_ob.astype(BF16), "w_out": w_out.astype(BF16),
        "g_ffn": g_ffn.reshape(1, d), "w_up": w_up.astype(BF16), "conv_w": conv_w,
        "conv_b": conv_b.reshape(1, 2 * d_ff), "w_down": w_down.astype(BF16),
    }
    lam_pack = jnp.stack([lambda_q1, lambda_k1, lambda_q2, lambda_k2])

    tl = _tiles(tp_)
    prm_p = dict(prm, ltri_in=_tri(tl["tm"], min(tl["tm"], tp_)))
    zeros_c = jnp.zeros((bp, 1, LANES), F32)
    zeros_state = jnp.zeros((bp, 2, 2 * d_ff), F32)
    (y_p, ka_p, va_p, lf_p, kb_p, vb_p, st_p) = _layer(
        x_prompt, 0, zeros_c, None, zeros_state, prm_p, lam_pack, tl)

    tl = _tiles(ts)
    c_past, c_tot = _cache_cumsum(_time_minor(cache_a_logf).reshape(bs * H_A, plen), min(512, plen))
    past = {
        "kt_a": _time_minor(cache_a_k), "vt_a": _time_minor(cache_a_v), "c": c_past.reshape(bs, H_A, plen),
        "kt_b": _time_minor(cache_b_k).reshape(bs, HEAD_SLOTS, DH_B, plen),
        "v_b": cache_b_v.reshape(bs, plen * H_B, DV_B),
    }
    prm_s = dict(prm, ltri_in=_tri(tl["tm"], min(tl["tm"], ts)))
    cinit_rows = jnp.repeat(_rep3(c_tot[:, 0].reshape(bs, H_A)), ts, axis=0)
    (y_s, ka_s, va_s, lf_s, kb_s, vb_s, st_s) = _layer(
        x_sample, plen, cinit_rows, past, state_ffn_conv, prm_s, lam_pack, tl)

    return (y_p, y_s, ka_p, va_p, lf_p, kb_p, vb_p, st_p, ka_s, va_s, lf_s, kb_s, vb_s, st_s)
```

```python
import functools
import math

import jax
import jax.numpy as jnp
import numpy as np
from jax import lax
from jax.experimental import pallas as pl
from jax.experimental.pallas import tpu as pltpu

F32 = jnp.float32
BF16 = jnp.bfloat16

CHUNK = 64
H_A = 8
DH_A = 64
H_B = 4
DH_B = 64
DV_B = 128
W_A = H_A * DH_A
W_B = H_B * DV_B
EPS = 1e-6
LAMBDA_INIT = 0.8 - 0.6 * math.exp(-0.3 * 0)
ALIBI_SLOPES = tuple(2.0 ** (-8.0 * (i + 1) / H_B) for i in range(H_B))

LANES = 128
SUBLANES = 8
HEAD_SLOTS = 8
SLOT_W = 64
N_CUM_PIECES = 3
N_POS_PIECES = 3
NEG_BIG = -1e30
LOG2E = math.log2(math.e)
ONES_ROWS = 16
VMEM_LIMIT = 58 * 1024 * 1024
FF_CHUNK = 256


def _lane_iota(shape):
    return lax.broadcasted_iota(jnp.int32, shape, len(shape) - 1)


def _row_iota(shape):
    return lax.broadcasted_iota(jnp.int32, shape, len(shape) - 2)


def _split3(c):
    hi = c.astype(BF16).astype(F32)
    r1 = c - hi
    lo = r1.astype(BF16).astype(F32)
    lo2 = (r1 - lo).astype(BF16).astype(F32)
    return hi, lo, lo2


def _cumsum_pieces(lf, ltri_ref, base):
    hi, lo, lo2 = _split3(lf)
    ltri = ltri_ref[...]
    c = (jnp.dot(ltri, hi.astype(BF16), preferred_element_type=F32)
         + jnp.dot(ltri, lo.astype(BF16), preferred_element_type=F32)
         + jnp.dot(ltri, lo2.astype(BF16), preferred_element_type=F32)) + base
    chi, clo, clo2 = _split3(c * LOG2E)
    lane = _lane_iota(c.shape)
    pieces = jnp.where(lane < HEAD_SLOTS, chi, jnp.where(lane < 2 * HEAD_SLOTS, clo, clo2))
    return c, pieces.astype(BF16)


def _slot(z, j):
    return z[:, SLOT_W * j:SLOT_W * (j + 1)]


def _pos_extra(pos, slope):
    hi, lo, lo2 = _split3(pos.astype(F32) * (slope * LOG2E))
    lane = _lane_iota(hi.shape)
    return jnp.where(lane == 0, hi, jnp.where(lane == 1, lo, jnp.where(lane == 2, lo2, 0.0)))


def _store_tile(ref, slot64, extra64, j, nb, rows, time_minor=False):
    tile = jnp.concatenate([slot64, extra64], axis=1)
    if time_minor:
        tile_t = tile.T.astype(BF16)
        for bi in range(nb):
            ref[bi, j, :, :] = tile_t[:, bi * rows:(bi + 1) * rows]
    else:
        tile = tile.astype(BF16)
        for bi in range(nb):
            ref[bi, j, :, :] = tile[bi * rows:(bi + 1) * rows, :]


def _emit_fox_keys(kaa_ref, slots, pieces, pc_ref, nb, rows):
    extra = jnp.dot(pieces, pc_ref[...], preferred_element_type=F32)
    for j in range(HEAD_SLOTS):
        _store_tile(kaa_ref, slots[j], _slot(extra, j), j, nb, rows)


def _emit_diff_keys(kba_ref, slots, pos, nb, rows):
    for h in range(H_B):
        extra = _pos_extra(pos, ALIBI_SLOPES[h])
        for m in range(2):
            _store_tile(kba_ref, slots[2 * h + m], extra, 2 * h + m, nb, rows)


def _emit_queries(q_ref, qn, n_ones, nb, rows, time_minor):
    lane = _lane_iota((qn.shape[0], SLOT_W))
    ones = jnp.where(lane < n_ones, 1.0, 0.0)
    for j in range(HEAD_SLOTS):
        _store_tile(q_ref, _slot(qn, j), ones, j, nb, rows, time_minor)


def _cache_cumsum_kernel(lf_ref, utri_ref, c_ref, tot_ref, carry_ref, *, tb):
    t = pl.program_id(0)

    @pl.when(t == 0)
    def _():
        carry_ref[...] = jnp.zeros_like(carry_ref)

    hi, lo, lo2 = _split3(lf_ref[...])
    u = utri_ref[...]
    c = (jnp.dot(hi.astype(BF16), u, preferred_element_type=F32)
         + jnp.dot(lo.astype(BF16), u, preferred_element_type=F32)
         + jnp.dot(lo2.astype(BF16), u, preferred_element_type=F32)) + carry_ref[...]
    c_ref[...] = c
    last = c[:, tb - 1:tb]
    carry_ref[...] = last
    tot_ref[...] = jnp.broadcast_to(last, tot_ref.shape)


def _cache_cumsum(lf_t, tb):
    r, p = lf_t.shape
    i = np.arange(tb)
    utri = jnp.asarray(i[:, None] <= i[None, :], BF16)
    return pl.pallas_call(
        functools.partial(_cache_cumsum_kernel, tb=tb),
        grid=(p // tb,),
        in_specs=[pl.BlockSpec((r, tb), lambda t: (0, t)),
                  pl.BlockSpec((tb, tb), lambda t: (0, 0), pipeline_mode=pl.Buffered(1))],
        out_specs=[pl.BlockSpec((r, tb), lambda t: (0, t)), pl.BlockSpec((r, LANES), lambda t: (0, 0))],
        out_shape=[jax.ShapeDtypeStruct((r, p), F32), jax.ShapeDtypeStruct((r, LANES), F32)],
        scratch_shapes=[pltpu.VMEM((r, 1), F32)],
        compiler_params=pltpu.CompilerParams(
            dimension_semantics=("arbitrary",), vmem_limit_bytes=VMEM_LIMIT),
        name="cache_cumsum",
    )(lf_t, utri)


def _in_proj_kernel(x_ref, g_ref, wm_ref, wf_ref, bf_ref, qna_ref, kna_ref, qnb_ref, knb_ref,
                    bd_ref, ltri_ref, pc_ref, cinit_ref,
                    ka_ref, va_ref, logf_ref, kb_ref, vb_ref,
                    qaa_ref, kaa_ref, qba_ref, kba_ref, vab_ref, vbb_ref, sig_ref, carry_ref,
                    *, tm, seq, pos_off, key_major):
    t = pl.program_id(0)
    carry_mode = seq >= tm
    nb = 1 if carry_mode else tm // seq
    rows = tm if carry_mode else seq
    tpb = max(1, seq // tm)

    x = x_ref[...]
    h = (x * lax.rsqrt(jnp.mean(x * x, axis=-1, keepdims=True) + EPS) * g_ref[...]).astype(BF16)

    def proj(lo, hi):
        return jnp.dot(h, wm_ref[:, lo:hi], preferred_element_type=F32)

    def head_norm(z, w_ref):
        ss = jnp.dot((z * z).astype(BF16), bd_ref[...], preferred_element_type=F32)
        return z * lax.rsqrt(ss * (1.0 / DH_A) + EPS) * w_ref[...]

    def store_heads(ref, z, heads):
        w = z.shape[1] // heads
        for j in range(heads):
            ref[pl.ds(j, tm, stride=heads), :] = z[:, w * j:w * (j + 1)]

    def store_time_minor(ref, z):
        zts = [z[bi * rows:(bi + 1) * rows, :].T for bi in range(nb)]
        for bi, zt in enumerate(zts):
            ref[bi] = zt.reshape(z.shape[1] // SLOT_W, SLOT_W, rows)
        return zts

    def store_value_slabs(ref, vt):
        ones = jnp.ones((ONES_ROWS, tm), BF16)
        for g in range(W_A // LANES):
            ref[0, 0, g, 0:LANES, :] = vt[LANES * g:LANES * (g + 1), :].astype(BF16)
            ref[0, 0, g, LANES:LANES + ONES_ROWS, :] = ones

    zf = jnp.dot(h, wf_ref[...], preferred_element_type=F32) + bf_ref[...]
    lf = jnp.minimum(zf, 0.0) - jnp.log1p(jnp.exp(-jnp.abs(zf)))
    lane = _lane_iota(lf.shape)
    lf = jnp.where(lane < N_CUM_PIECES * HEAD_SLOTS, lf, 0.0)
    for bi in range(nb):
        logf_ref[bi] = lf[bi * rows:(bi + 1) * rows, :].T[:H_A, :]
    if carry_mode:
        @pl.when(t % tpb == 0)
        def _():
            carry_ref[...] = cinit_ref[0]
        base = carry_ref[...]
    else:
        base = cinit_ref[...]
    c, pieces = _cumsum_pieces(lf, ltri_ref, base)
    if carry_mode:
        carry_ref[...] = c[tm - 1:tm, :]

    row = t * tm + _row_iota((tm, SLOT_W))
    pos = (row & (seq - 1)) + pos_off

    qn = head_norm(proj(0, W_A), qna_ref) * (DH_A ** -0.5 * LOG2E)
    _emit_queries(qaa_ref, qn, N_CUM_PIECES, nb, rows, key_major)
    kn = head_norm(proj(W_A, 2 * W_A), kna_ref)
    store_time_minor(ka_ref, kn)
    _emit_fox_keys(kaa_ref, [_slot(kn, j) for j in range(HEAD_SLOTS)], pieces, pc_ref, nb, rows)
    v = proj(2 * W_A, 3 * W_A)
    vts = store_time_minor(va_ref, v)
    if key_major:
        store_value_slabs(vab_ref, vts[0])
    else:
        vab_ref[...] = v.astype(BF16)

    o = 3 * W_A
    qn = head_norm(proj(o, o + W_B), qnb_ref) * (DH_B ** -0.5 * LOG2E)
    _emit_queries(qba_ref, qn, N_POS_PIECES, nb, rows, key_major)
    kn = head_norm(proj(o + W_B, o + 2 * W_B), knb_ref)
    store_time_minor(kb_ref, kn)
    _emit_diff_keys(kba_ref, [_slot(kn, j) for j in range(HEAD_SLOTS)], pos, nb, rows)
    v = proj(o + 2 * W_B, o + 3 * W_B)
    store_heads(vb_ref, v, H_B)
    if key_major:
        store_value_slabs(vbb_ref, v.T)
    else:
        vbb_ref[...] = v.astype(BF16)

    o = 3 * W_A + 3 * W_B
    d = (wm_ref.shape[1] - o) // 2
    for k in range(2):
        sig_ref[:, k * d:(k + 1) * d] = jax.nn.sigmoid(proj(o + k * d, o + (k + 1) * d)).astype(BF16)


def _in_proj(x2, seq, pos_off, cinit, prm, tm, key_major):
    n, d = x2.shape
    carry_mode = seq >= tm
    nb = 1 if carry_mode else tm // seq
    tpb = max(1, seq // tm)
    bx = n // seq
    assert n % tm == 0 and seq & (seq - 1) == 0
    grid = (n // tm,)
    const = lambda a: pl.BlockSpec(a.shape, lambda t: (0,) * a.ndim, pipeline_mode=pl.Buffered(1))
    rowblk = lambda w, mult=1: pl.BlockSpec((tm * mult, w), lambda t: (t, 0))
    if carry_mode:
        cinit_spec = pl.BlockSpec((1, 1, LANES), lambda t: (t // tpb, 0, 0))
        aug_spec = pl.BlockSpec((1, HEAD_SLOTS, tm, LANES), lambda t: (t // tpb, 0, t % tpb, 0))
    else:
        cinit_spec = rowblk(LANES)
        aug_spec = pl.BlockSpec((nb, HEAD_SLOTS, seq, LANES), lambda t: (t, 0, 0, 0))
    aug_shape = jax.ShapeDtypeStruct((bx, HEAD_SLOTS, seq, LANES), BF16)
    if carry_mode:
        tmin_spec = pl.BlockSpec((1, HEAD_SLOTS, SLOT_W, tm), lambda t: (t // tpb, 0, 0, t % tpb))
        lf_spec = pl.BlockSpec((1, H_A, tm), lambda t: (t // tpb, 0, t % tpb))
    else:
        tmin_spec = pl.BlockSpec((nb, HEAD_SLOTS, SLOT_W, seq), lambda t: (t, 0, 0, 0))
        lf_spec = pl.BlockSpec((nb, H_A, seq), lambda t: (t, 0, 0))
    tmin_shape = jax.ShapeDtypeStruct((bx, HEAD_SLOTS, SLOT_W, seq), F32)
    if key_major:
        assert carry_mode
        q_spec = pl.BlockSpec((1, HEAD_SLOTS, LANES, tm), lambda t: (t // tpb, 0, 0, t % tpb))
        q_shape = jax.ShapeDtypeStruct((bx, HEAD_SLOTS, LANES, seq), BF16)
        slab = (W_A // LANES, LANES + ONES_ROWS, tm)
        v_spec = pl.BlockSpec((1, 1) + slab, lambda t: (t // tpb, t % tpb, 0, 0, 0))
        v_shape = jax.ShapeDtypeStruct((bx, tpb) + slab, BF16)
    else:
        q_spec, q_shape = aug_spec, aug_shape
        v_spec = rowblk(W_A)
        v_shape = jax.ShapeDtypeStruct((n, W_A), BF16)
    consts = [prm["g_attn"], prm["w_main"], prm["w_f"], prm["b_f"], prm["qn_a"], prm["kn_a"],
              prm["qn_b"], prm["kn_b"], prm["bd"], prm["ltri_in"], prm["pc"]]
    return pl.pallas_call(
        functools.partial(_in_proj_kernel, tm=tm, seq=seq, pos_off=pos_off, key_major=key_major),
        grid=grid,
        in_specs=[rowblk(d)] + [const(a) for a in consts] + [cinit_spec],
        out_specs=[tmin_spec, tmin_spec, lf_spec, tmin_spec, rowblk(DV_B, H_B),
                   q_spec, aug_spec, q_spec, aug_spec, v_spec, v_spec, rowblk(2 * d)],
        out_shape=[
            tmin_shape, tmin_shape, jax.ShapeDtypeStruct((bx, H_A, seq), F32),
            tmin_shape, jax.ShapeDtypeStruct((n * H_B, DV_B), F32),
            q_shape, aug_shape, q_shape, aug_shape, v_shape, v_shape,
            jax.ShapeDtypeStruct((n, 2 * d), BF16),
        ],
        scratch_shapes=[pltpu.VMEM((1, LANES), F32)],
        compiler_params=pltpu.CompilerParams(
            dimension_semantics=("arbitrary",), vmem_limit_bytes=VMEM_LIMIT),
        name="in_proj",
    )(x2, *consts, cinit)


_CONTRACT_LAST = (((1,), (1,)), ((), ()))


def _attend_pair(q_ref, k_ref, v_ref, diag_bias, emit, *, tq):
    half = tq // 2
    bias_top, bias_bot = diag_bias

    def scores(c, n, j):
        qt = q_ref[0, c, :, n * tq:(n + 1) * tq]
        k0 = j * tq
        if j < n:
            return [(jnp.dot(k_ref[0, c, k0:k0 + tq, :], qt, preferred_element_type=F32), 0, 0)]
        top = jnp.dot(k_ref[0, c, k0:k0 + half, :], qt, preferred_element_type=F32) + bias_top
        bot = jnp.dot(k_ref[0, c, k0 + half:k0 + tq, :], qt[:, half:], preferred_element_type=F32) + bias_bot
        return [(top, 0, 0), (bot, half, half)]

    def absorb(m, acc, piece, vt):
        s, key0, q0 = piece
        m_old, acc_old = m[:, q0:], acc[:, q0:]
        m_new = jnp.maximum(m_old, jnp.max(s, axis=0, keepdims=True))
        p = jnp.exp2(s - m_new).astype(BF16)
        pv = jnp.dot(vt[:, key0:key0 + s.shape[0]], p, preferred_element_type=F32)
        acc_new = jnp.exp2(m_old - m_new) * acc_old + pv
        if q0:
            m_new = jnp.concatenate([m[:, :q0], m_new], axis=1)
            acc_new = jnp.concatenate([acc[:, :q0], acc_new], axis=1)
        return m_new, acc_new

    work = [(n, j) for n in range(v_ref.shape[1]) for j in range(n + 1)]
    s = [scores(c, *work[0]) for c in range(2)]
    for t, (n, j) in enumerate(work):
        if j == 0:
            m = [jnp.full((1, tq), NEG_BIG, F32)] * 2
            acc = [jnp.zeros((LANES + ONES_ROWS, tq), F32)] * 2
        s_next = [scores(c, *work[t + 1]) for c in range(2)] if t + 1 < len(work) else None
        vt = v_ref[0, j, 0]
        for c in range(2):
            for piece in s[c]:
                m[c], acc[c] = absorb(m[c], acc[c], piece, vt)
        if j == n:
            emit(n, [(acc[c][:LANES] / acc[c][LANES:LANES + 1]).T for c in range(2)])
        s = s_next


def _history_scores(q, kt_past, bias_row, k_cur, diag_bias):
    s_past = jnp.dot(q[:, :SLOT_W], kt_past, preferred_element_type=F32) + bias_row
    s_cur = lax.dot_general(q, k_cur, _CONTRACT_LAST, preferred_element_type=F32) + diag_bias
    return s_past, s_cur


def _history_output(s_past, s_cur, v_past, v_cur, v_time_minor):
    m = jnp.maximum(jnp.max(s_past, axis=-1, keepdims=True), jnp.max(s_cur, axis=-1, keepdims=True))
    p_past = jnp.exp2(s_past - m)
    p_cur = jnp.exp2(s_cur - m)
    l = jnp.sum(p_past, axis=-1, keepdims=True) + jnp.sum(p_cur, axis=-1, keepdims=True)
    if v_time_minor:
        o = lax.dot_general(p_past.astype(BF16), v_past, _CONTRACT_LAST, preferred_element_type=F32)
    else:
        o = jnp.dot(p_past.astype(BF16), v_past, preferred_element_type=F32)
    return (o + jnp.dot(p_cur.astype(BF16), v_cur, preferred_element_type=F32)) / l


def _pipelined(n, first, second, lookahead=2):
    pending = [first(h) for h in range(min(lookahead, n))]
    outs = []
    for h in range(n):
        if h + lookahead < n:
            pending.append(first(h + lookahead))
        outs.append(second(h, pending[h]))
    return outs


def _diag_indices(nk, nq, k0, q0, key_major):
    shape = (nk, nq) if key_major else (nq, nk)
    rowi, coli = _row_iota(shape), _lane_iota(shape)
    return (coli + q0, rowi + k0) if key_major else (rowi + q0, coli + k0)


def _diag_pieces(tq):
    half = tq // 2
    return _diag_indices(half, tq, 0, 0, True), _diag_indices(half, half, half, half, True)


def _causal_bias(qi, ki):
    return jnp.where(ki <= qi, 0.0, NEG_BIG)


def _diff_diag_bias(qi, ki, slope):
    ahead = jnp.maximum(ki - qi, 0).astype(F32)
    visible = (ki // CHUNK) <= (qi // CHUNK)
    return jnp.where(visible, (-2.0 * LOG2E) * slope * ahead, NEG_BIG)


def _diff_finish(o1, o2, lam_ref, sub_ref):
    lq1, lk1, lq2, lk2 = (lam_ref[k:k + 1, :] for k in range(4))
    lam = (jnp.exp(jnp.sum(lq1 * lk1, axis=-1, keepdims=True))
           - jnp.exp(jnp.sum(lq2 * lk2, axis=-1, keepdims=True)) + LAMBDA_INIT)
    o = o1 - lam * o2
    o = o * lax.rsqrt(jnp.mean(o * o, axis=-1, keepdims=True) + EPS) * sub_ref[...] * (1.0 - LAMBDA_INIT)
    return o.astype(BF16)


def _fox_kernel(q_ref, k_ref, v_ref, o_ref, *, tq):
    bias = tuple(_causal_bias(qi, ki) for qi, ki in _diag_pieces(tq))
    lane = _lane_iota((tq, LANES))

    def emit(n, outs):
        o_ref[0, n * tq:(n + 1) * tq, :] = jnp.where(lane < SLOT_W, outs[0], outs[1]).astype(BF16)

    _attend_pair(q_ref, k_ref, v_ref, bias, emit, tq=tq)


def _fox_history_kernel(q_ref, kc_ref, vc_ref, kp_ref, vp_ref, c_ref, o_ref, *, ts):
    causal = _causal_bias(*_diag_indices(ts, ts, 0, 0, False))
    lane = _lane_iota((ts, LANES))

    def scores(h):
        return _history_scores(q_ref[0, h], kp_ref[0, h].astype(BF16), c_ref[0, h:h + 1, :] * -LOG2E,
                               kc_ref[0, h], causal)

    pair_values = {}

    def output(h, s):
        g = h // 2
        if g not in pair_values:
            pair_values[g] = jnp.concatenate([vp_ref[0, 2 * g], vp_ref[0, 2 * g + 1]], axis=0).astype(BF16)
        return _history_output(*s, pair_values[g], vc_ref[0, :, LANES * g:LANES * (g + 1)], True)

    outs = _pipelined(H_A, scores, output)
    for g in range(H_A // 2):
        o_ref[0, :, LANES * g:LANES * (g + 1)] = jnp.where(
            lane < SLOT_W, outs[2 * g], outs[2 * g + 1]).astype(BF16)


def _diff_kernel(q_ref, k_ref, v_ref, lam_ref, sub_ref, o_ref, *, tq):
    hd = pl.program_id(1)
    slope = jnp.float32(ALIBI_SLOPES[0])
    for k in range(1, H_B):
        slope = jnp.where(hd == k, jnp.float32(ALIBI_SLOPES[k]), slope)
    bias = tuple(_diff_diag_bias(qi, ki, slope) for qi, ki in _diag_pieces(tq))

    def emit(n, outs):
        o_ref[0, n * tq:(n + 1) * tq, :] = _diff_finish(outs[0], outs[1], lam_ref, sub_ref)

    _attend_pair(q_ref, k_ref, v_ref, bias, emit, tq=tq)


def _diff_history_kernel(q_ref, kc_ref, vc_ref, kp_ref, vp_ref, lam_ref, sub_ref, o_ref, *, ts):
    plen = kp_ref.shape[3]
    pos = _lane_iota((1, plen)).astype(F32)
    qi, ki = _diag_indices(ts, ts, 0, 0, False)

    def scores(j):
        slope = ALIBI_SLOPES[j // 2]
        return _history_scores(q_ref[0, j], kp_ref[0, j].astype(BF16), pos * (slope * LOG2E),
                               kc_ref[0, j], _diff_diag_bias(qi, ki, slope))

    head_values = {}

    def output(j, s):
        h = j // 2
        if h not in head_values:
            head_values[h] = vp_ref[0, pl.ds(h, plen, stride=H_B), :].astype(BF16)
        return _history_output(*s, head_values[h], vc_ref[0, :, DV_B * h:DV_B * (h + 1)], False)

    outs = _pipelined(HEAD_SLOTS, scores, output)
    for h in range(H_B):
        o_ref[0, :, DV_B * h:DV_B * (h + 1)] = _diff_finish(outs[2 * h], outs[2 * h + 1], lam_ref, sub_ref)


def _attention(kind, q_aug, k_aug, v_cur, extras, tq):
    bx, _, seq, _ = k_aug.shape
    groups = HEAD_SLOTS // 2
    nblk = seq // tq
    slab_rows = LANES + ONES_ROWS
    assert v_cur.shape == (bx, nblk, groups, slab_rows, tq)
    grid = (bx, groups)
    in_specs = [
        pl.BlockSpec((1, 2, LANES, seq), lambda b, g: (b, g, 0, 0)),
        pl.BlockSpec((1, 2, seq, LANES), lambda b, g: (b, g, 0, 0)),
        pl.BlockSpec((1, nblk, 1, slab_rows, tq), lambda b, g: (b, 0, g, 0, 0)),
    ] + [pl.BlockSpec(a.shape, lambda b, g: (0,) * a.ndim) for a in extras]
    body = _fox_kernel if kind == "fox" else _diff_kernel
    return pl.pallas_call(
        functools.partial(body, tq=tq),
        grid=grid,
        in_specs=in_specs,
        out_specs=pl.BlockSpec((1, seq, LANES), lambda b, g: (b, 0, g)),
        out_shape=jax.ShapeDtypeStruct((bx, seq, groups * LANES), BF16),
        compiler_params=pltpu.CompilerParams(
            dimension_semantics=("arbitrary", "arbitrary"), vmem_limit_bytes=VMEM_LIMIT),
        name=kind + "_attn",
    )(q_aug, k_aug, v_cur, *extras)


def _history_attention(kind, q_aug, k_aug, v_cur, kt_past, v_past, extras):
    bx, _, ts, _ = q_aug.shape
    whole = lambda a: pl.BlockSpec((1,) + a.shape[1:], lambda b: (b,) + (0,) * (a.ndim - 1))
    shared = lambda a: pl.BlockSpec(a.shape, lambda b: (0,) * a.ndim)
    args = [q_aug, k_aug, v_cur, kt_past, v_past]
    in_specs = [whole(a) for a in args] + [whole(a) if kind == "fox" else shared(a) for a in extras]
    body = _fox_history_kernel if kind == "fox" else _diff_history_kernel
    return pl.pallas_call(
        functools.partial(body, ts=ts),
        grid=(bx,),
        in_specs=in_specs,
        out_specs=pl.BlockSpec((1, ts, v_cur.shape[2]), lambda b: (b, 0, 0)),
        out_shape=jax.ShapeDtypeStruct((bx, ts, v_cur.shape[2]), BF16),
        compiler_params=pltpu.CompilerParams(
            dimension_semantics=("arbitrary",), vmem_limit_bytes=VMEM_LIMIT),
        name=kind + "_history_attn",
    )(*args, *extras)


def _merge_ffn_kernel(x_ref, oa_ref, ob_ref, sig_ref, woa_ref, wob_ref, wout_ref, gffn_ref,
                      wup_ref, cw_ref, cb_ref, wdown_ref, st_ref,
                      y_ref, ns_ref, carry_ref, ubuf_ref, act_ref, *, tm, seq, d_ff):
    t = pl.program_id(0)
    carry_mode = seq >= tm
    nb = 1 if carry_mode else tm // seq
    rows = tm if carry_mode else seq
    tpb = max(1, seq // tm)
    d = x_ref.shape[1]
    seg = rows + SUBLANES

    ya = jnp.dot(oa_ref[...], woa_ref[...], preferred_element_type=F32)
    yb = jnp.dot(ob_ref[...], wob_ref[...], preferred_element_type=F32)
    m = sig_ref[:, :d].astype(F32) * ya + sig_ref[:, d:].astype(F32) * yb
    x1 = x_ref[...] + jnp.dot(m.astype(BF16), wout_ref[...], preferred_element_type=F32)
    h = (x1 * lax.rsqrt(jnp.mean(x1 * x1, axis=-1, keepdims=True) + EPS) * gffn_ref[...]).astype(BF16)

    if carry_mode:
        @pl.when(t % tpb == 0)
        def _():
            carry_ref[0:2, :] = st_ref[0]

    def up(c0):
        return (jnp.dot(h, wup_ref[:, c0:c0 + FF_CHUNK], preferred_element_type=F32),
                jnp.dot(h, wup_ref[:, d_ff + c0:d_ff + c0 + FF_CHUNK], preferred_element_type=F32))

    def conv(u, c0, buf):
        cols = slice(c0, c0 + FF_CHUNK)
        w0, w1, w2 = (cw_ref[k:k + 1, cols] for k in range(3))
        outs = []
        for bi in range(nb):
            useg = u[bi * rows:(bi + 1) * rows, :]
            r0 = bi * seg + SUBLANES
            hist = carry_ref[0:2, cols] if carry_mode else st_ref[bi, :, cols]
            ubuf_ref[buf, r0 - 2:r0, :] = hist
            ubuf_ref[buf, r0:r0 + rows, :] = useg
            ns_ref[bi, :, cols] = useg[rows - 2:rows, :]
            u1 = ubuf_ref[buf, r0 - 1:r0 - 1 + rows, :]
            u2 = ubuf_ref[buf, r0 - 2:r0 - 2 + rows, :]
            outs.append(w0 * u2 + w1 * u1 + w2 * useg + cb_ref[:, cols])
        if carry_mode:
            carry_ref[0:2, cols] = u[tm - 2:tm, :]
        return outs[0] if nb == 1 else jnp.concatenate(outs, axis=0)

    chunks = list(range(0, d_ff, FF_CHUNK))
    nxt = up(chunks[0])
    for k, c0 in enumerate(chunks):
        ua, ub = nxt
        if k + 1 < len(chunks):
            nxt = up(chunks[k + 1])
        a = conv(ua, c0, 2 * (k % 2))
        b = conv(ub, d_ff + c0, 2 * (k % 2) + 1)
        act_ref[:, c0:c0 + FF_CHUNK] = (a * jax.nn.sigmoid(a) * b).astype(BF16)
    y_ref[...] = x1 + jnp.dot(act_ref[...], wdown_ref[...], preferred_element_type=F32)


def _merge_ffn(x2, oa, ob, sig, state, seq, prm, tm):
    n, d = x2.shape
    d_ff = prm["w_down"].shape[0]
    assert d_ff % FF_CHUNK == 0 and seq >= 2
    carry_mode = seq >= tm
    nb = 1 if carry_mode else tm // seq
    rows = tm if carry_mode else seq
    tpb = max(1, seq // tm)
    bx = n // seq
    grid = (n // tm,)
    const = lambda a: pl.BlockSpec(a.shape, lambda t: (0,) * a.ndim, pipeline_mode=pl.Buffered(1))
    rowblk = lambda w: pl.BlockSpec((tm, w), lambda t: (t, 0))
    if carry_mode:
        st_spec = pl.BlockSpec((1, 2, 2 * d_ff), lambda t: (t // tpb, 0, 0))
    else:
        st_spec = pl.BlockSpec((nb, 2, 2 * d_ff), lambda t: (t, 0, 0))
    consts1 = [prm["w_oa"], prm["w_ob"], prm["w_out"], prm["g_ffn"], prm["w_up"], prm["conv_w"],
               prm["conv_b"], prm["w_down"]]
    return pl.pallas_call(
        functools.partial(_merge_ffn_kernel, tm=tm, seq=seq, d_ff=d_ff),
        grid=grid,
        in_specs=[rowblk(d), rowblk(W_A), rowblk(W_B), rowblk(2 * d)] + [const(a) for a in consts1] + [st_spec],
        out_specs=[rowblk(d), st_spec],
        out_shape=[jax.ShapeDtypeStruct((n, d), F32), jax.ShapeDtypeStruct((bx, 2, 2 * d_ff), F32)],
        scratch_shapes=[pltpu.VMEM((SUBLANES, 2 * d_ff), F32),
                        pltpu.VMEM((4, nb * (rows + SUBLANES), FF_CHUNK), F32),
                        pltpu.VMEM((tm, d_ff), BF16)],
        compiler_params=pltpu.CompilerParams(
            dimension_semantics=("arbitrary",), vmem_limit_bytes=VMEM_LIMIT),
        name="merge_ffn",
    )(x2, oa, ob, sig, *consts1, state)


def _tri(tm, seg):
    i = np.arange(tm)
    return jnp.asarray((i[None, :] <= i[:, None]) & (i[None, :] // seg == i[:, None] // seg), BF16)


def _piece_placer():
    p = np.zeros((LANES, HEAD_SLOTS * SLOT_W), np.float32)
    for k in range(N_CUM_PIECES):
        for h in range(HEAD_SLOTS):
            p[k * HEAD_SLOTS + h, h * SLOT_W + k] = -1.0
    return jnp.asarray(p, BF16)


def _block_diag_ones(width, blk):
    i = np.arange(width)
    return jnp.asarray(i[:, None] // blk == i[None, :] // blk, BF16)


def _rep3(a):
    pad = jnp.zeros(a.shape[:-1] + (LANES - N_CUM_PIECES * HEAD_SLOTS,), a.dtype)
    return jnp.concatenate([a] * N_CUM_PIECES + [pad], axis=-1)


def _tiles(seq):
    return dict(tm=512, tq=min(512, seq))


def _time_minor(a):
    return jnp.moveaxis(a, 1, -1)


def _time_major(a):
    return jnp.moveaxis(a, -1, 1)


def _layer(x, pos_off, cinit, past, state, prm, lam_pack, tiles):
    bx, seq, d = x.shape
    tm, tq = tiles["tm"], tiles["tq"]
    x2 = x.reshape(bx * seq, d)
    ka, va, logf, kb, vb, qaa, kaa, qba, kba, vab, vbb, sig = _in_proj(
        x2, seq, pos_off, cinit, prm, tm, key_major=past is None)
    diff_prm = (lam_pack, prm["subln_b"])
    if past is None:
        assert tq == tm
        oa = _attention("fox", qaa, kaa, vab, (), tq)
        ob = _attention("diff", qba, kba, vbb, diff_prm, tq)
    else:
        vab = vab.reshape(bx, seq, W_A)
        vbb = vbb.reshape(bx, seq, W_B)
        oa = _history_attention("fox", qaa, kaa, vab, past["kt_a"], past["vt_a"], (past["c"],))
        ob = _history_attention("diff", qba, kba, vbb, past["kt_b"], past["v_b"], diff_prm)
    y, new_state = _merge_ffn(x2, oa.reshape(bx * seq, W_A), ob.reshape(bx * seq, W_B), sig, state, seq, prm, tm)
    return (y.reshape(bx, seq, d), _time_major(ka), _time_major(va), _time_major(logf),
            _time_major(kb.reshape(bx, H_B, 2, DH_B, seq)), vb.reshape(bx, seq, H_B, DV_B), new_state)


def kernel(x_prompt, x_sample, cache_a_k, cache_a_v, cache_a_logf, cache_b_k, cache_b_v, state_ffn_conv,
           g_attn, w_in, b_f, qn_a, kn_a, qn_b, kn_b, lambda_q1, lambda_k1, lambda_q2, lambda_k2,
           subln_b, w_oa, w_ob, w_out, g_ffn, w_up, conv_w, conv_b, w_down):
    bp, tp_, d = x_prompt.shape
    bs, ts, _ = x_sample.shape
    plen = cache_a_k.shape[1]
    d_ff = w_down.shape[0]

    f0, f1 = 3 * W_A, 3 * W_A + H_A
    prm = {
        "g_attn": g_attn.reshape(1, d),
        "w_main": jnp.concatenate([w_in[:, :f0], w_in[:, f1:]], axis=1).astype(BF16),
        "w_f": _rep3(w_in[:, f0:f1]).astype(BF16),
        "b_f": _rep3(b_f.reshape(1, H_A)),
        "qn_a": jnp.tile(qn_a, H_A).reshape(1, W_A), "kn_a": jnp.tile(kn_a, H_A).reshape(1, W_A),
        "qn_b": jnp.tile(qn_b, 2 * H_B).reshape(1, W_B), "kn_b": jnp.tile(kn_b, 2 * H_B).reshape(1, W_B),
        "bd": _block_diag_ones(W_A, DH_A),
        "pc": _piece_placer(),
        "subln_b": subln_b.reshape(1, DV_B),
        "w_oa": w_oa.astype(BF16), "w_ob": w_ob.astype(BF16), "w_out": w_out.astype(BF16),
        "g_ffn": g_ffn.reshape(1, d), "w_up": w_up.astype(BF16), "conv_w": conv_w,
        "conv_b": conv_b.reshape(1, 2 * d_ff), "w_down": w_down.astype(BF16),
    }
    lam_pack = jnp.stack([lambda_q1, lambda_k1, lambda_q2, lambda_k2])

    tl = _tiles(tp_)
    prm_p = dict(prm, ltri_in=_tri(tl["tm"], min(tl["tm"], tp_)))
    zeros_c = jnp.zeros((bp, 1, LANES), F32)
    zeros_state = jnp.zeros((bp, 2, 2 * d_ff), F32)
    (y_p, ka_p, va_p, lf_p, kb_p, vb_p, st_p) = _layer(
        x_prompt, 0, zeros_c, None, zeros_state, prm_p, lam_pack, tl)

    tl = _tiles(ts)
    c_past, c_tot = _cache_cumsum(_time_minor(cache_a_logf).reshape(bs * H_A, plen), min(512, plen))
    past = {
        "kt_a": _time_minor(cache_a_k), "vt_a": _time_minor(cache_a_v), "c": c_past.reshape(bs, H_A, plen),
        "kt_b": _time_minor(cache_b_k).reshape(bs, HEAD_SLOTS, DH_B, plen),
        "v_b": cache_b_v.reshape(bs, plen * H_B, DV_B),
    }
    prm_s = dict(prm, ltri_in=_tri(tl["tm"], min(tl["tm"], ts)))
    cinit_rows = jnp.repeat(_rep3(c_tot[:, 0].reshape(bs, H_A)), ts, axis=0)
    (y_s, ka_s, va_s, lf_s, kb_s, vb_s, st_s) = _layer(
        x_sample, plen, cinit_rows, past, state_ffn_conv, prm_s, lam_pack, tl)

    return (y_p, y_s, ka_p, va_p, lf_p, kb_p, vb_p, st_p, ka_s, va_s, lf_s, kb_s, vb_s, st_s)
```

```python
import functools
import math

import jax
import jax.numpy as jnp
import numpy as np
from jax import lax
from jax.experimental import pallas as pl
from jax.experimental.pallas import tpu as pltpu

F32 = jnp.float32
BF16 = jnp.bfloat16

CHUNK = 64
H_A = 8
DH_A = 64
H_B = 4
DH_B = 64
DV_B = 128
W_A = H_A * DH_A
W_B = H_B * DV_B
EPS = 1e-6
LAMBDA_INIT = 0.8 - 0.6 * math.exp(-0.3 * 0)
ALIBI_SLOPES = tuple(2.0 ** (-8.0 * (i + 1) / H_B) for i in range(H_B))

LANES = 128
SUBLANES = 8
HEAD_SLOTS = 8
SLOT_W = 64
N_CUM_PIECES = 3
N_POS_PIECES = 3
NEG_BIG = -1e30
LOG2E = math.log2(math.e)
ONES_ROWS = 16
VMEM_LIMIT = 58 * 1024 * 1024
FF_CHUNK = 256


def _lane_iota(shape):
    return lax.broadcasted_iota(jnp.int32, shape, len(shape) - 1)


def _row_iota(shape):
    return lax.broadcasted_iota(jnp.int32, shape, len(shape) - 2)


def _split3(c):
    hi = c.astype(BF16).astype(F32)
    r1 = c - hi
    lo = r1.astype(BF16).astype(F32)
    lo2 = (r1 - lo).astype(BF16).astype(F32)
    return hi, lo, lo2


def _cumsum_pieces(lf, ltri_ref, base):
    hi, lo, lo2 = _split3(lf)
    ltri = ltri_ref[...]
    c = (jnp.dot(ltri, hi.astype(BF16), preferred_element_type=F32)
         + jnp.dot(ltri, lo.astype(BF16), preferred_element_type=F32)
         + jnp.dot(ltri, lo2.astype(BF16), preferred_element_type=F32)) + base
    chi, clo, clo2 = _split3(c * LOG2E)
    lane = _lane_iota(c.shape)
    pieces = jnp.where(lane < HEAD_SLOTS, chi, jnp.where(lane < 2 * HEAD_SLOTS, clo, clo2))
    return c, pieces.astype(BF16)


def _slot(z, j):
    return z[:, SLOT_W * j:SLOT_W * (j + 1)]


def _pos_extra(pos, slope):
    hi, lo, lo2 = _split3(pos.astype(F32) * (slope * LOG2E))
    lane = _lane_iota(hi.shape)
    return jnp.where(lane == 0, hi, jnp.where(lane == 1, lo, jnp.where(lane == 2, lo2, 0.0)))


def _store_tile(ref, slot64, extra64, j, nb, rows, time_minor=False):
    tile = jnp.concatenate([slot64, extra64], axis=1)
    if time_minor:
        tile_t = tile.T.astype(BF16)
        for bi in range(nb):
            ref[bi, j, :, :] = tile_t[:, bi * rows:(bi + 1) * rows]
    else:
        tile = tile.astype(BF16)
        for bi in range(nb):
            ref[bi, j, :, :] = tile[bi * rows:(bi + 1) * rows, :]


def _emit_fox_keys(kaa_ref, slots, pieces, pc_ref, nb, rows):
    extra = jnp.dot(pieces, pc_ref[...], preferred_element_type=F32)
    for j in range(HEAD_SLOTS):
        _store_tile(kaa_ref, slots[j], _slot(extra, j), j, nb, rows)


def _emit_diff_keys(kba_ref, slots, pos, nb, rows):
    for h in range(H_B):
        extra = _pos_extra(pos, ALIBI_SLOPES[h])
        for m in range(2):
            _store_tile(kba_ref, slots[2 * h + m], extra, 2 * h + m, nb, rows)


def _emit_queries(q_ref, qn, n_ones, nb, rows, time_minor):
    lane = _lane_iota((qn.shape[0], SLOT_W))
    ones = jnp.where(lane < n_ones, 1.0, 0.0)
    for j in range(HEAD_SLOTS):
        _store_tile(q_ref, _slot(qn, j), ones, j, nb, rows, time_minor)


def _cache_cumsum_kernel(lf_ref, utri_ref, c_ref, tot_ref, carry_ref, *, tb):
    t = pl.program_id(0)

    @pl.when(t == 0)
    def _():
        carry_ref[...] = jnp.zeros_like(carry_ref)

    hi, lo, lo2 = _split3(lf_ref[...])
    u = utri_ref[...]
    c = (jnp.dot(hi.astype(BF16), u, preferred_element_type=F32)
         + jnp.dot(lo.astype(BF16), u, preferred_element_type=F32)
         + jnp.dot(lo2.astype(BF16), u, preferred_element_type=F32)) + carry_ref[...]
    c_ref[...] = c
    last = c[:, tb - 1:tb]
    carry_ref[...] = last
    tot_ref[...] = jnp.broadcast_to(last, tot_ref.shape)


def _cache_cumsum(lf_t, tb):
    r, p = lf_t.shape
    i = np.arange(tb)
    utri = jnp.asarray(i[:, None] <= i[None, :], BF16)
    return pl.pallas_call(
        functools.partial(_cache_cumsum_kernel, tb=tb),
        grid=(p // tb,),
        in_specs=[pl.BlockSpec((r, tb), lambda t: (0, t)),
                  pl.BlockSpec((tb, tb), lambda t: (0, 0), pipeline_mode=pl.Buffered(1))],
        out_specs=[pl.BlockSpec((r, tb), lambda t: (0, t)), pl.BlockSpec((r, LANES), lambda t: (0, 0))],
        out_shape=[jax.ShapeDtypeStruct((r, p), F32), jax.ShapeDtypeStruct((r, LANES), F32)],
        scratch_shapes=[pltpu.VMEM((r, 1), F32)],
        compiler_params=pltpu.CompilerParams(
            dimension_semantics=("arbitrary",), vmem_limit_bytes=VMEM_LIMIT),
        name="cache_cumsum",
    )(lf_t, utri)


def _in_proj_kernel(x_ref, g_ref, wm_ref, wf_ref, bf_ref, qna_ref, kna_ref, qnb_ref, knb_ref,
                    bd_ref, ltri_ref, pc_ref, cinit_ref,
                    ka_ref, va_ref, logf_ref, kb_ref, vb_ref,
                    qaa_ref, kaa_ref, qba_ref, kba_ref, vab_ref, vbb_ref, sig_ref, carry_ref,
                    *, tm, seq, pos_off, key_major):
    t = pl.program_id(0)
    carry_mode = seq >= tm
    nb = 1 if carry_mode else tm // seq
    rows = tm if carry_mode else seq
    tpb = max(1, seq // tm)

    x = x_ref[...]
    h = (x * lax.rsqrt(jnp.mean(x * x, axis=-1, keepdims=True) + EPS) * g_ref[...]).astype(BF16)

    def proj(lo, hi):
        return jnp.dot(h, wm_ref[:, lo:hi], preferred_element_type=F32)

    def head_norm(z, w_ref):
        z2 = (z * z).astype(BF16)
        bw = bd_ref.shape[0]
        ss = jnp.concatenate([jnp.dot(z2[:, k:k + bw], bd_ref[...], preferred_element_type=F32)
                              for k in range(0, z.shape[1], bw)], axis=1)
        return z * lax.rsqrt(ss * (1.0 / DH_A) + EPS) * w_ref[...]

    def store_heads(ref, z, heads):
        w = z.shape[1] // heads
        for j in range(heads):
            ref[pl.ds(j, tm, stride=heads), :] = z[:, w * j:w * (j + 1)]

    def store_time_minor(ref, z):
        zts = [z[bi * rows:(bi + 1) * rows, :].T for bi in range(nb)]
        for bi, zt in enumerate(zts):
            ref[bi] = zt.reshape(z.shape[1] // SLOT_W, SLOT_W, rows)
        return zts

    def store_value_slabs(ref, vt):
        ones = jnp.ones((ONES_ROWS, tm), BF16)
        for g in range(W_A // LANES):
            ref[0, 0, g, 0:LANES, :] = vt[LANES * g:LANES * (g + 1), :].astype(BF16)
            ref[0, 0, g, LANES:LANES + ONES_ROWS, :] = ones

    zf = jnp.dot(h, wf_ref[...], preferred_element_type=F32) + bf_ref[...]
    lf = jnp.minimum(zf, 0.0) - jnp.log1p(jnp.exp(-jnp.abs(zf)))
    lane = _lane_iota(lf.shape)
    lf = jnp.where(lane < N_CUM_PIECES * HEAD_SLOTS, lf, 0.0)
    for bi in range(nb):
        logf_ref[bi] = lf[bi * rows:(bi + 1) * rows, :].T[:H_A, :]
    if carry_mode:
        @pl.when(t % tpb == 0)
        def _():
            carry_ref[...] = cinit_ref[0]
        base = carry_ref[...]
    else:
        base = cinit_ref[...]
    c, pieces = _cumsum_pieces(lf, ltri_ref, base)
    if carry_mode:
        carry_ref[...] = c[tm - 1:tm, :]

    row = t * tm + _row_iota((tm, SLOT_W))
    pos = (row & (seq - 1)) + pos_off

    qn = head_norm(proj(0, W_A), qna_ref) * (DH_A ** -0.5 * LOG2E)
    _emit_queries(qaa_ref, qn, N_CUM_PIECES, nb, rows, key_major)
    kn = head_norm(proj(W_A, 2 * W_A), kna_ref)
    store_time_minor(ka_ref, kn)
    _emit_fox_keys(kaa_ref, [_slot(kn, j) for j in range(HEAD_SLOTS)], pieces, pc_ref, nb, rows)
    v = proj(2 * W_A, 3 * W_A)
    vts = store_time_minor(va_ref, v)
    if key_major:
        store_value_slabs(vab_ref, vts[0])
    else:
        vab_ref[...] = v.astype(BF16)

    o = 3 * W_A
    qn = head_norm(proj(o, o + W_B), qnb_ref) * (DH_B ** -0.5 * LOG2E)
    _emit_queries(qba_ref, qn, N_POS_PIECES, nb, rows, key_major)
    kn = head_norm(proj(o + W_B, o + 2 * W_B), knb_ref)
    store_time_minor(kb_ref, kn)
    _emit_diff_keys(kba_ref, [_slot(kn, j) for j in range(HEAD_SLOTS)], pos, nb, rows)
    v = proj(o + 2 * W_B, o + 3 * W_B)
    store_heads(vb_ref, v, H_B)
    if key_major:
        store_value_slabs(vbb_ref, v.T)
    else:
        vbb_ref[...] = v.astype(BF16)

    o = 3 * W_A + 3 * W_B
    d = (wm_ref.shape[1] - o) // 2
    for k in range(2):
        sig_ref[:, k * d:(k + 1) * d] = jax.nn.sigmoid(proj(o + k * d, o + (k + 1) * d)).astype(BF16)


def _in_proj(x2, seq, pos_off, cinit, prm, tm, key_major):
    n, d = x2.shape
    carry_mode = seq >= tm
    nb = 1 if carry_mode else tm // seq
    tpb = max(1, seq // tm)
    bx = n // seq
    assert n % tm == 0 and seq & (seq - 1) == 0
    grid = (n // tm,)
    const = lambda a: pl.BlockSpec(a.shape, lambda t: (0,) * a.ndim, pipeline_mode=pl.Buffered(1))
    rowblk = lambda w, mult=1: pl.BlockSpec((tm * mult, w), lambda t: (t, 0))
    if carry_mode:
        cinit_spec = pl.BlockSpec((1, 1, LANES), lambda t: (t // tpb, 0, 0))
        aug_spec = pl.BlockSpec((1, HEAD_SLOTS, tm, LANES), lambda t: (t // tpb, 0, t % tpb, 0))
    else:
        cinit_spec = rowblk(LANES)
        aug_spec = pl.BlockSpec((nb, HEAD_SLOTS, seq, LANES), lambda t: (t, 0, 0, 0))
    aug_shape = jax.ShapeDtypeStruct((bx, HEAD_SLOTS, seq, LANES), BF16)
    if carry_mode:
        tmin_spec = pl.BlockSpec((1, HEAD_SLOTS, SLOT_W, tm), lambda t: (t // tpb, 0, 0, t % tpb))
        lf_spec = pl.BlockSpec((1, H_A, tm), lambda t: (t // tpb, 0, t % tpb))
    else:
        tmin_spec = pl.BlockSpec((nb, HEAD_SLOTS, SLOT_W, seq), lambda t: (t, 0, 0, 0))
        lf_spec = pl.BlockSpec((nb, H_A, seq), lambda t: (t, 0, 0))
    tmin_shape = jax.ShapeDtypeStruct((bx, HEAD_SLOTS, SLOT_W, seq), F32)
    if key_major:
        assert carry_mode
        q_spec = pl.BlockSpec((1, HEAD_SLOTS, LANES, tm), lambda t: (t // tpb, 0, 0, t % tpb))
        q_shape = jax.ShapeDtypeStruct((bx, HEAD_SLOTS, LANES, seq), BF16)
        slab = (W_A // LANES, LANES + ONES_ROWS, tm)
        v_spec = pl.BlockSpec((1, 1) + slab, lambda t: (t // tpb, t % tpb, 0, 0, 0))
        v_shape = jax.ShapeDtypeStruct((bx, tpb) + slab, BF16)
    else:
        q_spec, q_shape = aug_spec, aug_shape
        v_spec = rowblk(W_A)
        v_shape = jax.ShapeDtypeStruct((n, W_A), BF16)
    consts = [prm["g_attn"], prm["w_main"], prm["w_f"], prm["b_f"], prm["qn_a"], prm["kn_a"],
              prm["qn_b"], prm["kn_b"], prm["bd"], prm["ltri_in"], prm["pc"]]
    return pl.pallas_call(
        functools.partial(_in_proj_kernel, tm=tm, seq=seq, pos_off=pos_off, key_major=key_major),
        grid=grid,
        in_specs=[rowblk(d)] + [const(a) for a in consts] + [cinit_spec],
        out_specs=[tmin_spec, tmin_spec, lf_spec, tmin_spec, rowblk(DV_B, H_B),
                   q_spec, aug_spec, q_spec, aug_spec, v_spec, v_spec, rowblk(2 * d)],
        out_shape=[
            tmin_shape, tmin_shape, jax.ShapeDtypeStruct((bx, H_A, seq), F32),
            tmin_shape, jax.ShapeDtypeStruct((n * H_B, DV_B), F32),
            q_shape, aug_shape, q_shape, aug_shape, v_shape, v_shape,
            jax.ShapeDtypeStruct((n, 2 * d), BF16),
        ],
        scratch_shapes=[pltpu.VMEM((1, LANES), F32)],
        compiler_params=pltpu.CompilerParams(
            dimension_semantics=("arbitrary",), vmem_limit_bytes=VMEM_LIMIT),
        name="in_proj",
    )(x2, *consts, cinit)


_CONTRACT_LAST = (((1,), (1,)), ((), ()))


def _attend_pair(q_ref, k_ref, v_ref, diag_bias, emit, *, tq):
    half = tq // 2
    bias_top, bias_bot = diag_bias

    def scores(c, n, j):
        qt = q_ref[0, c, :, n * tq:(n + 1) * tq]
        k0 = j * tq
        if j < n:
            return [(jnp.dot(k_ref[0, c, k0:k0 + tq, :], qt, preferred_element_type=F32), 0, 0)]
        top = jnp.dot(k_ref[0, c, k0:k0 + half, :], qt, preferred_element_type=F32) + bias_top
        bot = jnp.dot(k_ref[0, c, k0 + half:k0 + tq, :], qt[:, half:], preferred_element_type=F32) + bias_bot
        return [(top, 0, 0), (bot, half, half)]

    def absorb(m, acc, piece, vt):
        s, key0, q0 = piece
        m_old, acc_old = m[:, q0:], acc[:, q0:]
        m_new = jnp.maximum(m_old, jnp.max(s, axis=0, keepdims=True))
        p = jnp.exp2(s - m_new).astype(BF16)
        pv = jnp.dot(vt[:, key0:key0 + s.shape[0]], p, preferred_element_type=F32)
        acc_new = jnp.exp2(m_old - m_new) * acc_old + pv
        if q0:
            m_new = jnp.concatenate([m[:, :q0], m_new], axis=1)
            acc_new = jnp.concatenate([acc[:, :q0], acc_new], axis=1)
        return m_new, acc_new

    work = [(n, j) for n in range(v_ref.shape[1]) for j in range(n + 1)]
    m, acc = [None, None], [None, None]

    def first(t):
        return [scores(c, *work[t]) for c in range(2)]

    def second(t, s):
        n, j = work[t]
        vt = v_ref[0, j, 0]
        for c in range(2):
            if j == 0:
                m[c] = jnp.full((1, tq), NEG_BIG, F32)
                acc[c] = jnp.zeros((LANES + ONES_ROWS, tq), F32)
            for piece in s[c]:
                m[c], acc[c] = absorb(m[c], acc[c], piece, vt)
        if j == n:
            emit(n, [(acc[c][:LANES] / acc[c][LANES:LANES + 1]).T for c in range(2)])

    _pipelined(len(work), first, second, lookahead=2)


def _history_scores(q, kt_past, bias_row, k_cur, diag_bias):
    s_past = jnp.dot(q[:, :SLOT_W], kt_past, preferred_element_type=F32) + bias_row
    s_cur = lax.dot_general(q, k_cur, _CONTRACT_LAST, preferred_element_type=F32) + diag_bias
    return s_past, s_cur


def _history_output(s_past, s_cur, v_past, v_cur, v_time_minor):
    m = jnp.maximum(jnp.max(s_past, axis=-1, keepdims=True), jnp.max(s_cur, axis=-1, keepdims=True))
    p_past = jnp.exp2(s_past - m)
    p_cur = jnp.exp2(s_cur - m)
    l = jnp.sum(p_past, axis=-1, keepdims=True) + jnp.sum(p_cur, axis=-1, keepdims=True)
    if v_time_minor:
        o = lax.dot_general(p_past.astype(BF16), v_past, _CONTRACT_LAST, preferred_element_type=F32)
    else:
        o = jnp.dot(p_past.astype(BF16), v_past, preferred_element_type=F32)
    return (o + jnp.dot(p_cur.astype(BF16), v_cur, preferred_element_type=F32)) / l


def _pipelined(n, first, second, lookahead=2):
    pending = [first(h) for h in range(min(lookahead, n))]
    outs = []
    for h in range(n):
        if h + lookahead < n:
            pending.append(first(h + lookahead))
        outs.append(second(h, pending[h]))
    return outs


def _diag_indices(nk, nq, k0, q0, key_major):
    shape = (nk, nq) if key_major else (nq, nk)
    rowi, coli = _row_iota(shape), _lane_iota(shape)
    return (coli + q0, rowi + k0) if key_major else (rowi + q0, coli + k0)


def _diag_pieces(tq):
    half = tq // 2
    return _diag_indices(half, tq, 0, 0, True), _diag_indices(half, half, half, half, True)


def _causal_bias(qi, ki):
    return jnp.where(ki <= qi, 0.0, NEG_BIG)


def _diff_diag_bias(qi, ki, slope):
    ahead = jnp.maximum(ki - qi, 0).astype(F32)
    visible = (ki // CHUNK) <= (qi // CHUNK)
    return jnp.where(visible, (-2.0 * LOG2E) * slope * ahead, NEG_BIG)


def _diff_finish(o1, o2, lam_ref, sub_ref):
    lq1, lk1, lq2, lk2 = (lam_ref[k:k + 1, :] for k in range(4))
    lam = (jnp.exp(jnp.sum(lq1 * lk1, axis=-1, keepdims=True))
           - jnp.exp(jnp.sum(lq2 * lk2, axis=-1, keepdims=True)) + LAMBDA_INIT)
    o = o1 - lam * o2
    o = o * lax.rsqrt(jnp.mean(o * o, axis=-1, keepdims=True) + EPS) * sub_ref[...] * (1.0 - LAMBDA_INIT)
    return o.astype(BF16)


def _fox_kernel(q_ref, k_ref, v_ref, o_ref, *, tq):
    bias = tuple(_causal_bias(qi, ki) for qi, ki in _diag_pieces(tq))
    lane = _lane_iota((tq, LANES))

    def emit(n, outs):
        o_ref[0, n * tq:(n + 1) * tq, :] = jnp.where(lane < SLOT_W, outs[0], outs[1]).astype(BF16)

    _attend_pair(q_ref, k_ref, v_ref, bias, emit, tq=tq)


def _fox_history_kernel(q_ref, kc_ref, vc_ref, kp_ref, vp_ref, c_ref, o_ref, *, ts):
    causal = _causal_bias(*_diag_indices(ts, ts, 0, 0, False))
    lane = _lane_iota((ts, LANES))

    def scores(h):
        return _history_scores(q_ref[0, h], kp_ref[0, h].astype(BF16), c_ref[0, h:h + 1, :] * -LOG2E,
                               kc_ref[0, h], causal)

    pair_values = {}

    def output(h, s):
        g = h // 2
        if g not in pair_values:
            pair_values[g] = jnp.concatenate([vp_ref[0, 2 * g], vp_ref[0, 2 * g + 1]], axis=0).astype(BF16)
        return _history_output(*s, pair_values[g], vc_ref[0, :, LANES * g:LANES * (g + 1)], True)

    outs = _pipelined(H_A, scores, output)
    for g in range(H_A // 2):
        o_ref[0, :, LANES * g:LANES * (g + 1)] = jnp.where(
            lane < SLOT_W, outs[2 * g], outs[2 * g + 1]).astype(BF16)


def _diff_kernel(q_ref, k_ref, v_ref, lam_ref, sub_ref, o_ref, *, tq):
    hd = pl.program_id(1)
    slope = jnp.float32(ALIBI_SLOPES[0])
    for k in range(1, H_B):
        slope = jnp.where(hd == k, jnp.float32(ALIBI_SLOPES[k]), slope)
    bias = tuple(_diff_diag_bias(qi, ki, slope) for qi, ki in _diag_pieces(tq))

    def emit(n, outs):
        o_ref[0, n * tq:(n + 1) * tq, :] = _diff_finish(outs[0], outs[1], lam_ref, sub_ref)

    _attend_pair(q_ref, k_ref, v_ref, bias, emit, tq=tq)


def _diff_history_kernel(q_ref, kc_ref, vc_ref, kp_ref, vp_ref, lam_ref, sub_ref, o_ref, *, ts):
    plen = kp_ref.shape[3]
    pos = _lane_iota((1, plen)).astype(F32)
    qi, ki = _diag_indices(ts, ts, 0, 0, False)

    def scores(j):
        slope = ALIBI_SLOPES[j // 2]
        return _history_scores(q_ref[0, j], kp_ref[0, j].astype(BF16), pos * (slope * LOG2E),
                               kc_ref[0, j], _diff_diag_bias(qi, ki, slope))

    head_values = {}

    def output(j, s):
        h = j // 2
        if h not in head_values:
            head_values[h] = vp_ref[0, pl.ds(h, plen, stride=H_B), :].astype(BF16)
        return _history_output(*s, head_values[h], vc_ref[0, :, DV_B * h:DV_B * (h + 1)], False)

    outs = _pipelined(HEAD_SLOTS, scores, output)
    for h in range(H_B):
        o_ref[0, :, DV_B * h:DV_B * (h + 1)] = _diff_finish(outs[2 * h], outs[2 * h + 1], lam_ref, sub_ref)


def _attention(kind, q_aug, k_aug, v_cur, extras, tq):
    bx, _, seq, _ = k_aug.shape
    groups = HEAD_SLOTS // 2
    nblk = seq // tq
    slab_rows = LANES + ONES_ROWS
    assert v_cur.shape == (bx, nblk, groups, slab_rows, tq)
    grid = (bx, groups)
    in_specs = [
        pl.BlockSpec((1, 2, LANES, seq), lambda b, g: (b, g, 0, 0)),
        pl.BlockSpec((1, 2, seq, LANES), lambda b, g: (b, g, 0, 0)),
        pl.BlockSpec((1, nblk, 1, slab_rows, tq), lambda b, g: (b, 0, g, 0, 0)),
    ] + [pl.BlockSpec(a.shape, lambda b, g: (0,) * a.ndim) for a in extras]
    body = _fox_kernel if kind == "fox" else _diff_kernel
    return pl.pallas_call(
        functools.partial(body, tq=tq),
        grid=grid,
        in_specs=in_specs,
        out_specs=pl.BlockSpec((1, seq, LANES), lambda b, g: (b, 0, g)),
        out_shape=jax.ShapeDtypeStruct((bx, seq, groups * LANES), BF16),
        compiler_params=pltpu.CompilerParams(
            dimension_semantics=("arbitrary", "arbitrary"), vmem_limit_bytes=VMEM_LIMIT),
        name=kind + "_attn",
    )(q_aug, k_aug, v_cur, *extras)


def _history_attention(kind, q_aug, k_aug, v_cur, kt_past, v_past, extras):
    bx, _, ts, _ = q_aug.shape
    whole = lambda a: pl.BlockSpec((1,) + a.shape[1:], lambda b: (b,) + (0,) * (a.ndim - 1))
    shared = lambda a: pl.BlockSpec(a.shape, lambda b: (0,) * a.ndim)
    args = [q_aug, k_aug, v_cur, kt_past, v_past]
    in_specs = [whole(a) for a in args] + [whole(a) if kind == "fox" else shared(a) for a in extras]
    body = _fox_history_kernel if kind == "fox" else _diff_history_kernel
    return pl.pallas_call(
        functools.partial(body, ts=ts),
        grid=(bx,),
        in_specs=in_specs,
        out_specs=pl.BlockSpec((1, ts, v_cur.shape[2]), lambda b: (b, 0, 0)),
        out_shape=jax.ShapeDtypeStruct((bx, ts, v_cur.shape[2]), BF16),
        compiler_params=pltpu.CompilerParams(
            dimension_semantics=("arbitrary",), vmem_limit_bytes=VMEM_LIMIT),
        name=kind + "_history_attn",
    )(*args, *extras)


def _merge_ffn_kernel(x_ref, oa_ref, ob_ref, sig_ref, woa_ref, wob_ref, wout_ref, gffn_ref,
                      wup_ref, cw_ref, cb_ref, wdown_ref, st_ref,
                      y_ref, ns_ref, carry_ref, ubuf_ref, act_ref, *, tm, seq, d_ff):
    t = pl.program_id(0)
    carry_mode = seq >= tm
    nb = 1 if carry_mode else tm // seq
    rows = tm if carry_mode else seq
    tpb = max(1, seq // tm)
    d = x_ref.shape[1]
    seg = rows + SUBLANES

    ya = jnp.dot(oa_ref[...], woa_ref[...], preferred_element_type=F32)
    yb = jnp.dot(ob_ref[...], wob_ref[...], preferred_element_type=F32)
    m = sig_ref[:, :d].astype(F32) * ya + sig_ref[:, d:].astype(F32) * yb
    x1 = x_ref[...] + jnp.dot(m.astype(BF16), wout_ref[...], preferred_element_type=F32)
    h = (x1 * lax.rsqrt(jnp.mean(x1 * x1, axis=-1, keepdims=True) + EPS) * gffn_ref[...]).astype(BF16)

    if carry_mode:
        @pl.when(t % tpb == 0)
        def _():
            carry_ref[0:2, :] = st_ref[0]

    def up(c0):
        return (jnp.dot(h, wup_ref[:, c0:c0 + FF_CHUNK], preferred_element_type=F32),
                jnp.dot(h, wup_ref[:, d_ff + c0:d_ff + c0 + FF_CHUNK], preferred_element_type=F32))

    def conv(u, c0, buf):
        cols = slice(c0, c0 + FF_CHUNK)
        w0, w1, w2 = (cw_ref[k:k + 1, cols] for k in range(3))
        outs = []
        for bi in range(nb):
            useg = u[bi * rows:(bi + 1) * rows, :]
            r0 = bi * seg + SUBLANES
            hist = carry_ref[0:2, cols] if carry_mode else st_ref[bi, :, cols]
            ubuf_ref[buf, r0 - 2:r0, :] = hist
            ubuf_ref[buf, r0:r0 + rows, :] = useg
            ns_ref[bi, :, cols] = useg[rows - 2:rows, :]
            u1 = ubuf_ref[buf, r0 - 1:r0 - 1 + rows, :]
            u2 = ubuf_ref[buf, r0 - 2:r0 - 2 + rows, :]
            outs.append(w0 * u2 + w1 * u1 + w2 * useg + cb_ref[:, cols])
        if carry_mode:
            carry_ref[0:2, cols] = u[tm - 2:tm, :]
        return outs[0] if nb == 1 else jnp.concatenate(outs, axis=0)

    chunks = list(range(0, d_ff, FF_CHUNK))
    split = chunks[len(chunks) // 2]
    nxt = up(chunks[0])
    for k, c0 in enumerate(chunks):
        ua, ub = nxt
        if k + 1 < len(chunks):
            nxt = up(chunks[k + 1])
        if c0 == split:
            y_half = x1 + jnp.dot(act_ref[:, :split], wdown_ref[:split, :], preferred_element_type=F32)
        a = conv(ua, c0, 2 * (k % 2))
        b = conv(ub, d_ff + c0, 2 * (k % 2) + 1)
        act_ref[:, c0:c0 + FF_CHUNK] = (a * jax.nn.sigmoid(a) * b).astype(BF16)
    y_ref[...] = y_half + jnp.dot(act_ref[:, split:], wdown_ref[split:, :], preferred_element_type=F32)


def _merge_ffn(x2, oa, ob, sig, state, seq, prm, tm):
    n, d = x2.shape
    d_ff = prm["w_down"].shape[0]
    assert d_ff % FF_CHUNK == 0 and seq >= 2
    carry_mode = seq >= tm
    nb = 1 if carry_mode else tm // seq
    rows = tm if carry_mode else seq
    tpb = max(1, seq // tm)
    bx = n // seq
    grid = (n // tm,)
    const = lambda a: pl.BlockSpec(a.shape, lambda t: (0,) * a.ndim, pipeline_mode=pl.Buffered(1))
    rowblk = lambda w: pl.BlockSpec((tm, w), lambda t: (t, 0))
    if carry_mode:
        st_spec = pl.BlockSpec((1, 2, 2 * d_ff), lambda t: (t // tpb, 0, 0))
    else:
        st_spec = pl.BlockSpec((nb, 2, 2 * d_ff), lambda t: (t, 0, 0))
    consts1 = [prm["w_oa"], prm["w_ob"], prm["w_out"], prm["g_ffn"], prm["w_up"], prm["conv_w"],
               prm["conv_b"], prm["w_down"]]
    return pl.pallas_call(
        functools.partial(_merge_ffn_kernel, tm=tm, seq=seq, d_ff=d_ff),
        grid=grid,
        in_specs=[rowblk(d), rowblk(W_A), rowblk(W_B), rowblk(2 * d)] + [const(a) for a in consts1] + [st_spec],
        out_specs=[rowblk(d), st_spec],
        out_shape=[jax.ShapeDtypeStruct((n, d), F32), jax.ShapeDtypeStruct((bx, 2, 2 * d_ff), F32)],
        scratch_shapes=[pltpu.VMEM((SUBLANES, 2 * d_ff), F32),
                        pltpu.VMEM((4, nb * (rows + SUBLANES), FF_CHUNK), F32),
                        pltpu.VMEM((tm, d_ff), BF16)],
        compiler_params=pltpu.CompilerParams(
            dimension_semantics=("arbitrary",), vmem_limit_bytes=VMEM_LIMIT),
        name="merge_ffn",
    )(x2, oa, ob, sig, *consts1, state)


def _tri(tm, seg):
    i = np.arange(tm)
    return jnp.asarray((i[None, :] <= i[:, None]) & (i[None, :] // seg == i[:, None] // seg), BF16)


def _piece_placer():
    p = np.zeros((LANES, HEAD_SLOTS * SLOT_W), np.float32)
    for k in range(N_CUM_PIECES):
        for h in range(HEAD_SLOTS):
            p[k * HEAD_SLOTS + h, h * SLOT_W + k] = -1.0
    return jnp.asarray(p, BF16)


def _block_diag_ones(width, blk):
    i = np.arange(width)
    return jnp.asarray(i[:, None] // blk == i[None, :] // blk, BF16)


def _rep3(a):
    pad = jnp.zeros(a.shape[:-1] + (LANES - N_CUM_PIECES * HEAD_SLOTS,), a.dtype)
    return jnp.concatenate([a] * N_CUM_PIECES + [pad], axis=-1)


def _tiles(seq):
    return dict(tm=512, tq=min(512, seq))


def _time_minor(a):
    return jnp.moveaxis(a, 1, -1)


def _time_major(a):
    return jnp.moveaxis(a, -1, 1)


def _layer(x, pos_off, cinit, past, state, prm, lam_pack, tiles):
    bx, seq, d = x.shape
    tm, tq = tiles["tm"], tiles["tq"]
    x2 = x.reshape(bx * seq, d)
    ka, va, logf, kb, vb, qaa, kaa, qba, kba, vab, vbb, sig = _in_proj(
        x2, seq, pos_off, cinit, prm, tm, key_major=past is None)
    diff_prm = (lam_pack, prm["subln_b"])
    if past is None:
        assert tq == tm
        oa = _attention("fox", qaa, kaa, vab, (), tq)
        ob = _attention("diff", qba, kba, vbb, diff_prm, tq)
    else:
        vab = vab.reshape(bx, seq, W_A)
        vbb = vbb.reshape(bx, seq, W_B)
        oa = _history_attention("fox", qaa, kaa, vab, past["kt_a"], past["vt_a"], (past["c"],))
        ob = _history_attention("diff", qba, kba, vbb, past["kt_b"], past["v_b"], diff_prm)
    y, new_state = _merge_ffn(x2, oa.reshape(bx * seq, W_A), ob.reshape(bx * seq, W_B), sig, state, seq, prm, tm)
    return (y.reshape(bx, seq, d), _time_major(ka), _time_major(va), _time_major(logf),
            _time_major(kb.reshape(bx, H_B, 2, DH_B, seq)), vb.reshape(bx, seq, H_B, DV_B), new_state)


def kernel(x_prompt, x_sample, cache_a_k, cache_a_v, cache_a_logf, cache_b_k, cache_b_v, state_ffn_conv,
           g_attn, w_in, b_f, qn_a, kn_a, qn_b, kn_b, lambda_q1, lambda_k1, lambda_q2, lambda_k2,
           subln_b, w_oa, w_ob, w_out, g_ffn, w_up, conv_w, conv_b, w_down):
    bp, tp_, d = x_prompt.shape
    bs, ts, _ = x_sample.shape
    plen = cache_a_k.shape[1]
    d_ff = w_down.shape[0]

    f0, f1 = 3 * W_A, 3 * W_A + H_A
    prm = {
        "g_attn": g_attn.reshape(1, d),
        "w_main": jnp.concatenate([w_in[:, :f0], w_in[:, f1:]], axis=1).astype(BF16),
        "w_f": _rep3(w_in[:, f0:f1]).astype(BF16),
        "b_f": _rep3(b_f.reshape(1, H_A)),
        "qn_a": jnp.tile(qn_a, H_A).reshape(1, W_A), "kn_a": jnp.tile(kn_a, H_A).reshape(1, W_A),
        "qn_b": jnp.tile(qn_b, 2 * H_B).reshape(1, W_B), "kn_b": jnp.tile(kn_b, 2 * H_B).reshape(1, W_B),
        "bd": _block_diag_ones(2 * LANES, DH_A),
        "pc": _piece_placer(),
        "subln_b": subln_b.reshape(1, DV_B),
        "w_oa": w_oa.astype(BF16), "w_ob": w_ob.astype(BF16), "w_out": w_out.astype(BF16),
        "g_ffn": g_ffn.reshape(1, d), "w_up": w_up.astype(BF16), "conv_w": conv_w,
        "conv_b": conv_b.reshape(1, 2 * d_ff), "w_down": w_down.astype(BF16),
    }
    lam_pack = jnp.stack([lambda_q1, lambda_k1, lambda_q2, lambda_k2])

    tl = _tiles(tp_)
    prm_p = dict(prm, ltri_in=_tri(tl["tm"], min(tl["tm"], tp_)))
    zeros_c = jnp.zeros((bp, 1, LANES), F32)
    zeros_state = jnp.zeros((bp, 2, 2 * d_ff), F32)
    (y_p, ka_p, va_p, lf_p, kb_p, vb_p, st_p) = _layer(
        x_prompt, 0, zeros_c, None, zeros_state, prm_p, lam_pack, tl)

    tl = _tiles(ts)
    c_past, c_tot = _cache_cumsum(_time_minor(cache_a_logf).reshape(bs * H_A, plen), min(512, plen))
    past = {
        "kt_a": _time_minor(cache_a_k), "vt_a": _time_minor(cache_a_v), "c": c_past.reshape(bs, H_A, plen),
        "kt_b": _time_minor(cache_b_k).reshape(bs, HEAD_SLOTS, DH_B, plen),
        "v_b": cache_b_v.reshape(bs, plen * H_B, DV_B),
    }
    prm_s = dict(prm, ltri_in=_tri(tl["tm"], min(tl["tm"], ts)))
    cinit_rows = jnp.repeat(_rep3(c_tot[:, 0].reshape(bs, H_A)), ts, axis=0)
    (y_s, ka_s, va_s, lf_s, kb_s, vb_s, st_s) = _layer(
        x_sample, plen, cinit_rows, past, state_ffn_conv, prm_s, lam_pack, tl)

    return (y_p, y_s, ka_p, va_p, lf_p, kb_p, vb_p, st_p, ka_s, va_s, lf_s, kb_s, vb_s, st_s)
```

```python
import functools
import math

import jax
import jax.numpy as jnp
import numpy as np
from jax import lax
from jax.experimental import pallas as pl
from jax.experimental.pallas import tpu as pltpu

F32 = jnp.float32
BF16 = jnp.bfloat16

CHUNK = 64
H_A = 8
DH_A = 64
H_B = 4
DH_B = 64
DV_B = 128
W_A = H_A * DH_A
W_B = H_B * DV_B
EPS = 1e-6
LAMBDA_INIT = 0.8 - 0.6 * math.exp(-0.3 * 0)
ALIBI_SLOPES = tuple(2.0 ** (-8.0 * (i + 1) / H_B) for i in range(H_B))

LANES = 128
SUBLANES = 8
HEAD_SLOTS = 8
SLOT_W = 64
N_CUM_PIECES = 3
N_POS_PIECES = 3
NEG_BIG = -1e30
LOG2E = math.log2(math.e)
ONES_ROWS = 16
VMEM_LIMIT = 58 * 1024 * 1024
FF_CHUNK = 256


def _lane_iota(shape):
    return lax.broadcasted_iota(jnp.int32, shape, len(shape) - 1)


def _row_iota(shape):
    return lax.broadcasted_iota(jnp.int32, shape, len(shape) - 2)


def _split3(c):
    hi = c.astype(BF16).astype(F32)
    r1 = c - hi
    lo = r1.astype(BF16).astype(F32)
    lo2 = (r1 - lo).astype(BF16).astype(F32)
    return hi, lo, lo2


def _cumsum_pieces(lf, ltri_ref, base):
    hi, lo, lo2 = _split3(lf)
    ltri = ltri_ref[...]
    c = (jnp.dot(ltri, hi.astype(BF16), preferred_element_type=F32)
         + jnp.dot(ltri, lo.astype(BF16), preferred_element_type=F32)
         + jnp.dot(ltri, lo2.astype(BF16), preferred_element_type=F32)) + base
    chi, clo, clo2 = _split3(c * LOG2E)
    lane = _lane_iota(c.shape)
    pieces = jnp.where(lane < HEAD_SLOTS, chi, jnp.where(lane < 2 * HEAD_SLOTS, clo, clo2))
    return c, pieces.astype(BF16)


def _slot(z, j):
    return z[:, SLOT_W * j:SLOT_W * (j + 1)]


def _pos_extra(pos, slope):
    hi, lo, lo2 = _split3(pos.astype(F32) * (slope * LOG2E))
    lane = _lane_iota(hi.shape)
    return jnp.where(lane == 0, hi, jnp.where(lane == 1, lo, jnp.where(lane == 2, lo2, 0.0)))


def _store_tile(ref, slot64, extra64, j, nb, rows, time_minor=False):
    tile = jnp.concatenate([slot64, extra64], axis=1)
    if time_minor:
        tile_t = tile.T.astype(BF16)
        for bi in range(nb):
            ref[bi, j, :, :] = tile_t[:, bi * rows:(bi + 1) * rows]
    else:
        tile = tile.astype(BF16)
        for bi in range(nb):
            ref[bi, j, :, :] = tile[bi * rows:(bi + 1) * rows, :]


def _emit_fox_keys(kaa_ref, slots, pieces, pc_ref, nb, rows):
    extra = jnp.dot(pieces, pc_ref[...], preferred_element_type=F32)
    for j in range(HEAD_SLOTS):
        _store_tile(kaa_ref, slots[j], _slot(extra, j), j, nb, rows)


def _emit_diff_keys(kba_ref, slots, pos, nb, rows):
    for h in range(H_B):
        extra = _pos_extra(pos, ALIBI_SLOPES[h])
        for m in range(2):
            _store_tile(kba_ref, slots[2 * h + m], extra, 2 * h + m, nb, rows)


def _emit_queries(q_ref, qn, n_ones, nb, rows, time_minor):
    lane = _lane_iota((qn.shape[0], SLOT_W))
    ones = jnp.where(lane < n_ones, 1.0, 0.0)
    for j in range(HEAD_SLOTS):
        _store_tile(q_ref, _slot(qn, j), ones, j, nb, rows, time_minor)


def _cache_cumsum_kernel(lf_ref, utri_ref, c_ref, tot_ref, carry_ref, *, tb):
    t = pl.program_id(0)

    @pl.when(t == 0)
    def _():
        carry_ref[...] = jnp.zeros_like(carry_ref)

    hi, lo, lo2 = _split3(lf_ref[...])
    u = utri_ref[...]
    c = (jnp.dot(hi.astype(BF16), u, preferred_element_type=F32)
         + jnp.dot(lo.astype(BF16), u, preferred_element_type=F32)
         + jnp.dot(lo2.astype(BF16), u, preferred_element_type=F32)) + carry_ref[...]
    c_ref[...] = c
    last = c[:, tb - 1:tb]
    carry_ref[...] = last
    tot_ref[...] = jnp.broadcast_to(last, tot_ref.shape)


def _cache_cumsum(lf_t, tb):
    r, p = lf_t.shape
    i = np.arange(tb)
    utri = jnp.asarray(i[:, None] <= i[None, :], BF16)
    return pl.pallas_call(
        functools.partial(_cache_cumsum_kernel, tb=tb),
        grid=(p // tb,),
        in_specs=[pl.BlockSpec((r, tb), lambda t: (0, t)),
                  pl.BlockSpec((tb, tb), lambda t: (0, 0), pipeline_mode=pl.Buffered(1))],
        out_specs=[pl.BlockSpec((r, tb), lambda t: (0, t)), pl.BlockSpec((r, LANES), lambda t: (0, 0))],
        out_shape=[jax.ShapeDtypeStruct((r, p), F32), jax.ShapeDtypeStruct((r, LANES), F32)],
        scratch_shapes=[pltpu.VMEM((r, 1), F32)],
        compiler_params=pltpu.CompilerParams(
            dimension_semantics=("arbitrary",), vmem_limit_bytes=VMEM_LIMIT),
        name="cache_cumsum",
    )(lf_t, utri)


def _in_proj_kernel(x_ref, g_ref, wm_ref, wf_ref, bf_ref, qna_ref, kna_ref, qnb_ref, knb_ref,
                    bd_ref, ltri_ref, pc_ref, cinit_ref,
                    ka_ref, va_ref, logf_ref, kb_ref, vb_ref,
                    qaa_ref, kaa_ref, qba_ref, kba_ref, vab_ref, vbb_ref, sig_ref, carry_ref,
                    *, tm, seq, pos_off, key_major):
    t = pl.program_id(0)
    carry_mode = seq >= tm
    nb = 1 if carry_mode else tm // seq
    rows = tm if carry_mode else seq
    tpb = max(1, seq // tm)

    x = x_ref[...]
    h = (x * lax.rsqrt(jnp.mean(x * x, axis=-1, keepdims=True) + EPS) * g_ref[...]).astype(BF16)

    def proj(lo, hi):
        return jnp.dot(h, wm_ref[:, lo:hi], preferred_element_type=F32)

    def head_norm(z, w_ref):
        z2 = (z * z).astype(BF16)
        bw = bd_ref.shape[0]
        ss = jnp.concatenate([jnp.dot(z2[:, k:k + bw], bd_ref[...], preferred_element_type=F32)
                              for k in range(0, z.shape[1], bw)], axis=1)
        return z * lax.rsqrt(ss * (1.0 / DH_A) + EPS) * w_ref[...]

    def store_heads(ref, z, heads):
        w = z.shape[1] // heads
        for j in range(heads):
            ref[pl.ds(j, tm, stride=heads), :] = z[:, w * j:w * (j + 1)]

    def store_time_minor(ref, z):
        zts = [z[bi * rows:(bi + 1) * rows, :].T for bi in range(nb)]
        for bi, zt in enumerate(zts):
            ref[bi] = zt.reshape(z.shape[1] // SLOT_W, SLOT_W, rows)
        return zts

    def store_slots(ref, z):
        if seq >= LANES:
            return store_time_minor(ref, z)
        store_heads(ref, z, z.shape[1] // SLOT_W)
        return None

    def store_value_slabs(ref, vt):
        ones = jnp.ones((ONES_ROWS, tm), BF16)
        for g in range(W_A // LANES):
            ref[0, 0, g, 0:LANES, :] = vt[LANES * g:LANES * (g + 1), :].astype(BF16)
            ref[0, 0, g, LANES:LANES + ONES_ROWS, :] = ones

    zf = jnp.dot(h, wf_ref[...], preferred_element_type=F32) + bf_ref[...]
    lf = jnp.minimum(zf, 0.0) - jnp.log1p(jnp.exp(-jnp.abs(zf)))
    lane = _lane_iota(lf.shape)
    lf = jnp.where(lane < N_CUM_PIECES * HEAD_SLOTS, lf, 0.0)
    for bi in range(nb):
        logf_ref[bi] = lf[bi * rows:(bi + 1) * rows, :].T[:H_A, :]
    if carry_mode:
        @pl.when(t % tpb == 0)
        def _():
            carry_ref[...] = cinit_ref[0]
        base = carry_ref[...]
    else:
        base = cinit_ref[...]
    c, pieces = _cumsum_pieces(lf, ltri_ref, base)
    if carry_mode:
        carry_ref[...] = c[tm - 1:tm, :]

    row = t * tm + _row_iota((tm, SLOT_W))
    pos = (row & (seq - 1)) + pos_off

    qn = head_norm(proj(0, W_A), qna_ref) * (DH_A ** -0.5 * LOG2E)
    _emit_queries(qaa_ref, qn, N_CUM_PIECES, nb, rows, key_major)
    kn = head_norm(proj(W_A, 2 * W_A), kna_ref)
    store_slots(ka_ref, kn)
    _emit_fox_keys(kaa_ref, [_slot(kn, j) for j in range(HEAD_SLOTS)], pieces, pc_ref, nb, rows)
    v = proj(2 * W_A, 3 * W_A)
    vts = store_slots(va_ref, v)
    if key_major:
        store_value_slabs(vab_ref, vts[0])
    else:
        vab_ref[...] = v.astype(BF16)

    o = 3 * W_A
    qn = head_norm(proj(o, o + W_B), qnb_ref) * (DH_B ** -0.5 * LOG2E)
    _emit_queries(qba_ref, qn, N_POS_PIECES, nb, rows, key_major)
    kn = head_norm(proj(o + W_B, o + 2 * W_B), knb_ref)
    store_slots(kb_ref, kn)
    _emit_diff_keys(kba_ref, [_slot(kn, j) for j in range(HEAD_SLOTS)], pos, nb, rows)
    v = proj(o + 2 * W_B, o + 3 * W_B)
    store_heads(vb_ref, v, H_B)
    if key_major:
        store_value_slabs(vbb_ref, v.T)
    else:
        vbb_ref[...] = v.astype(BF16)

    o = 3 * W_A + 3 * W_B
    d = (wm_ref.shape[1] - o) // 2
    for k in range(2):
        sig_ref[:, k * d:(k + 1) * d] = jax.nn.sigmoid(proj(o + k * d, o + (k + 1) * d)).astype(BF16)


def _in_proj(x2, seq, pos_off, cinit, prm, tm, key_major):
    n, d = x2.shape
    carry_mode = seq >= tm
    nb = 1 if carry_mode else tm // seq
    tpb = max(1, seq // tm)
    bx = n // seq
    assert n % tm == 0 and seq & (seq - 1) == 0
    grid = (n // tm,)
    const = lambda a: pl.BlockSpec(a.shape, lambda t: (0,) * a.ndim, pipeline_mode=pl.Buffered(1))
    rowblk = lambda w, mult=1: pl.BlockSpec((tm * mult, w), lambda t: (t, 0))
    if carry_mode:
        cinit_spec = pl.BlockSpec((1, 1, LANES), lambda t: (t // tpb, 0, 0))
        aug_spec = pl.BlockSpec((1, HEAD_SLOTS, tm, LANES), lambda t: (t // tpb, 0, t % tpb, 0))
    else:
        cinit_spec = rowblk(LANES)
        aug_spec = pl.BlockSpec((nb, HEAD_SLOTS, seq, LANES), lambda t: (t, 0, 0, 0))
    aug_shape = jax.ShapeDtypeStruct((bx, HEAD_SLOTS, seq, LANES), BF16)
    if carry_mode:
        tmin_spec = pl.BlockSpec((1, HEAD_SLOTS, SLOT_W, tm), lambda t: (t // tpb, 0, 0, t % tpb))
        lf_spec = pl.BlockSpec((1, H_A, tm), lambda t: (t // tpb, 0, t % tpb))
    else:
        tmin_spec = pl.BlockSpec((nb, HEAD_SLOTS, SLOT_W, seq), lambda t: (t, 0, 0, 0))
        lf_spec = pl.BlockSpec((nb, H_A, seq), lambda t: (t, 0, 0))
    tmin_shape = jax.ShapeDtypeStruct((bx, HEAD_SLOTS, SLOT_W, seq), F32)
    if seq < LANES:
        tmin_spec = rowblk(SLOT_W, HEAD_SLOTS)
        tmin_shape = jax.ShapeDtypeStruct((n * HEAD_SLOTS, SLOT_W), F32)
    if key_major:
        assert carry_mode
        q_spec = pl.BlockSpec((1, HEAD_SLOTS, LANES, tm), lambda t: (t // tpb, 0, 0, t % tpb))
        q_shape = jax.ShapeDtypeStruct((bx, HEAD_SLOTS, LANES, seq), BF16)
        slab = (W_A // LANES, LANES + ONES_ROWS, tm)
        v_spec = pl.BlockSpec((1, 1) + slab, lambda t: (t // tpb, t % tpb, 0, 0, 0))
        v_shape = jax.ShapeDtypeStruct((bx, tpb) + slab, BF16)
    else:
        q_spec, q_shape = aug_spec, aug_shape
        v_spec = rowblk(W_A)
        v_shape = jax.ShapeDtypeStruct((n, W_A), BF16)
    consts = [prm["g_attn"], prm["w_main"], prm["w_f"], prm["b_f"], prm["qn_a"], prm["kn_a"],
              prm["qn_b"], prm["kn_b"], prm["bd"], prm["ltri_in"], prm["pc"]]
    return pl.pallas_call(
        functools.partial(_in_proj_kernel, tm=tm, seq=seq, pos_off=pos_off, key_major=key_major),
        grid=grid,
        in_specs=[rowblk(d)] + [const(a) for a in consts] + [cinit_spec],
        out_specs=[tmin_spec, tmin_spec, lf_spec, tmin_spec, rowblk(DV_B, H_B),
                   q_spec, aug_spec, q_spec, aug_spec, v_spec, v_spec, rowblk(2 * d)],
        out_shape=[
            tmin_shape, tmin_shape, jax.ShapeDtypeStruct((bx, H_A, seq), F32),
            tmin_shape, jax.ShapeDtypeStruct((n * H_B, DV_B), F32),
            q_shape, aug_shape, q_shape, aug_shape, v_shape, v_shape,
            jax.ShapeDtypeStruct((n, 2 * d), BF16),
        ],
        scratch_shapes=[pltpu.VMEM((1, LANES), F32)],
        compiler_params=pltpu.CompilerParams(
            dimension_semantics=("arbitrary",), vmem_limit_bytes=VMEM_LIMIT),
        name="in_proj",
    )(x2, *consts, cinit)


_CONTRACT_LAST = (((1,), (1,)), ((), ()))


def _attend_pair(q_ref, k_ref, v_ref, diag_bias, emit, *, tq):
    half = tq // 2
    bias_top, bias_bot = diag_bias

    def scores(c, n, j):
        qt = q_ref[0, c, :, n * tq:(n + 1) * tq]
        k0 = j * tq
        if j < n:
            return [(jnp.dot(k_ref[0, c, k0:k0 + tq, :], qt, preferred_element_type=F32), 0, 0)]
        top = jnp.dot(k_ref[0, c, k0:k0 + half, :], qt, preferred_element_type=F32) + bias_top
        bot = jnp.dot(k_ref[0, c, k0 + half:k0 + tq, :], qt[:, half:], preferred_element_type=F32) + bias_bot
        return [(top, 0, 0), (bot, half, half)]

    def absorb(m, acc, pieces, vt):
        _, key0, q0 = pieces[0]
        rows = pieces[0][0].shape[0]
        m_old = [m[c][:, q0:] for c in range(2)]
        m_new = [jnp.maximum(m_old[c], jnp.max(pieces[c][0], axis=0, keepdims=True)) for c in range(2)]
        p = [jnp.exp2(pieces[c][0] - m_new[c]).astype(BF16) for c in range(2)]
        pv = [jnp.dot(vt[:, key0:key0 + rows], p[c], preferred_element_type=F32) for c in range(2)]
        for c in range(2):
            acc_new = jnp.exp2(m_old[c] - m_new[c]) * acc[c][:, q0:] + pv[c]
            if q0:
                m_new[c] = jnp.concatenate([m[c][:, :q0], m_new[c]], axis=1)
                acc_new = jnp.concatenate([acc[c][:, :q0], acc_new], axis=1)
            m[c], acc[c] = m_new[c], acc_new

    work = [(n, j) for n in range(v_ref.shape[1]) for j in range(n + 1)]
    m, acc = [None, None], [None, None]

    def first(t):
        return [scores(c, *work[t]) for c in range(2)]

    def second(t, s):
        n, j = work[t]
        vt = v_ref[0, j, 0]
        if j == 0:
            for c in range(2):
                m[c] = jnp.full((1, tq), NEG_BIG, F32)
                acc[c] = jnp.zeros((LANES + ONES_ROWS, tq), F32)
        for pieces in zip(*s):
            absorb(m, acc, pieces, vt)
        if j == n:
            emit(n, [(acc[c][:LANES] / acc[c][LANES:LANES + 1]).T for c in range(2)])

    _pipelined(len(work), first, second, lookahead=2)


def _history_scores(q, kt_past, bias_row, k_cur, diag_bias):
    s_past = jnp.dot(q[:, :SLOT_W], kt_past, preferred_element_type=F32) + bias_row
    s_cur = lax.dot_general(q, k_cur, _CONTRACT_LAST, preferred_element_type=F32) + diag_bias
    return s_past, s_cur


def _history_output(s_past, s_cur, v_past, v_cur, v_time_minor):
    m = jnp.maximum(jnp.max(s_past, axis=-1, keepdims=True), jnp.max(s_cur, axis=-1, keepdims=True))
    p_past = jnp.exp2(s_past - m)
    p_cur = jnp.exp2(s_cur - m)
    l = jnp.sum(p_past, axis=-1, keepdims=True) + jnp.sum(p_cur, axis=-1, keepdims=True)
    if v_time_minor:
        o = lax.dot_general(p_past.astype(BF16), v_past, _CONTRACT_LAST, preferred_element_type=F32)
    else:
        o = jnp.dot(p_past.astype(BF16), v_past, preferred_element_type=F32)
    return (o + jnp.dot(p_cur.astype(BF16), v_cur, preferred_element_type=F32)) / l


def _pipelined(n, first, second, lookahead=2):
    pending = [first(h) for h in range(min(lookahead, n))]
    outs = []
    for h in range(n):
        if h + lookahead < n:
            pending.append(first(h + lookahead))
        outs.append(second(h, pending[h]))
    return outs


def _diag_indices(nk, nq, k0, q0, key_major):
    shape = (nk, nq) if key_major else (nq, nk)
    rowi, coli = _row_iota(shape), _lane_iota(shape)
    return (coli + q0, rowi + k0) if key_major else (rowi + q0, coli + k0)


def _diag_pieces(tq):
    half = tq // 2
    return _diag_indices(half, tq, 0, 0, True), _diag_indices(half, half, half, half, True)


def _causal_bias(qi, ki):
    return jnp.where(ki <= qi, 0.0, NEG_BIG)


def _diff_diag_bias(qi, ki, slope):
    ahead = jnp.maximum(ki - qi, 0).astype(F32)
    visible = (ki // CHUNK) <= (qi // CHUNK)
    return jnp.where(visible, (-2.0 * LOG2E) * slope * ahead, NEG_BIG)


def _diff_finish(o1, o2, lam_ref, sub_ref):
    lq1, lk1, lq2, lk2 = (lam_ref[k:k + 1, :] for k in range(4))
    lam = (jnp.exp(jnp.sum(lq1 * lk1, axis=-1, keepdims=True))
           - jnp.exp(jnp.sum(lq2 * lk2, axis=-1, keepdims=True)) + LAMBDA_INIT)
    o = o1 - lam * o2
    o = o * lax.rsqrt(jnp.mean(o * o, axis=-1, keepdims=True) + EPS) * sub_ref[...] * (1.0 - LAMBDA_INIT)
    return o.astype(BF16)


def _fox_kernel(q_ref, k_ref, v_ref, o_ref, *, tq):
    bias = tuple(_causal_bias(qi, ki) for qi, ki in _diag_pieces(tq))
    lane = _lane_iota((tq, LANES))

    def emit(n, outs):
        o_ref[0, n * tq:(n + 1) * tq, :] = jnp.where(lane < SLOT_W, outs[0], outs[1]).astype(BF16)

    _attend_pair(q_ref, k_ref, v_ref, bias, emit, tq=tq)


def _fox_history_kernel(q_ref, kc_ref, vc_ref, kp_ref, vp_ref, c_ref, o_ref, *, ts):
    causal = _causal_bias(*_diag_indices(ts, ts, 0, 0, False))
    lane = _lane_iota((ts, LANES))

    def scores(h):
        return _history_scores(q_ref[0, h], kp_ref[0, h].astype(BF16), c_ref[0, h:h + 1, :] * -LOG2E,
                               kc_ref[0, h], causal)

    pair_values = {}

    def output(h, s):
        g = h // 2
        if g not in pair_values:
            pair_values[g] = jnp.concatenate([vp_ref[0, 2 * g], vp_ref[0, 2 * g + 1]], axis=0).astype(BF16)
        return _history_output(*s, pair_values[g], vc_ref[0, :, LANES * g:LANES * (g + 1)], True)

    outs = _pipelined(H_A, scores, output)
    for g in range(H_A // 2):
        o_ref[0, :, LANES * g:LANES * (g + 1)] = jnp.where(
            lane < SLOT_W, outs[2 * g], outs[2 * g + 1]).astype(BF16)


def _diff_kernel(q_ref, k_ref, v_ref, lam_ref, sub_ref, o_ref, *, tq):
    hd = pl.program_id(1)
    slope = jnp.float32(ALIBI_SLOPES[0])
    for k in range(1, H_B):
        slope = jnp.where(hd == k, jnp.float32(ALIBI_SLOPES[k]), slope)
    bias = tuple(_diff_diag_bias(qi, ki, slope) for qi, ki in _diag_pieces(tq))

    def emit(n, outs):
        o_ref[0, n * tq:(n + 1) * tq, :] = _diff_finish(outs[0], outs[1], lam_ref, sub_ref)

    _attend_pair(q_ref, k_ref, v_ref, bias, emit, tq=tq)


def _diff_history_kernel(q_ref, kc_ref, vc_ref, kp_ref, vp_ref, lam_ref, sub_ref, o_ref, *, ts):
    plen = kp_ref.shape[3]
    pos = _lane_iota((1, plen)).astype(F32)
    qi, ki = _diag_indices(ts, ts, 0, 0, False)

    def scores(j):
        slope = ALIBI_SLOPES[j // 2]
        return _history_scores(q_ref[0, j], kp_ref[0, j].astype(BF16), pos * (slope * LOG2E),
                               kc_ref[0, j], _diff_diag_bias(qi, ki, slope))

    head_values = {}

    def output(j, s):
        h = j // 2
        if h not in head_values:
            head_values[h] = vp_ref[0, pl.ds(h, plen, stride=H_B), :].astype(BF16)
        return _history_output(*s, head_values[h], vc_ref[0, :, DV_B * h:DV_B * (h + 1)], False)

    outs = _pipelined(HEAD_SLOTS, scores, output)
    for h in range(H_B):
        o_ref[0, :, DV_B * h:DV_B * (h + 1)] = _diff_finish(outs[2 * h], outs[2 * h + 1], lam_ref, sub_ref)


def _attention(kind, q_aug, k_aug, v_cur, extras, tq):
    bx, _, seq, _ = k_aug.shape
    groups = HEAD_SLOTS // 2
    nblk = seq // tq
    slab_rows = LANES + ONES_ROWS
    assert v_cur.shape == (bx, nblk, groups, slab_rows, tq)
    grid = (bx, groups)
    in_specs = [
        pl.BlockSpec((1, 2, LANES, seq), lambda b, g: (b, g, 0, 0)),
        pl.BlockSpec((1, 2, seq, LANES), lambda b, g: (b, g, 0, 0)),
        pl.BlockSpec((1, nblk, 1, slab_rows, tq), lambda b, g: (b, 0, g, 0, 0)),
    ] + [pl.BlockSpec(a.shape, lambda b, g: (0,) * a.ndim) for a in extras]
    body = _fox_kernel if kind == "fox" else _diff_kernel
    return pl.pallas_call(
        functools.partial(body, tq=tq),
        grid=grid,
        in_specs=in_specs,
        out_specs=pl.BlockSpec((1, seq, LANES), lambda b, g: (b, 0, g)),
        out_shape=jax.ShapeDtypeStruct((bx, seq, groups * LANES), BF16),
        compiler_params=pltpu.CompilerParams(
            dimension_semantics=("arbitrary", "arbitrary"), vmem_limit_bytes=VMEM_LIMIT),
        name=kind + "_attn",
    )(q_aug, k_aug, v_cur, *extras)


def _history_attention(kind, q_aug, k_aug, v_cur, kt_past, v_past, extras):
    bx, _, ts, _ = q_aug.shape
    whole = lambda a: pl.BlockSpec((1,) + a.shape[1:], lambda b: (b,) + (0,) * (a.ndim - 1))
    shared = lambda a: pl.BlockSpec(a.shape, lambda b: (0,) * a.ndim)
    args = [q_aug, k_aug, v_cur, kt_past, v_past]
    in_specs = [whole(a) for a in args] + [whole(a) if kind == "fox" else shared(a) for a in extras]
    body = _fox_history_kernel if kind == "fox" else _diff_history_kernel
    return pl.pallas_call(
        functools.partial(body, ts=ts),
        grid=(bx,),
        in_specs=in_specs,
        out_specs=pl.BlockSpec((1, ts, v_cur.shape[2]), lambda b: (b, 0, 0)),
        out_shape=jax.ShapeDtypeStruct((bx, ts, v_cur.shape[2]), BF16),
        compiler_params=pltpu.CompilerParams(
            dimension_semantics=("arbitrary",), vmem_limit_bytes=VMEM_LIMIT),
        name=kind + "_history_attn",
    )(*args, *extras)


def _merge_ffn_kernel(x_ref, oa_ref, ob_ref, sig_ref, woa_ref, wob_ref, wout_ref, gffn_ref,
                      wup_ref, cw_ref, cb_ref, wdown_ref, st_ref,
                      y_ref, ns_ref, carry_ref, ubuf_ref, act_ref, *, tm, seq, d_ff):
    t = pl.program_id(0)
    carry_mode = seq >= tm
    nb = 1 if carry_mode else tm // seq
    rows = tm if carry_mode else seq
    tpb = max(1, seq // tm)
    d = x_ref.shape[1]
    seg = rows + SUBLANES

    if carry_mode:
        @pl.when(t % tpb == 0)
        def _():
            carry_ref[0:2, :] = st_ref[0]

    ya = jnp.dot(oa_ref[...], woa_ref[...], preferred_element_type=F32)
    yb = jnp.dot(ob_ref[...], wob_ref[...], preferred_element_type=F32)
    m = sig_ref[:, :d].astype(F32) * ya + sig_ref[:, d:].astype(F32) * yb
    x1 = x_ref[...] + jnp.dot(m.astype(BF16), wout_ref[...], preferred_element_type=F32)
    h = (x1 * lax.rsqrt(jnp.mean(x1 * x1, axis=-1, keepdims=True) + EPS) * gffn_ref[...]).astype(BF16)

    def up(c0):
        return (jnp.dot(h, wup_ref[:, c0:c0 + FF_CHUNK], preferred_element_type=F32),
                jnp.dot(h, wup_ref[:, d_ff + c0:d_ff + c0 + FF_CHUNK], preferred_element_type=F32))

    def conv(u, c0, buf):
        cols = slice(c0, c0 + FF_CHUNK)
        w0, w1, w2 = (cw_ref[k:k + 1, cols] for k in range(3))
        outs = []
        for bi in range(nb):
            useg = u[bi * rows:(bi + 1) * rows, :]
            r0 = bi * seg + SUBLANES
            hist = carry_ref[0:2, cols] if carry_mode else st_ref[bi, :, cols]
            ubuf_ref[buf, r0 - 2:r0, :] = hist
            ubuf_ref[buf, r0:r0 + rows, :] = useg
            ns_ref[bi, :, cols] = useg[rows - 2:rows, :]
            u1 = ubuf_ref[buf, r0 - 1:r0 - 1 + rows, :]
            u2 = ubuf_ref[buf, r0 - 2:r0 - 2 + rows, :]
            outs.append(w0 * u2 + w1 * u1 + w2 * useg + cb_ref[:, cols])
        if carry_mode:
            carry_ref[0:2, cols] = u[tm - 2:tm, :]
        return outs[0] if nb == 1 else jnp.concatenate(outs, axis=0)

    chunks = list(range(0, d_ff, FF_CHUNK))
    split = chunks[len(chunks) // 2]
    nxt = up(chunks[0])
    for k, c0 in enumerate(chunks):
        ua, ub = nxt
        if k + 1 < len(chunks):
            nxt = up(chunks[k + 1])
        if c0 == split:
            y_half = x1 + jnp.dot(act_ref[:, :split], wdown_ref[:split, :], preferred_element_type=F32)
        a = conv(ua, c0, 2 * (k % 2))
        b = conv(ub, d_ff + c0, 2 * (k % 2) + 1)
        act_ref[:, c0:c0 + FF_CHUNK] = (a * jax.nn.sigmoid(a) * b).astype(BF16)
    y_ref[...] = y_half + jnp.dot(act_ref[:, split:], wdown_ref[split:, :], preferred_element_type=F32)


def _merge_ffn(x2, oa, ob, sig, state, seq, prm, tm):
    n, d = x2.shape
    d_ff = prm["w_down"].shape[0]
    assert d_ff % FF_CHUNK == 0 and seq >= 2
    carry_mode = seq >= tm
    nb = 1 if carry_mode else tm // seq
    rows = tm if carry_mode else seq
    tpb = max(1, seq // tm)
    bx = n // seq
    grid = (n // tm,)
    const = lambda a: pl.BlockSpec(a.shape, lambda t: (0,) * a.ndim, pipeline_mode=pl.Buffered(1))
    rowblk = lambda w: pl.BlockSpec((tm, w), lambda t: (t, 0))
    if carry_mode:
        st_spec = pl.BlockSpec((1, 2, 2 * d_ff), lambda t: (t // tpb, 0, 0))
    else:
        st_spec = pl.BlockSpec((nb, 2, 2 * d_ff), lambda t: (t, 0, 0))
    consts1 = [prm["w_oa"], prm["w_ob"], prm["w_out"], prm["g_ffn"], prm["w_up"], prm["conv_w"],
               prm["conv_b"], prm["w_down"]]
    return pl.pallas_call(
        functools.partial(_merge_ffn_kernel, tm=tm, seq=seq, d_ff=d_ff),
        grid=grid,
        in_specs=[rowblk(d), rowblk(W_A), rowblk(W_B), rowblk(2 * d)] + [const(a) for a in consts1] + [st_spec],
        out_specs=[rowblk(d), st_spec],
        out_shape=[jax.ShapeDtypeStruct((n, d), F32), jax.ShapeDtypeStruct((bx, 2, 2 * d_ff), F32)],
        scratch_shapes=[pltpu.VMEM((SUBLANES, 2 * d_ff), F32),
                        pltpu.VMEM((4, nb * (rows + SUBLANES), FF_CHUNK), F32),
                        pltpu.VMEM((tm, d_ff), BF16)],
        compiler_params=pltpu.CompilerParams(
            dimension_semantics=("arbitrary",), vmem_limit_bytes=VMEM_LIMIT),
        name="merge_ffn",
    )(x2, oa, ob, sig, *consts1, state)


def _tri(tm, seg):
    i = np.arange(tm)
    return jnp.asarray((i[None, :] <= i[:, None]) & (i[None, :] // seg == i[:, None] // seg), BF16)


def _piece_placer():
    p = np.zeros((LANES, HEAD_SLOTS * SLOT_W), np.float32)
    for k in range(N_CUM_PIECES):
        for h in range(HEAD_SLOTS):
            p[k * HEAD_SLOTS + h, h * SLOT_W + k] = -1.0
    return jnp.asarray(p, BF16)


def _block_diag_ones(width, blk):
    i = np.arange(width)
    return jnp.asarray(i[:, None] // blk == i[None, :] // blk, BF16)


def _rep3(a):
    pad = jnp.zeros(a.shape[:-1] + (LANES - N_CUM_PIECES * HEAD_SLOTS,), a.dtype)
    return jnp.concatenate([a] * N_CUM_PIECES + [pad], axis=-1)


def _tiles(seq):
    return dict(tm=512, tq=min(512, seq))


def _time_minor(a):
    return jnp.moveaxis(a, 1, -1)


def _time_major(a):
    return jnp.moveaxis(a, -1, 1)


def _layer(x, pos_off, cinit, past, state, prm, lam_pack, tiles):
    bx, seq, d = x.shape
    tm, tq = tiles["tm"], tiles["tq"]
    x2 = x.reshape(bx * seq, d)
    ka, va, logf, kb, vb, qaa, kaa, qba, kba, vab, vbb, sig = _in_proj(
        x2, seq, pos_off, cinit, prm, tm, key_major=past is None)
    diff_prm = (lam_pack, prm["subln_b"])
    if past is None:
        assert tq == tm
        oa = _attention("fox", qaa, kaa, vab, (), tq)
        ob = _attention("diff", qba, kba, vbb, diff_prm, tq)
    else:
        vab = vab.reshape(bx, seq, W_A)
        vbb = vbb.reshape(bx, seq, W_B)
        oa = _history_attention("fox", qaa, kaa, vab, past["kt_a"], past["vt_a"], (past["c"],))
        ob = _history_attention("diff", qba, kba, vbb, past["kt_b"], past["v_b"], diff_prm)
    y, new_state = _merge_ffn(x2, oa.reshape(bx * seq, W_A), ob.reshape(bx * seq, W_B), sig, state, seq, prm, tm)
    if seq >= LANES:
        ka, va, kb = _time_major(ka), _time_major(va), _time_major(kb.reshape(bx, H_B, 2, DH_B, seq))
    return (y.reshape(bx, seq, d), ka.reshape(bx, seq, H_A, DH_A), va.reshape(bx, seq, H_A, DH_A),
            _time_major(logf), kb.reshape(bx, seq, H_B, 2, DH_B), vb.reshape(bx, seq, H_B, DV_B), new_state)


def kernel(x_prompt, x_sample, cache_a_k, cache_a_v, cache_a_logf, cache_b_k, cache_b_v, state_ffn_conv,
           g_attn, w_in, b_f, qn_a, kn_a, qn_b, kn_b, lambda_q1, lambda_k1, lambda_q2, lambda_k2,
           subln_b, w_oa, w_ob, w_out, g_ffn, w_up, conv_w, conv_b, w_down):
    bp, tp_, d = x_prompt.shape
    bs, ts, _ = x_sample.shape
    plen = cache_a_k.shape[1]
    d_ff = w_down.shape[0]

    f0, f1 = 3 * W_A, 3 * W_A + H_A
    prm = {
        "g_attn": g_attn.reshape(1, d),
        "w_main": jnp.concatenate([w_in[:, :f0], w_in[:, f1:]], axis=1).astype(BF16),
        "w_f": _rep3(w_in[:, f0:f1]).astype(BF16),
        "b_f": _rep3(b_f.reshape(1, H_A)),
        "qn_a": jnp.tile(qn_a, H_A).reshape(1, W_A), "kn_a": jnp.tile(kn_a, H_A).reshape(1, W_A),
        "qn_b": jnp.tile(qn_b, 2 * H_B).reshape(1, W_B), "kn_b": jnp.tile(kn_b, 2 * H_B).reshape(1, W_B),
        "bd": _block_diag_ones(2 * LANES, DH_A),
        "pc": _piece_placer(),
        "subln_b": subln_b.reshape(1, DV_B),
        "w_oa": w_oa.astype(BF16), "w_ob": w_ob.astype(BF16), "w_out": w_out.astype(BF16),
        "g_ffn": g_ffn.reshape(1, d), "w_up": w_up.astype(BF16), "conv_w": conv_w,
        "conv_b": conv_b.reshape(1, 2 * d_ff), "w_down": w_down.astype(BF16),
    }
    lam_pack = jnp.stack([lambda_q1, lambda_k1, lambda_q2, lambda_k2])

    tl = _tiles(tp_)
    prm_p = dict(prm, ltri_in=_tri(tl["tm"], min(tl["tm"], tp_)))
    zeros_c = jnp.zeros((bp, 1, LANES), F32)
    zeros_state = jnp.zeros((bp, 2, 2 * d_ff), F32)
    (y_p, ka_p, va_p, lf_p, kb_p, vb_p, st_p) = _layer(
        x_prompt, 0, zeros_c, None, zeros_state, prm_p, lam_pack, tl)

    tl = _tiles(ts)
    c_past, c_tot = _cache_cumsum(_time_minor(cache_a_logf).reshape(bs * H_A, plen), min(512, plen))
    past = {
        "kt_a": _time_minor(cache_a_k), "vt_a": _time_minor(cache_a_v), "c": c_past.reshape(bs, H_A, plen),
        "kt_b": _time_minor(cache_b_k).reshape(bs, HEAD_SLOTS, DH_B, plen),
        "v_b": cache_b_v.reshape(bs, plen * H_B, DV_B),
    }
    prm_s = dict(prm, ltri_in=_tri(tl["tm"], min(tl["tm"], ts)))
    cinit_rows = jnp.repeat(_rep3(c_tot[:, 0].reshape(bs, H_A)), ts, axis=0)
    (y_s, ka_s, va_s, lf_s, kb_s, vb_s, st_s) = _layer(
        x_sample, plen, cinit_rows, past, state_ffn_conv, prm_s, lam_pack, tl)

    return (y_p, y_s, ka_p, va_p, lf_p, kb_p, vb_p, st_p, ka_s, va_s, lf_s, kb_s, vb_s, st_s)
```

```python
import functools
import math

import jax
import jax.numpy as jnp
import numpy as np
from jax import lax
from jax.experimental import pallas as pl
from jax.experimental.pallas import tpu as pltpu

F32 = jnp.float32
BF16 = jnp.bfloat16

CHUNK = 64
H_A = 8
DH_A = 64
H_B = 4
DH_B = 64
DV_B = 128
W_A = H_A * DH_A
W_B = H_B * DV_B
EPS = 1e-6
LAMBDA_INIT = 0.8 - 0.6 * math.exp(-0.3 * 0)
ALIBI_SLOPES = tuple(2.0 ** (-8.0 * (i + 1) / H_B) for i in range(H_B))

LANES = 128
SUBLANES = 8
HEAD_SLOTS = 8
SLOT_W = 64
N_CUM_PIECES = 3
N_POS_PIECES = 3
NEG_BIG = -1e30
LOG2E = math.log2(math.e)
ONES_ROWS = 16
VMEM_LIMIT = 58 * 1024 * 1024
FF_CHUNK = 256


def _lane_iota(shape):
    return lax.broadcasted_iota(jnp.int32, shape, len(shape) - 1)


def _row_iota(shape):
    return lax.broadcasted_iota(jnp.int32, shape, len(shape) - 2)


def _split3(c):
    hi = c.astype(BF16).astype(F32)
    r1 = c - hi
    lo = r1.astype(BF16).astype(F32)
    lo2 = (r1 - lo).astype(BF16).astype(F32)
    return hi, lo, lo2


def _cumsum_pieces(lf, ltri_ref, base):
    hi, lo, lo2 = _split3(lf)
    ltri = ltri_ref[...]
    c = (jnp.dot(ltri, hi.astype(BF16), preferred_element_type=F32)
         + jnp.dot(ltri, lo.astype(BF16), preferred_element_type=F32)
         + jnp.dot(ltri, lo2.astype(BF16), preferred_element_type=F32)) + base
    chi, clo, clo2 = _split3(c * LOG2E)
    lane = _lane_iota(c.shape)
    pieces = jnp.where(lane < HEAD_SLOTS, chi, jnp.where(lane < 2 * HEAD_SLOTS, clo, clo2))
    return c, pieces.astype(BF16)


def _slot(z, j):
    return z[:, SLOT_W * j:SLOT_W * (j + 1)]


def _pos_extra(pos, slope):
    hi, lo, lo2 = _split3(pos.astype(F32) * (slope * LOG2E))
    lane = _lane_iota(hi.shape)
    return jnp.where(lane == 0, hi, jnp.where(lane == 1, lo, jnp.where(lane == 2, lo2, 0.0)))


def _store_tile(ref, slot64, extra64, j, nb, rows, time_minor=False):
    tile = jnp.concatenate([slot64, extra64], axis=1)
    if time_minor:
        tile_t = tile.T.astype(BF16)
        for bi in range(nb):
            ref[bi, j, :, :] = tile_t[:, bi * rows:(bi + 1) * rows]
    else:
        tile = tile.astype(BF16)
        for bi in range(nb):
            ref[bi, j, :, :] = tile[bi * rows:(bi + 1) * rows, :]


def _emit_fox_keys(kaa_ref, slots, pieces, pc_ref, nb, rows):
    extra = jnp.dot(pieces, pc_ref[...], preferred_element_type=F32)
    for j in range(HEAD_SLOTS):
        _store_tile(kaa_ref, slots[j], _slot(extra, j), j, nb, rows)


def _emit_diff_keys(kba_ref, slots, pos, nb, rows):
    for h in range(H_B):
        extra = _pos_extra(pos, ALIBI_SLOPES[h])
        for m in range(2):
            _store_tile(kba_ref, slots[2 * h + m], extra, 2 * h + m, nb, rows)


def _emit_queries(q_ref, qn, n_ones, nb, rows, time_minor):
    lane = _lane_iota((qn.shape[0], SLOT_W))
    ones = jnp.where(lane < n_ones, 1.0, 0.0)
    for j in range(HEAD_SLOTS):
        _store_tile(q_ref, _slot(qn, j), ones, j, nb, rows, time_minor)


def _cache_cumsum_kernel(lf_ref, utri_ref, c_ref, tot_ref, carry_ref, *, tb):
    t = pl.program_id(0)

    @pl.when(t == 0)
    def _():
        carry_ref[...] = jnp.zeros_like(carry_ref)

    hi, lo, lo2 = _split3(lf_ref[...])
    u = utri_ref[...]
    c = (jnp.dot(hi.astype(BF16), u, preferred_element_type=F32)
         + jnp.dot(lo.astype(BF16), u, preferred_element_type=F32)
         + jnp.dot(lo2.astype(BF16), u, preferred_element_type=F32)) + carry_ref[...]
    c_ref[...] = c
    last = c[:, tb - 1:tb]
    carry_ref[...] = last
    tot_ref[...] = jnp.broadcast_to(last, tot_ref.shape)


def _cache_cumsum(lf_t, tb):
    r, p = lf_t.shape
    i = np.arange(tb)
    utri = jnp.asarray(i[:, None] <= i[None, :], BF16)
    return pl.pallas_call(
        functools.partial(_cache_cumsum_kernel, tb=tb),
        grid=(p // tb,),
        in_specs=[pl.BlockSpec((r, tb), lambda t: (0, t)),
                  pl.BlockSpec((tb, tb), lambda t: (0, 0), pipeline_mode=pl.Buffered(1))],
        out_specs=[pl.BlockSpec((r, tb), lambda t: (0, t)), pl.BlockSpec((r, LANES), lambda t: (0, 0))],
        out_shape=[jax.ShapeDtypeStruct((r, p), F32), jax.ShapeDtypeStruct((r, LANES), F32)],
        scratch_shapes=[pltpu.VMEM((r, 1), F32)],
        compiler_params=pltpu.CompilerParams(
            dimension_semantics=("arbitrary",), vmem_limit_bytes=VMEM_LIMIT),
        name="cache_cumsum",
    )(lf_t, utri)


def _in_proj_kernel(x_ref, g_ref, wm_ref, wf_ref, bf_ref, qna_ref, kna_ref, qnb_ref, knb_ref,
                    bd_ref, ltri_ref, pc_ref, cinit_ref,
                    ka_ref, va_ref, logf_ref, kb_ref, vb_ref,
                    qaa_ref, kaa_ref, qba_ref, kba_ref, vab_ref, vbb_ref, sig_ref, carry_ref,
                    *, tm, seq, pos_off, key_major):
    t = pl.program_id(0)
    carry_mode = seq >= tm
    nb = 1 if carry_mode else tm // seq
    rows = tm if carry_mode else seq
    tpb = max(1, seq // tm)

    x = x_ref[...]
    h = (x * lax.rsqrt(jnp.mean(x * x, axis=-1, keepdims=True) + EPS) * g_ref[...]).astype(BF16)

    def proj(lo, hi):
        return jnp.dot(h, wm_ref[:, lo:hi], preferred_element_type=F32)

    def head_norm(z, w_ref):
        z2 = (z * z).astype(BF16)
        bw = bd_ref.shape[0]
        ss = jnp.concatenate([jnp.dot(z2[:, k:k + bw], bd_ref[...], preferred_element_type=F32)
                              for k in range(0, z.shape[1], bw)], axis=1)
        return z * lax.rsqrt(ss * (1.0 / DH_A) + EPS) * w_ref[...]

    def store_heads(ref, z, heads):
        w = z.shape[1] // heads
        for j in range(heads):
            ref[pl.ds(j, tm, stride=heads), :] = z[:, w * j:w * (j + 1)]

    def store_time_minor(ref, z):
        zts = [z[bi * rows:(bi + 1) * rows, :].T for bi in range(nb)]
        for bi, zt in enumerate(zts):
            ref[bi] = zt.reshape(z.shape[1] // SLOT_W, SLOT_W, rows)
        return zts

    def store_slots(ref, z):
        if seq >= LANES:
            return store_time_minor(ref, z)
        store_heads(ref, z, z.shape[1] // SLOT_W)
        return None

    def store_value_slabs(ref, vt):
        ones = jnp.ones((ONES_ROWS, tm), BF16)
        for g in range(W_A // LANES):
            ref[0, 0, g, 0:LANES, :] = vt[LANES * g:LANES * (g + 1), :].astype(BF16)
            ref[0, 0, g, LANES:LANES + ONES_ROWS, :] = ones

    zf = jnp.dot(h, wf_ref[...], preferred_element_type=F32) + bf_ref[...]
    lf = jnp.minimum(zf, 0.0) - jnp.log1p(jnp.exp(-jnp.abs(zf)))
    lane = _lane_iota(lf.shape)
    lf = jnp.where(lane < N_CUM_PIECES * HEAD_SLOTS, lf, 0.0)
    for bi in range(nb):
        logf_ref[bi] = lf[bi * rows:(bi + 1) * rows, :].T[:H_A, :]
    if carry_mode:
        @pl.when(t % tpb == 0)
        def _():
            carry_ref[...] = cinit_ref[0]
        base = carry_ref[...]
    else:
        base = cinit_ref[...]
    c, pieces = _cumsum_pieces(lf, ltri_ref, base)
    if carry_mode:
        carry_ref[...] = c[tm - 1:tm, :]

    row = t * tm + _row_iota((tm, SLOT_W))
    pos = (row & (seq - 1)) + pos_off

    qn = head_norm(proj(0, W_A), qna_ref) * (DH_A ** -0.5 * LOG2E)
    _emit_queries(qaa_ref, qn, N_CUM_PIECES, nb, rows, key_major)
    kn = head_norm(proj(W_A, 2 * W_A), kna_ref)
    store_slots(ka_ref, kn)
    _emit_fox_keys(kaa_ref, [_slot(kn, j) for j in range(HEAD_SLOTS)], pieces, pc_ref, nb, rows)
    v = proj(2 * W_A, 3 * W_A)
    vts = store_slots(va_ref, v)
    if key_major:
        store_value_slabs(vab_ref, vts[0])
    else:
        vab_ref[...] = v.astype(BF16)

    o = 3 * W_A
    qn = head_norm(proj(o, o + W_B), qnb_ref) * (DH_B ** -0.5 * LOG2E)
    _emit_queries(qba_ref, qn, N_POS_PIECES, nb, rows, key_major)
    kn = head_norm(proj(o + W_B, o + 2 * W_B), knb_ref)
    store_slots(kb_ref, kn)
    _emit_diff_keys(kba_ref, [_slot(kn, j) for j in range(HEAD_SLOTS)], pos, nb, rows)
    v = proj(o + 2 * W_B, o + 3 * W_B)
    store_heads(vb_ref, v, H_B)
    if key_major:
        store_value_slabs(vbb_ref, v.T)
    else:
        vbb_ref[...] = v.astype(BF16)

    o = 3 * W_A + 3 * W_B
    d = (wm_ref.shape[1] - o) // 2
    for k in range(2):
        sig_ref[:, k * d:(k + 1) * d] = jax.nn.sigmoid(proj(o + k * d, o + (k + 1) * d)).astype(BF16)


def _in_proj(x2, seq, pos_off, cinit, prm, tm, key_major):
    n, d = x2.shape
    carry_mode = seq >= tm
    nb = 1 if carry_mode else tm // seq
    tpb = max(1, seq // tm)
    bx = n // seq
    assert n % tm == 0 and seq & (seq - 1) == 0
    grid = (n // tm,)
    const = lambda a: pl.BlockSpec(a.shape, lambda t: (0,) * a.ndim, pipeline_mode=pl.Buffered(1))
    rowblk = lambda w, mult=1: pl.BlockSpec((tm * mult, w), lambda t: (t, 0))
    if carry_mode:
        cinit_spec = pl.BlockSpec((1, 1, LANES), lambda t: (t // tpb, 0, 0))
        aug_spec = pl.BlockSpec((1, HEAD_SLOTS, tm, LANES), lambda t: (t // tpb, 0, t % tpb, 0))
    else:
        cinit_spec = rowblk(LANES)
        aug_spec = pl.BlockSpec((nb, HEAD_SLOTS, seq, LANES), lambda t: (t, 0, 0, 0))
    aug_shape = jax.ShapeDtypeStruct((bx, HEAD_SLOTS, seq, LANES), BF16)
    if carry_mode:
        tmin_spec = pl.BlockSpec((1, HEAD_SLOTS, SLOT_W, tm), lambda t: (t // tpb, 0, 0, t % tpb))
        lf_spec = pl.BlockSpec((1, H_A, tm), lambda t: (t // tpb, 0, t % tpb))
    else:
        tmin_spec = pl.BlockSpec((nb, HEAD_SLOTS, SLOT_W, seq), lambda t: (t, 0, 0, 0))
        lf_spec = pl.BlockSpec((nb, H_A, seq), lambda t: (t, 0, 0))
    tmin_shape = jax.ShapeDtypeStruct((bx, HEAD_SLOTS, SLOT_W, seq), F32)
    if seq < LANES:
        tmin_spec = rowblk(SLOT_W, HEAD_SLOTS)
        tmin_shape = jax.ShapeDtypeStruct((n * HEAD_SLOTS, SLOT_W), F32)
    if key_major:
        assert carry_mode
        q_spec = pl.BlockSpec((1, HEAD_SLOTS, LANES, tm), lambda t: (t // tpb, 0, 0, t % tpb))
        q_shape = jax.ShapeDtypeStruct((bx, HEAD_SLOTS, LANES, seq), BF16)
        slab = (W_A // LANES, LANES + ONES_ROWS, tm)
        v_spec = pl.BlockSpec((1, 1) + slab, lambda t: (t // tpb, t % tpb, 0, 0, 0))
        v_shape = jax.ShapeDtypeStruct((bx, tpb) + slab, BF16)
    else:
        q_spec, q_shape = aug_spec, aug_shape
        v_spec = rowblk(W_A)
        v_shape = jax.ShapeDtypeStruct((n, W_A), BF16)
    consts = [prm["g_attn"], prm["w_main"], prm["w_f"], prm["b_f"], prm["qn_a"], prm["kn_a"],
              prm["qn_b"], prm["kn_b"], prm["bd"], prm["ltri_in"], prm["pc"]]
    return pl.pallas_call(
        functools.partial(_in_proj_kernel, tm=tm, seq=seq, pos_off=pos_off, key_major=key_major),
        grid=grid,
        in_specs=[rowblk(d)] + [const(a) for a in consts] + [cinit_spec],
        out_specs=[tmin_spec, tmin_spec, lf_spec, tmin_spec, rowblk(DV_B, H_B),
                   q_spec, aug_spec, q_spec, aug_spec, v_spec, v_spec, rowblk(2 * d)],
        out_shape=[
            tmin_shape, tmin_shape, jax.ShapeDtypeStruct((bx, H_A, seq), F32),
            tmin_shape, jax.ShapeDtypeStruct((n * H_B, DV_B), F32),
            q_shape, aug_shape, q_shape, aug_shape, v_shape, v_shape,
            jax.ShapeDtypeStruct((n, 2 * d), BF16),
        ],
        scratch_shapes=[pltpu.VMEM((1, LANES), F32)],
        compiler_params=pltpu.CompilerParams(
            dimension_semantics=("arbitrary",), vmem_limit_bytes=VMEM_LIMIT),
        name="in_proj",
    )(x2, *consts, cinit)


_CONTRACT_LAST = (((1,), (1,)), ((), ()))


def _attend_pair(q_ref, k_ref, v_ref, diag_bias, emit, *, tq):
    half = tq // 2
    bias_top, bias_bot = diag_bias

    def scores(c, n, j):
        qt = q_ref[0, c, :, n * tq:(n + 1) * tq]
        k0 = j * tq
        if j < n:
            return [(jnp.dot(k_ref[0, c, k0:k0 + tq, :], qt, preferred_element_type=F32), 0, 0)]
        top = jnp.dot(k_ref[0, c, k0:k0 + half, :], qt, preferred_element_type=F32) + bias_top
        bot = jnp.dot(k_ref[0, c, k0 + half:k0 + tq, :], qt[:, half:], preferred_element_type=F32) + bias_bot
        return [(top, 0, 0), (bot, half, half)]

    def absorb(m, acc, piece, vt):
        s, key0, q0 = piece
        m_old, acc_old = m[:, q0:], acc[:, q0:]
        m_new = jnp.maximum(m_old, jnp.max(s, axis=0, keepdims=True))
        p = jnp.exp2(s - m_new).astype(BF16)
        pv = jnp.dot(vt[:, key0:key0 + s.shape[0]], p, preferred_element_type=F32)
        acc_new = jnp.exp2(m_old - m_new) * acc_old + pv
        if q0:
            m_new = jnp.concatenate([m[:, :q0], m_new], axis=1)
            acc_new = jnp.concatenate([acc[:, :q0], acc_new], axis=1)
        return m_new, acc_new

    work = [(n, j) for n in range(v_ref.shape[1]) for j in range(n + 1)]
    m, acc = [None, None], [None, None]

    def first(t):
        return [scores(c, *work[t]) for c in range(2)]

    def second(t, s):
        n, j = work[t]
        vt = v_ref[0, j, 0]
        for c in range(2):
            if j == 0:
                m[c] = jnp.full((1, tq), NEG_BIG, F32)
                acc[c] = jnp.zeros((LANES + ONES_ROWS, tq), F32)
            for piece in s[c]:
                m[c], acc[c] = absorb(m[c], acc[c], piece, vt)
        if j == n:
            emit(n, [(acc[c][:LANES] / acc[c][LANES:LANES + 1]).T for c in range(2)])

    _pipelined(len(work), first, second, lookahead=1)


def _history_scores(q, kt_past, bias_row, k_cur, diag_bias):
    s_past = jnp.dot(q[:, :SLOT_W], kt_past, preferred_element_type=F32) + bias_row
    s_cur = lax.dot_general(q, k_cur, _CONTRACT_LAST, preferred_element_type=F32) + diag_bias
    return s_past, s_cur


def _history_output(s_past, s_cur, v_past, v_cur, v_time_minor):
    m = jnp.maximum(jnp.max(s_past, axis=-1, keepdims=True), jnp.max(s_cur, axis=-1, keepdims=True))
    p_past = jnp.exp2(s_past - m)
    p_cur = jnp.exp2(s_cur - m)
    l = jnp.sum(p_past, axis=-1, keepdims=True) + jnp.sum(p_cur, axis=-1, keepdims=True)
    if v_time_minor:
        o = lax.dot_general(p_past.astype(BF16), v_past, _CONTRACT_LAST, preferred_element_type=F32)
    else:
        o = jnp.dot(p_past.astype(BF16), v_past, preferred_element_type=F32)
    return (o + jnp.dot(p_cur.astype(BF16), v_cur, preferred_element_type=F32)) / l


def _pipelined(n, first, second, lookahead=2):
    pending = [first(h) for h in range(min(lookahead, n))]
    outs = []
    for h in range(n):
        if h + lookahead < n:
            pending.append(first(h + lookahead))
        outs.append(second(h, pending[h]))
    return outs


def _diag_indices(nk, nq, k0, q0, key_major):
    shape = (nk, nq) if key_major else (nq, nk)
    rowi, coli = _row_iota(shape), _lane_iota(shape)
    return (coli + q0, rowi + k0) if key_major else (rowi + q0, coli + k0)


def _diag_pieces(tq):
    half = tq // 2
    return _diag_indices(half, tq, 0, 0, True), _diag_indices(half, half, half, half, True)


def _causal_bias(qi, ki):
    return jnp.where(ki <= qi, 0.0, NEG_BIG)


def _diff_diag_bias(qi, ki, slope):
    ahead = jnp.maximum(ki - qi, 0).astype(F32)
    visible = (ki // CHUNK) <= (qi // CHUNK)
    return jnp.where(visible, (-2.0 * LOG2E) * slope * ahead, NEG_BIG)


def _diff_finish(o1, o2, lam_ref, sub_ref):
    lq1, lk1, lq2, lk2 = (lam_ref[k:k + 1, :] for k in range(4))
    lam = (jnp.exp(jnp.sum(lq1 * lk1, axis=-1, keepdims=True))
           - jnp.exp(jnp.sum(lq2 * lk2, axis=-1, keepdims=True)) + LAMBDA_INIT)
    o = o1 - lam * o2
    o = o * lax.rsqrt(jnp.mean(o * o, axis=-1, keepdims=True) + EPS) * sub_ref[...] * (1.0 - LAMBDA_INIT)
    return o.astype(BF16)


def _fox_kernel(q_ref, k_ref, v_ref, o_ref, *, tq):
    bias = tuple(_causal_bias(qi, ki) for qi, ki in _diag_pieces(tq))
    lane = _lane_iota((tq, LANES))

    def emit(n, outs):
        o_ref[0, n * tq:(n + 1) * tq, :] = jnp.where(lane < SLOT_W, outs[0], outs[1]).astype(BF16)

    _attend_pair(q_ref, k_ref, v_ref, bias, emit, tq=tq)


def _fox_history_kernel(q_ref, kc_ref, vc_ref, kp_ref, vp_ref, c_ref, o_ref, *, ts):
    causal = _causal_bias(*_diag_indices(ts, ts, 0, 0, False))
    lane = _lane_iota((ts, LANES))

    def scores(h):
        return _history_scores(q_ref[0, h], kp_ref[0, h].astype(BF16), c_ref[0, h:h + 1, :] * -LOG2E,
                               kc_ref[0, h], causal)

    pair_values = {}

    def output(h, s):
        g = h // 2
        if g not in pair_values:
            pair_values[g] = jnp.concatenate([vp_ref[0, 2 * g], vp_ref[0, 2 * g + 1]], axis=0).astype(BF16)
        return _history_output(*s, pair_values[g], vc_ref[0, :, LANES * g:LANES * (g + 1)], True)

    outs = _pipelined(H_A, scores, output)
    for g in range(H_A // 2):
        o_ref[0, :, LANES * g:LANES * (g + 1)] = jnp.where(
            lane < SLOT_W, outs[2 * g], outs[2 * g + 1]).astype(BF16)


def _diff_kernel(q_ref, k_ref, v_ref, lam_ref, sub_ref, o_ref, *, tq):
    hd = pl.program_id(1)
    slope = jnp.float32(ALIBI_SLOPES[0])
    for k in range(1, H_B):
        slope = jnp.where(hd == k, jnp.float32(ALIBI_SLOPES[k]), slope)
    bias = tuple(_diff_diag_bias(qi, ki, slope) for qi, ki in _diag_pieces(tq))

    def emit(n, outs):
        o_ref[0, n * tq:(n + 1) * tq, :] = _diff_finish(outs[0], outs[1], lam_ref, sub_ref)

    _attend_pair(q_ref, k_ref, v_ref, bias, emit, tq=tq)


def _diff_history_kernel(q_ref, kc_ref, vc_ref, kp_ref, vp_ref, lam_ref, sub_ref, o_ref, *, ts):
    plen = kp_ref.shape[3]
    pos = _lane_iota((1, plen)).astype(F32)
    qi, ki = _diag_indices(ts, ts, 0, 0, False)

    def scores(j):
        slope = ALIBI_SLOPES[j // 2]
        return _history_scores(q_ref[0, j], kp_ref[0, j].astype(BF16), pos * (slope * LOG2E),
                               kc_ref[0, j], _diff_diag_bias(qi, ki, slope))

    head_values = {}

    def output(j, s):
        h = j // 2
        if h not in head_values:
            head_values[h] = vp_ref[0, pl.ds(h, plen, stride=H_B), :].astype(BF16)
        return _history_output(*s, head_values[h], vc_ref[0, :, DV_B * h:DV_B * (h + 1)], False)

    outs = _pipelined(HEAD_SLOTS, scores, output)
    for h in range(H_B):
        o_ref[0, :, DV_B * h:DV_B * (h + 1)] = _diff_finish(outs[2 * h], outs[2 * h + 1], lam_ref, sub_ref)


def _attention(kind, q_aug, k_aug, v_cur, extras, tq):
    bx, _, seq, _ = k_aug.shape
    groups = HEAD_SLOTS // 2
    nblk = seq // tq
    slab_rows = LANES + ONES_ROWS
    assert v_cur.shape == (bx, nblk, groups, slab_rows, tq)
    grid = (bx, groups)
    in_specs = [
        pl.BlockSpec((1, 2, LANES, seq), lambda b, g: (b, g, 0, 0)),
        pl.BlockSpec((1, 2, seq, LANES), lambda b, g: (b, g, 0, 0)),
        pl.BlockSpec((1, nblk, 1, slab_rows, tq), lambda b, g: (b, 0, g, 0, 0)),
    ] + [pl.BlockSpec(a.shape, lambda b, g: (0,) * a.ndim) for a in extras]
    body = _fox_kernel if kind == "fox" else _diff_kernel
    return pl.pallas_call(
        functools.partial(body, tq=tq),
        grid=grid,
        in_specs=in_specs,
        out_specs=pl.BlockSpec((1, seq, LANES), lambda b, g: (b, 0, g)),
        out_shape=jax.ShapeDtypeStruct((bx, seq, groups * LANES), BF16),
        compiler_params=pltpu.CompilerParams(
            dimension_semantics=("arbitrary", "arbitrary"), vmem_limit_bytes=VMEM_LIMIT),
        name=kind + "_attn",
    )(q_aug, k_aug, v_cur, *extras)


def _history_attention(kind, q_aug, k_aug, v_cur, kt_past, v_past, extras):
    bx, _, ts, _ = q_aug.shape
    whole = lambda a: pl.BlockSpec((1,) + a.shape[1:], lambda b: (b,) + (0,) * (a.ndim - 1))
    shared = lambda a: pl.BlockSpec(a.shape, lambda b: (0,) * a.ndim)
    args = [q_aug, k_aug, v_cur, kt_past, v_past]
    in_specs = [whole(a) for a in args] + [whole(a) if kind == "fox" else shared(a) for a in extras]
    body = _fox_history_kernel if kind == "fox" else _diff_history_kernel
    return pl.pallas_call(
        functools.partial(body, ts=ts),
        grid=(bx,),
        in_specs=in_specs,
        out_specs=pl.BlockSpec((1, ts, v_cur.shape[2]), lambda b: (b, 0, 0)),
        out_shape=jax.ShapeDtypeStruct((bx, ts, v_cur.shape[2]), BF16),
        compiler_params=pltpu.CompilerParams(
            dimension_semantics=("arbitrary",), vmem_limit_bytes=VMEM_LIMIT),
        name=kind + "_history_attn",
    )(*args, *extras)


def _merge_ffn_kernel(x_ref, oa_ref, ob_ref, sig_ref, woa_ref, wob_ref, wout_ref, gffn_ref,
                      wup_ref, cw_ref, cb_ref, wdown_ref, st_ref,
                      y_ref, ns_ref, carry_ref, ubuf_ref, act_ref, *, tm, seq, d_ff):
    t = pl.program_id(0)
    carry_mode = seq >= tm
    nb = 1 if carry_mode else tm // seq
    rows = tm if carry_mode else seq
    tpb = max(1, seq // tm)
    d = x_ref.shape[1]
    seg = rows + SUBLANES

    if carry_mode:
        @pl.when(t % tpb == 0)
        def _():
            carry_ref[0:2, :] = st_ref[0]

    ya = jnp.dot(oa_ref[...], woa_ref[...], preferred_element_type=F32)
    yb = jnp.dot(ob_ref[...], wob_ref[...], preferred_element_type=F32)
    m = sig_ref[:, :d].astype(F32) * ya + sig_ref[:, d:].astype(F32) * yb
    x1 = x_ref[...] + jnp.dot(m.astype(BF16), wout_ref[...], preferred_element_type=F32)
    h = (x1 * lax.rsqrt(jnp.mean(x1 * x1, axis=-1, keepdims=True) + EPS) * gffn_ref[...]).astype(BF16)

    def up(c0):
        return (jnp.dot(h, wup_ref[:, c0:c0 + FF_CHUNK], preferred_element_type=F32),
                jnp.dot(h, wup_ref[:, d_ff + c0:d_ff + c0 + FF_CHUNK], preferred_element_type=F32))

    def conv(u, c0, buf):
        cols = slice(c0, c0 + FF_CHUNK)
        w0, w1, w2 = (cw_ref[k:k + 1, cols] for k in range(3))
        outs = []
        for bi in range(nb):
            useg = u[bi * rows:(bi + 1) * rows, :]
            r0 = bi * seg + SUBLANES
            hist = carry_ref[0:2, cols] if carry_mode else st_ref[bi, :, cols]
            ubuf_ref[buf, r0 - 2:r0, :] = hist
            ubuf_ref[buf, r0:r0 + rows, :] = useg
            ns_ref[bi, :, cols] = useg[rows - 2:rows, :]
            u1 = ubuf_ref[buf, r0 - 1:r0 - 1 + rows, :]
            u2 = ubuf_ref[buf, r0 - 2:r0 - 2 + rows, :]
            outs.append(w0 * u2 + w1 * u1 + w2 * useg + cb_ref[:, cols])
        if carry_mode:
            carry_ref[0:2, cols] = u[tm - 2:tm, :]
        return outs[0] if nb == 1 else jnp.concatenate(outs, axis=0)

    chunks = list(range(0, d_ff, FF_CHUNK))
    split = chunks[len(chunks) // 2]
    nxt = up(chunks[0])
    for k, c0 in enumerate(chunks):
        ua, ub = nxt
        if k + 1 < len(chunks):
            nxt = up(chunks[k + 1])
        if c0 == split:
            y_half = x1 + jnp.dot(act_ref[:, :split], wdown_ref[:split, :], preferred_element_type=F32)
        a = conv(ua, c0, 2 * (k % 2))
        b = conv(ub, d_ff + c0, 2 * (k % 2) + 1)
        act_ref[:, c0:c0 + FF_CHUNK] = (a * jax.nn.sigmoid(a) * b).astype(BF16)
    y_ref[...] = y_half + jnp.dot(act_ref[:, split:], wdown_ref[split:, :], preferred_element_type=F32)


def _merge_ffn(x2, oa, ob, sig, state, seq, prm, tm):
    n, d = x2.shape
    d_ff = prm["w_down"].shape[0]
    assert d_ff % FF_CHUNK == 0 and seq >= 2
    carry_mode = seq >= tm
    nb = 1 if carry_mode else tm // seq
    rows = tm if carry_mode else seq
    tpb = max(1, seq // tm)
    bx = n // seq
    grid = (n // tm,)
    const = lambda a: pl.BlockSpec(a.shape, lambda t: (0,) * a.ndim, pipeline_mode=pl.Buffered(1))
    rowblk = lambda w: pl.BlockSpec((tm, w), lambda t: (t, 0))
    if carry_mode:
        st_spec = pl.BlockSpec((1, 2, 2 * d_ff), lambda t: (t // tpb, 0, 0))
    else:
        st_spec = pl.BlockSpec((nb, 2, 2 * d_ff), lambda t: (t, 0, 0))
    consts1 = [prm["w_oa"], prm["w_ob"], prm["w_out"], prm["g_ffn"], prm["w_up"], prm["conv_w"],
               prm["conv_b"], prm["w_down"]]
    return pl.pallas_call(
        functools.partial(_merge_ffn_kernel, tm=tm, seq=seq, d_ff=d_ff),
        grid=grid,
        in_specs=[rowblk(d), rowblk(W_A), rowblk(W_B), rowblk(2 * d)] + [const(a) for a in consts1] + [st_spec],
        out_specs=[rowblk(d), st_spec],
        out_shape=[jax.ShapeDtypeStruct((n, d), F32), jax.ShapeDtypeStruct((bx, 2, 2 * d_ff), F32)],
        scratch_shapes=[pltpu.VMEM((SUBLANES, 2 * d_ff), F32),
                        pltpu.VMEM((4, nb * (rows + SUBLANES), FF_CHUNK), F32),
                        pltpu.VMEM((tm, d_ff), BF16)],
        compiler_params=pltpu.CompilerParams(
            dimension_semantics=("arbitrary",), vmem_limit_bytes=VMEM_LIMIT),
        name="merge_ffn",
    )(x2, oa, ob, sig, *consts1, state)


def _tri(tm, seg):
    i = np.arange(tm)
    return jnp.asarray((i[None, :] <= i[:, None]) & (i[None, :] // seg == i[:, None] // seg), BF16)


def _piece_placer():
    p = np.zeros((LANES, HEAD_SLOTS * SLOT_W), np.float32)
    for k in range(N_CUM_PIECES):
        for h in range(HEAD_SLOTS):
            p[k * HEAD_SLOTS + h, h * SLOT_W + k] = -1.0
    return jnp.asarray(p, BF16)


def _block_diag_ones(width, blk):
    i = np.arange(width)
    return jnp.asarray(i[:, None] // blk == i[None, :] // blk, BF16)


def _rep3(a):
    pad = jnp.zeros(a.shape[:-1] + (LANES - N_CUM_PIECES * HEAD_SLOTS,), a.dtype)
    return jnp.concatenate([a] * N_CUM_PIECES + [pad], axis=-1)


def _tiles(seq):
    return dict(tm=512, tq=min(512, seq))


def _time_minor(a):
    return jnp.moveaxis(a, 1, -1)


def _time_major(a):
    return jnp.moveaxis(a, -1, 1)


def _layer(x, pos_off, cinit, past, state, prm, lam_pack, tiles):
    bx, seq, d = x.shape
    tm, tq = tiles["tm"], tiles["tq"]
    x2 = x.reshape(bx * seq, d)
    ka, va, logf, kb, vb, qaa, kaa, qba, kba, vab, vbb, sig = _in_proj(
        x2, seq, pos_off, cinit, prm, tm, key_major=past is None)
    diff_prm = (lam_pack, prm["subln_b"])
    if past is None:
        assert tq == tm
        oa = _attention("fox", qaa, kaa, vab, (), tq)
        ob = _attention("diff", qba, kba, vbb, diff_prm, tq)
    else:
        vab = vab.reshape(bx, seq, W_A)
        vbb = vbb.reshape(bx, seq, W_B)
        oa = _history_attention("fox", qaa, kaa, vab, past["kt_a"], past["vt_a"], (past["c"],))
        ob = _history_attention("diff", qba, kba, vbb, past["kt_b"], past["v_b"], diff_prm)
    y, new_state = _merge_ffn(x2, oa.reshape(bx * seq, W_A), ob.reshape(bx * seq, W_B), sig, state, seq, prm, tm)
    if seq >= LANES:
        ka, va, kb = _time_major(ka), _time_major(va), _time_major(kb.reshape(bx, H_B, 2, DH_B, seq))
    return (y.reshape(bx, seq, d), ka.reshape(bx, seq, H_A, DH_A), va.reshape(bx, seq, H_A, DH_A),
            _time_major(logf), kb.reshape(bx, seq, H_B, 2, DH_B), vb.reshape(bx, seq, H_B, DV_B), new_state)


def kernel(x_prompt, x_sample, cache_a_k, cache_a_v, cache_a_logf, cache_b_k, cache_b_v, state_ffn_conv,
           g_attn, w_in, b_f, qn_a, kn_a, qn_b, kn_b, lambda_q1, lambda_k1, lambda_q2, lambda_k2,
           subln_b, w_oa, w_ob, w_out, g_ffn, w_up, conv_w, conv_b, w_down):
    bp, tp_, d = x_prompt.shape
    bs, ts, _ = x_sample.shape
    plen = cache_a_k.shape[1]
    d_ff = w_down.shape[0]

    f0, f1 = 3 * W_A, 3 * W_A + H_A
    prm = {
        "g_attn": g_attn.reshape(1, d),
        "w_main": jnp.concatenate([w_in[:, :f0], w_in[:, f1:]], axis=1).astype(BF16),
        "w_f": _rep3(w_in[:, f0:f1]).astype(BF16),
        "b_f": _rep3(b_f.reshape(1, H_A)),
        "qn_a": jnp.tile(qn_a, H_A).reshape(1, W_A), "kn_a": jnp.tile(kn_a, H_A).reshape(1, W_A),
        "qn_b": jnp.tile(qn_b, 2 * H_B).reshape(1, W_B), "kn_b": jnp.tile(kn_b, 2 * H_B).reshape(1, W_B),
        "bd": _block_diag_ones(2 * LANES, DH_A),
        "pc": _piece_placer(),
        "subln_b": subln_b.reshape(1, DV_B),
        "w_oa": w_oa.astype(BF16), "w_ob": w_ob.astype(BF16), "w_out": w_out.astype(BF16),
        "g_ffn": g_ffn.reshape(1, d), "w_up": w_up.astype(BF16), "conv_w": conv_w,
        "conv_b": conv_b.reshape(1, 2 * d_ff), "w_down": w_down.astype(BF16),
    }
    lam_pack = jnp.stack([lambda_q1, lambda_k1, lambda_q2, lambda_k2])

    tl = _tiles(tp_)
    prm_p = dict(prm, ltri_in=_tri(tl["tm"], min(tl["tm"], tp_)))
    zeros_c = jnp.zeros((bp, 1, LANES), F32)
    zeros_state = jnp.zeros((bp, 2, 2 * d_ff), F32)
    (y_p, ka_p, va_p, lf_p, kb_p, vb_p, st_p) = _layer(
        x_prompt, 0, zeros_c, None, zeros_state, prm_p, lam_pack, tl)

    tl = _tiles(ts)
    c_past, c_tot = _cache_cumsum(_time_minor(cache_a_logf).reshape(bs * H_A, plen), min(512, plen))
    past = {
        "kt_a": _time_minor(cache_a_k), "vt_a": _time_minor(cache_a_v), "c": c_past.reshape(bs, H_A, plen),
        "kt_b": _time_minor(cache_b_k).reshape(bs, HEAD_SLOTS, DH_B, plen),
        "v_b": cache_b_v.reshape(bs, plen * H_B, DV_B),
    }
    prm_s = dict(prm, ltri_in=_tri(tl["tm"], min(tl["tm"], ts)))
    cinit_rows = jnp.repeat(_rep3(c_tot[:, 0].reshape(bs, H_A)), ts, axis=0)
    (y_s, ka_s, va_s, lf_s, kb_s, vb_s, st_s) = _layer(
        x_sample, plen, cinit_rows, past, state_ffn_conv, prm_s, lam_pack, tl)

    return (y_p, y_s, ka_p, va_p, lf_p, kb_p, vb_p, st_p, ka_s, va_s, lf_s, kb_s, vb_s, st_s)
```

```python
import functools
import math

import jax
import jax.numpy as jnp
import numpy as np
from jax import lax
from jax.experimental import pallas as pl
from jax.experimental.pallas import tpu as pltpu

F32 = jnp.float32
BF16 = jnp.bfloat16

CHUNK = 64
H_A = 8
DH_A = 64
H_B = 4
DH_B = 64
DV_B = 128
W_A = H_A * DH_A
W_B = H_B * DV_B
EPS = 1e-6
LAMBDA_INIT = 0.8 - 0.6 * math.exp(-0.3 * 0)
ALIBI_SLOPES = tuple(2.0 ** (-8.0 * (i + 1) / H_B) for i in range(H_B))

LANES = 128
SUBLANES = 8
HEAD_SLOTS = 8
SLOT_W = 64
N_CUM_PIECES = 3
N_POS_PIECES = 3
NEG_BIG = -1e30
LOG2E = math.log2(math.e)
ONES_ROWS = 16
VMEM_LIMIT = 58 * 1024 * 1024
FF_CHUNK = 256


def _lane_iota(shape):
    return lax.broadcasted_iota(jnp.int32, shape, len(shape) - 1)


def _row_iota(shape):
    return lax.broadcasted_iota(jnp.int32, shape, len(shape) - 2)


def _split3(c):
    hi = c.astype(BF16).astype(F32)
    r1 = c - hi
    lo = r1.astype(BF16).astype(F32)
    lo2 = (r1 - lo).astype(BF16).astype(F32)
    return hi, lo, lo2


def _cumsum_pieces(lf, ltri_ref, base):
    hi, lo, lo2 = _split3(lf)
    ltri = ltri_ref[...]
    c = (jnp.dot(ltri, hi.astype(BF16), preferred_element_type=F32)
         + jnp.dot(ltri, lo.astype(BF16), preferred_element_type=F32)
         + jnp.dot(ltri, lo2.astype(BF16), preferred_element_type=F32)) + base
    chi, clo, clo2 = _split3(c * LOG2E)
    lane = _lane_iota(c.shape)
    pieces = jnp.where(lane < HEAD_SLOTS, chi, jnp.where(lane < 2 * HEAD_SLOTS, clo, clo2))
    return c, pieces.astype(BF16)


def _slot(z, j):
    return z[:, SLOT_W * j:SLOT_W * (j + 1)]


def _pos_extra(pos, slope):
    hi, lo, lo2 = _split3(pos.astype(F32) * (slope * LOG2E))
    lane = _lane_iota(hi.shape)
    return jnp.where(lane == 0, hi, jnp.where(lane == 1, lo, jnp.where(lane == 2, lo2, 0.0)))


def _store_tile(ref, slot64, extra64, j, nb, rows, time_minor=False):
    tile = jnp.concatenate([slot64, extra64], axis=1)
    if time_minor:
        tile_t = tile.T.astype(BF16)
        for bi in range(nb):
            ref[bi, j, :, :] = tile_t[:, bi * rows:(bi + 1) * rows]
    else:
        tile = tile.astype(BF16)
        for bi in range(nb):
            ref[bi, j, :, :] = tile[bi * rows:(bi + 1) * rows, :]


def _emit_fox_keys(kaa_ref, slots, pieces, pc_ref, nb, rows):
    extra = jnp.dot(pieces, pc_ref[...], preferred_element_type=F32)
    for j in range(HEAD_SLOTS):
        _store_tile(kaa_ref, slots[j], _slot(extra, j), j, nb, rows)


def _emit_diff_keys(kba_ref, slots, pos, nb, rows):
    for h in range(H_B):
        extra = _pos_extra(pos, ALIBI_SLOPES[h])
        for m in range(2):
            _store_tile(kba_ref, slots[2 * h + m], extra, 2 * h + m, nb, rows)


def _emit_queries(q_ref, qn, n_ones, nb, rows, time_minor):
    lane = _lane_iota((qn.shape[0], SLOT_W))
    ones = jnp.where(lane < n_ones, 1.0, 0.0)
    for j in range(HEAD_SLOTS):
        _store_tile(q_ref, _slot(qn, j), ones, j, nb, rows, time_minor)


def _cache_cumsum_kernel(lf_ref, utri_ref, c_ref, tot_ref, carry_ref, *, tb):
    t = pl.program_id(0)

    @pl.when(t == 0)
    def _():
        carry_ref[...] = jnp.zeros_like(carry_ref)

    hi, lo, lo2 = _split3(lf_ref[...])
    u = utri_ref[...]
    c = (jnp.dot(hi.astype(BF16), u, preferred_element_type=F32)
         + jnp.dot(lo.astype(BF16), u, preferred_element_type=F32)
         + jnp.dot(lo2.astype(BF16), u, preferred_element_type=F32)) + carry_ref[...]
    c_ref[...] = c
    last = c[:, tb - 1:tb]
    carry_ref[...] = last
    tot_ref[...] = jnp.broadcast_to(last, tot_ref.shape)


def _cache_cumsum(lf_t, tb):
    r, p = lf_t.shape
    i = np.arange(tb)
    utri = jnp.asarray(i[:, None] <= i[None, :], BF16)
    return pl.pallas_call(
        functools.partial(_cache_cumsum_kernel, tb=tb),
        grid=(p // tb,),
        in_specs=[pl.BlockSpec((r, tb), lambda t: (0, t)),
                  pl.BlockSpec((tb, tb), lambda t: (0, 0), pipeline_mode=pl.Buffered(1))],
        out_specs=[pl.BlockSpec((r, tb), lambda t: (0, t)), pl.BlockSpec((r, LANES), lambda t: (0, 0))],
        out_shape=[jax.ShapeDtypeStruct((r, p), F32), jax.ShapeDtypeStruct((r, LANES), F32)],
        scratch_shapes=[pltpu.VMEM((r, 1), F32)],
        compiler_params=pltpu.CompilerParams(
            dimension_semantics=("arbitrary",), vmem_limit_bytes=VMEM_LIMIT),
        name="cache_cumsum",
    )(lf_t, utri)


def _in_proj_kernel(x_ref, g_ref, wm_ref, wf_ref, bf_ref, qna_ref, kna_ref, qnb_ref, knb_ref,
                    bd_ref, ltri_ref, pc_ref, cinit_ref,
                    ka_ref, va_ref, logf_ref, kb_ref, vb_ref,
                    qaa_ref, kaa_ref, qba_ref, kba_ref, vab_ref, vbb_ref, sig_ref, carry_ref,
                    *, tm, seq, pos_off, key_major):
    t = pl.program_id(0)
    carry_mode = seq >= tm
    nb = 1 if carry_mode else tm // seq
    rows = tm if carry_mode else seq
    tpb = max(1, seq // tm)

    x = x_ref[...]
    h = (x * lax.rsqrt(jnp.mean(x * x, axis=-1, keepdims=True) + EPS) * g_ref[...]).astype(BF16)

    def proj(lo, hi):
        return jnp.dot(h, wm_ref[:, lo:hi], preferred_element_type=F32)

    def head_norm(z, w_ref):
        z2 = (z * z).astype(BF16)
        bw = bd_ref.shape[0]
        ss = jnp.concatenate([jnp.dot(z2[:, k:k + bw], bd_ref[...], preferred_element_type=F32)
                              for k in range(0, z.shape[1], bw)], axis=1)
        return z * lax.rsqrt(ss * (1.0 / DH_A) + EPS) * w_ref[...]

    def store_heads(ref, z, heads):
        w = z.shape[1] // heads
        for j in range(heads):
            ref[pl.ds(j, tm, stride=heads), :] = z[:, w * j:w * (j + 1)]

    def store_time_minor(ref, z):
        zts = [z[bi * rows:(bi + 1) * rows, :].T for bi in range(nb)]
        for bi, zt in enumerate(zts):
            ref[bi] = zt.reshape(z.shape[1] // SLOT_W, SLOT_W, rows)
        return zts

    def store_slots(ref, z):
        if seq >= LANES:
            return store_time_minor(ref, z)
        store_heads(ref, z, z.shape[1] // SLOT_W)
        return None

    def store_value_slabs(ref, vt):
        ones = jnp.ones((ONES_ROWS, tm), BF16)
        for g in range(W_A // LANES):
            ref[0, 0, g, 0:LANES, :] = vt[LANES * g:LANES * (g + 1), :].astype(BF16)
            ref[0, 0, g, LANES:LANES + ONES_ROWS, :] = ones

    zf = jnp.dot(h, wf_ref[...], preferred_element_type=F32) + bf_ref[...]
    lf = jnp.minimum(zf, 0.0) - jnp.log1p(jnp.exp(-jnp.abs(zf)))
    lane = _lane_iota(lf.shape)
    lf = jnp.where(lane < N_CUM_PIECES * HEAD_SLOTS, lf, 0.0)
    for bi in range(nb):
        logf_ref[bi] = lf[bi * rows:(bi + 1) * rows, :].T[:H_A, :]
    if carry_mode:
        @pl.when(t % tpb == 0)
        def _():
            carry_ref[...] = cinit_ref[0]
        base = carry_ref[...]
    else:
        base = cinit_ref[...]
    c, pieces = _cumsum_pieces(lf, ltri_ref, base)
    if carry_mode:
        carry_ref[...] = c[tm - 1:tm, :]

    row = t * tm + _row_iota((tm, SLOT_W))
    pos = (row & (seq - 1)) + pos_off

    qn = head_norm(proj(0, W_A), qna_ref) * (DH_A ** -0.5 * LOG2E)
    _emit_queries(qaa_ref, qn, N_CUM_PIECES, nb, rows, key_major)
    kn = head_norm(proj(W_A, 2 * W_A), kna_ref)
    store_slots(ka_ref, kn)
    _emit_fox_keys(kaa_ref, [_slot(kn, j) for j in range(HEAD_SLOTS)], pieces, pc_ref, nb, rows)
    v = proj(2 * W_A, 3 * W_A)
    vts = store_slots(va_ref, v)
    if key_major:
        store_value_slabs(vab_ref, vts[0])
    else:
        vab_ref[...] = v.astype(BF16)

    o = 3 * W_A
    qn = head_norm(proj(o, o + W_B), qnb_ref) * (DH_B ** -0.5 * LOG2E)
    _emit_queries(qba_ref, qn, N_POS_PIECES, nb, rows, key_major)
    kn = head_norm(proj(o + W_B, o + 2 * W_B), knb_ref)
    store_slots(kb_ref, kn)
    _emit_diff_keys(kba_ref, [_slot(kn, j) for j in range(HEAD_SLOTS)], pos, nb, rows)
    v = proj(o + 2 * W_B, o + 3 * W_B)
    store_heads(vb_ref, v, H_B)
    if key_major:
        store_value_slabs(vbb_ref, v.T)
    else:
        vbb_ref[...] = v.astype(BF16)

    o = 3 * W_A + 3 * W_B
    d = (wm_ref.shape[1] - o) // 2
    for k in range(2):
        sig_ref[:, k * d:(k + 1) * d] = jax.nn.sigmoid(proj(o + k * d, o + (k + 1) * d)).astype(BF16)


def _in_proj(x2, seq, pos_off, cinit, prm, tm, key_major):
    n, d = x2.shape
    carry_mode = seq >= tm
    nb = 1 if carry_mode else tm // seq
    tpb = max(1, seq // tm)
    bx = n // seq
    assert n % tm == 0 and seq & (seq - 1) == 0
    grid = (n // tm,)
    const = lambda a: pl.BlockSpec(a.shape, lambda t: (0,) * a.ndim, pipeline_mode=pl.Buffered(1))
    rowblk = lambda w, mult=1: pl.BlockSpec((tm * mult, w), lambda t: (t, 0))
    if carry_mode:
        cinit_spec = pl.BlockSpec((1, 1, LANES), lambda t: (t // tpb, 0, 0))
        aug_spec = pl.BlockSpec((1, HEAD_SLOTS, tm, LANES), lambda t: (t // tpb, 0, t % tpb, 0))
    else:
        cinit_spec = rowblk(LANES)
        aug_spec = pl.BlockSpec((nb, HEAD_SLOTS, seq, LANES), lambda t: (t, 0, 0, 0))
    aug_shape = jax.ShapeDtypeStruct((bx, HEAD_SLOTS, seq, LANES), BF16)
    if carry_mode:
        tmin_spec = pl.BlockSpec((1, HEAD_SLOTS, SLOT_W, tm), lambda t: (t // tpb, 0, 0, t % tpb))
        lf_spec = pl.BlockSpec((1, H_A, tm), lambda t: (t // tpb, 0, t % tpb))
    else:
        tmin_spec = pl.BlockSpec((nb, HEAD_SLOTS, SLOT_W, seq), lambda t: (t, 0, 0, 0))
        lf_spec = pl.BlockSpec((nb, H_A, seq), lambda t: (t, 0, 0))
    tmin_shape = jax.ShapeDtypeStruct((bx, HEAD_SLOTS, SLOT_W, seq), F32)
    if seq < LANES:
        tmin_spec = rowblk(SLOT_W, HEAD_SLOTS)
        tmin_shape = jax.ShapeDtypeStruct((n * HEAD_SLOTS, SLOT_W), F32)
    if key_major:
        assert carry_mode
        q_spec = pl.BlockSpec((1, HEAD_SLOTS, LANES, tm), lambda t: (t // tpb, 0, 0, t % tpb))
        q_shape = jax.ShapeDtypeStruct((bx, HEAD_SLOTS, LANES, seq), BF16)
        slab = (W_A // LANES, LANES + ONES_ROWS, tm)
        v_spec = pl.BlockSpec((1, 1) + slab, lambda t: (t // tpb, t % tpb, 0, 0, 0))
        v_shape = jax.ShapeDtypeStruct((bx, tpb) + slab, BF16)
    else:
        q_spec, q_shape = aug_spec, aug_shape
        v_spec = rowblk(W_A)
        v_shape = jax.ShapeDtypeStruct((n, W_A), BF16)
    consts = [prm["g_attn"], prm["w_main"], prm["w_f"], prm["b_f"], prm["qn_a"], prm["kn_a"],
              prm["qn_b"], prm["kn_b"], prm["bd"], prm["ltri_in"], prm["pc"]]
    return pl.pallas_call(
        functools.partial(_in_proj_kernel, tm=tm, seq=seq, pos_off=pos_off, key_major=key_major),
        grid=grid,
        in_specs=[rowblk(d)] + [const(a) for a in consts] + [cinit_spec],
        out_specs=[tmin_spec, tmin_spec, lf_spec, tmin_spec, rowblk(DV_B, H_B),
                   q_spec, aug_spec, q_spec, aug_spec, v_spec, v_spec, rowblk(2 * d)],
        out_shape=[
            tmin_shape, tmin_shape, jax.ShapeDtypeStruct((bx, H_A, seq), F32),
            tmin_shape, jax.ShapeDtypeStruct((n * H_B, DV_B), F32),
            q_shape, aug_shape, q_shape, aug_shape, v_shape, v_shape,
            jax.ShapeDtypeStruct((n, 2 * d), BF16),
        ],
        scratch_shapes=[pltpu.VMEM((1, LANES), F32)],
        compiler_params=pltpu.CompilerParams(
            dimension_semantics=("arbitrary",), vmem_limit_bytes=VMEM_LIMIT),
        name="in_proj",
    )(x2, *consts, cinit)


_CONTRACT_LAST = (((1,), (1,)), ((), ()))


def _attend_pair(q_ref, k_ref, v_ref, diag_bias, emit, *, tq):
    half = tq // 2
    bias_top, bias_bot = diag_bias

    def scores(c, n, j):
        qt = q_ref[0, c, :, n * tq:(n + 1) * tq]
        k0 = j * tq
        if j < n:
            return [(jnp.dot(k_ref[0, c, k0 + kk:k0 + kk + half, :], qt, preferred_element_type=F32), kk, 0)
                    for kk in (0, half)]
        top = jnp.dot(k_ref[0, c, k0:k0 + half, :], qt, preferred_element_type=F32) + bias_top
        bot = jnp.dot(k_ref[0, c, k0 + half:k0 + tq, :], qt[:, half:], preferred_element_type=F32) + bias_bot
        return [(top, 0, 0), (bot, half, half)]

    def absorb(m, acc, piece, vt):
        s, key0, q0 = piece
        m_old, acc_old = m[:, q0:], acc[:, q0:]
        m_new = jnp.maximum(m_old, jnp.max(s, axis=0, keepdims=True))
        p = jnp.exp2(s - m_new).astype(BF16)
        pv = jnp.dot(vt[:, key0:key0 + s.shape[0]], p, preferred_element_type=F32)
        acc_new = jnp.exp2(m_old - m_new) * acc_old + pv
        if q0:
            m_new = jnp.concatenate([m[:, :q0], m_new], axis=1)
            acc_new = jnp.concatenate([acc[:, :q0], acc_new], axis=1)
        return m_new, acc_new

    work = [(n, j) for n in range(v_ref.shape[1]) for j in range(n + 1)]
    m, acc = [None, None], [None, None]

    def first(t):
        return [scores(c, *work[t]) for c in range(2)]

    def second(t, s):
        n, j = work[t]
        vt = v_ref[0, j, 0]
        for c in range(2):
            if j == 0:
                m[c] = jnp.full((1, tq), NEG_BIG, F32)
                acc[c] = jnp.zeros((LANES + ONES_ROWS, tq), F32)
            for piece in s[c]:
                m[c], acc[c] = absorb(m[c], acc[c], piece, vt)
        if j == n:
            emit(n, [(acc[c][:LANES] / acc[c][LANES:LANES + 1]).T for c in range(2)])

    _pipelined(len(work), first, second, lookahead=1)


def _history_scores(q, kt_past, bias_row, k_cur, diag_bias):
    s_past = jnp.dot(q[:, :SLOT_W], kt_past, preferred_element_type=F32) + bias_row
    s_cur = lax.dot_general(q, k_cur, _CONTRACT_LAST, preferred_element_type=F32) + diag_bias
    return s_past, s_cur


def _history_output(s_past, s_cur, v_past, v_cur, v_time_minor):
    m = jnp.maximum(jnp.max(s_past, axis=-1, keepdims=True), jnp.max(s_cur, axis=-1, keepdims=True))
    p_past = jnp.exp2(s_past - m)
    p_cur = jnp.exp2(s_cur - m)
    l = jnp.sum(p_past, axis=-1, keepdims=True) + jnp.sum(p_cur, axis=-1, keepdims=True)
    if v_time_minor:
        o = lax.dot_general(p_past.astype(BF16), v_past, _CONTRACT_LAST, preferred_element_type=F32)
    else:
        o = jnp.dot(p_past.astype(BF16), v_past, preferred_element_type=F32)
    return (o + jnp.dot(p_cur.astype(BF16), v_cur, preferred_element_type=F32)) / l


def _pipelined(n, first, second, lookahead=2):
    pending = [first(h) for h in range(min(lookahead, n))]
    outs = []
    for h in range(n):
        if h + lookahead < n:
            pending.append(first(h + lookahead))
        outs.append(second(h, pending[h]))
    return outs


def _diag_indices(nk, nq, k0, q0, key_major):
    shape = (nk, nq) if key_major else (nq, nk)
    rowi, coli = _row_iota(shape), _lane_iota(shape)
    return (coli + q0, rowi + k0) if key_major else (rowi + q0, coli + k0)


def _diag_pieces(tq):
    half = tq // 2
    return _diag_indices(half, tq, 0, 0, True), _diag_indices(half, half, half, half, True)


def _causal_bias(qi, ki):
    return jnp.where(ki <= qi, 0.0, NEG_BIG)


def _diff_diag_bias(qi, ki, slope):
    ahead = jnp.maximum(ki - qi, 0).astype(F32)
    visible = (ki // CHUNK) <= (qi // CHUNK)
    return jnp.where(visible, (-2.0 * LOG2E) * slope * ahead, NEG_BIG)


def _diff_finish(o1, o2, lam_ref, sub_ref):
    lq1, lk1, lq2, lk2 = (lam_ref[k:k + 1, :] for k in range(4))
    lam = (jnp.exp(jnp.sum(lq1 * lk1, axis=-1, keepdims=True))
           - jnp.exp(jnp.sum(lq2 * lk2, axis=-1, keepdims=True)) + LAMBDA_INIT)
    o = o1 - lam * o2
    o = o * lax.rsqrt(jnp.mean(o * o, axis=-1, keepdims=True) + EPS) * sub_ref[...] * (1.0 - LAMBDA_INIT)
    return o.astype(BF16)


def _fox_kernel(q_ref, k_ref, v_ref, o_ref, *, tq):
    bias = tuple(_causal_bias(qi, ki) for qi, ki in _diag_pieces(tq))
    lane = _lane_iota((tq, LANES))

    def emit(n, outs):
        o_ref[0, n * tq:(n + 1) * tq, :] = jnp.where(lane < SLOT_W, outs[0], outs[1]).astype(BF16)

    _attend_pair(q_ref, k_ref, v_ref, bias, emit, tq=tq)


def _fox_history_kernel(q_ref, kc_ref, vc_ref, kp_ref, vp_ref, c_ref, o_ref, *, ts):
    causal = _causal_bias(*_diag_indices(ts, ts, 0, 0, False))
    lane = _lane_iota((ts, LANES))

    def scores(h):
        return _history_scores(q_ref[0, h], kp_ref[0, h].astype(BF16), c_ref[0, h:h + 1, :] * -LOG2E,
                               kc_ref[0, h], causal)

    pair_values = {}

    def output(h, s):
        g = h // 2
        if g not in pair_values:
            pair_values[g] = jnp.concatenate([vp_ref[0, 2 * g], vp_ref[0, 2 * g + 1]], axis=0).astype(BF16)
        return _history_output(*s, pair_values[g], vc_ref[0, :, LANES * g:LANES * (g + 1)], True)

    outs = _pipelined(H_A, scores, output)
    for g in range(H_A // 2):
        o_ref[0, :, LANES * g:LANES * (g + 1)] = jnp.where(
            lane < SLOT_W, outs[2 * g], outs[2 * g + 1]).astype(BF16)


def _diff_kernel(q_ref, k_ref, v_ref, lam_ref, sub_ref, o_ref, *, tq):
    hd = pl.program_id(1)
    slope = jnp.float32(ALIBI_SLOPES[0])
    for k in range(1, H_B):
        slope = jnp.where(hd == k, jnp.float32(ALIBI_SLOPES[k]), slope)
    bias = tuple(_diff_diag_bias(qi, ki, slope) for qi, ki in _diag_pieces(tq))

    def emit(n, outs):
        o_ref[0, n * tq:(n + 1) * tq, :] = _diff_finish(outs[0], outs[1], lam_ref, sub_ref)

    _attend_pair(q_ref, k_ref, v_ref, bias, emit, tq=tq)


def _diff_history_kernel(q_ref, kc_ref, vc_ref, kp_ref, vp_ref, lam_ref, sub_ref, o_ref, *, ts):
    plen = kp_ref.shape[3]
    pos = _lane_iota((1, plen)).astype(F32)
    qi, ki = _diag_indices(ts, ts, 0, 0, False)

    def scores(j):
        slope = ALIBI_SLOPES[j // 2]
        return _history_scores(q_ref[0, j], kp_ref[0, j].astype(BF16), pos * (slope * LOG2E),
                               kc_ref[0, j], _diff_diag_bias(qi, ki, slope))

    head_values = {}

    def output(j, s):
        h = j // 2
        if h not in head_values:
            head_values[h] = vp_ref[0, pl.ds(h, plen, stride=H_B), :].astype(BF16)
        return _history_output(*s, head_values[h], vc_ref[0, :, DV_B * h:DV_B * (h + 1)], False)

    outs = _pipelined(HEAD_SLOTS, scores, output)
    for h in range(H_B):
        o_ref[0, :, DV_B * h:DV_B * (h + 1)] = _diff_finish(outs[2 * h], outs[2 * h + 1], lam_ref, sub_ref)


def _attention(kind, q_aug, k_aug, v_cur, extras, tq):
    bx, _, seq, _ = k_aug.shape
    groups = HEAD_SLOTS // 2
    nblk = seq // tq
    slab_rows = LANES + ONES_ROWS
    assert v_cur.shape == (bx, nblk, groups, slab_rows, tq)
    grid = (bx, groups)
    in_specs = [
        pl.BlockSpec((1, 2, LANES, seq), lambda b, g: (b, g, 0, 0)),
        pl.BlockSpec((1, 2, seq, LANES), lambda b, g: (b, g, 0, 0)),
        pl.BlockSpec((1, nblk, 1, slab_rows, tq), lambda b, g: (b, 0, g, 0, 0)),
    ] + [pl.BlockSpec(a.shape, lambda b, g: (0,) * a.ndim) for a in extras]
    body = _fox_kernel if kind == "fox" else _diff_kernel
    return pl.pallas_call(
        functools.partial(body, tq=tq),
        grid=grid,
        in_specs=in_specs,
        out_specs=pl.BlockSpec((1, seq, LANES), lambda b, g: (b, 0, g)),
        out_shape=jax.ShapeDtypeStruct((bx, seq, groups * LANES), BF16),
        compiler_params=pltpu.CompilerParams(
            dimension_semantics=("arbitrary", "arbitrary"), vmem_limit_bytes=VMEM_LIMIT),
        name=kind + "_attn",
    )(q_aug, k_aug, v_cur, *extras)


def _history_attention(kind, q_aug, k_aug, v_cur, kt_past, v_past, extras):
    bx, _, ts, _ = q_aug.shape
    whole = lambda a: pl.BlockSpec((1,) + a.shape[1:], lambda b: (b,) + (0,) * (a.ndim - 1))
    shared = lambda a: pl.BlockSpec(a.shape, lambda b: (0,) * a.ndim)
    args = [q_aug, k_aug, v_cur, kt_past, v_past]
    in_specs = [whole(a) for a in args] + [whole(a) if kind == "fox" else shared(a) for a in extras]
    body = _fox_history_kernel if kind == "fox" else _diff_history_kernel
    return pl.pallas_call(
        functools.partial(body, ts=ts),
        grid=(bx,),
        in_specs=in_specs,
        out_specs=pl.BlockSpec((1, ts, v_cur.shape[2]), lambda b: (b, 0, 0)),
        out_shape=jax.ShapeDtypeStruct((bx, ts, v_cur.shape[2]), BF16),
        compiler_params=pltpu.CompilerParams(
            dimension_semantics=("arbitrary",), vmem_limit_bytes=VMEM_LIMIT),
        name=kind + "_history_attn",
    )(*args, *extras)


def _merge_ffn_kernel(x_ref, oa_ref, ob_ref, sig_ref, woa_ref, wob_ref, wout_ref, gffn_ref,
                      wup_ref, cw_ref, cb_ref, wdown_ref, st_ref,
                      y_ref, ns_ref, carry_ref, ubuf_ref, act_ref, *, tm, seq, d_ff):
    t = pl.program_id(0)
    carry_mode = seq >= tm
    nb = 1 if carry_mode else tm // seq
    rows = tm if carry_mode else seq
    tpb = max(1, seq // tm)
    d = x_ref.shape[1]
    seg = rows + SUBLANES

    if carry_mode:
        @pl.when(t % tpb == 0)
        def _():
            carry_ref[0:2, :] = st_ref[0]

    ya = jnp.dot(oa_ref[...], woa_ref[...], preferred_element_type=F32)
    yb = jnp.dot(ob_ref[...], wob_ref[...], preferred_element_type=F32)
    m = sig_ref[:, :d].astype(F32) * ya + sig_ref[:, d:].astype(F32) * yb
    x1 = x_ref[...] + jnp.dot(m.astype(BF16), wout_ref[...], preferred_element_type=F32)
    h = (x1 * lax.rsqrt(jnp.mean(x1 * x1, axis=-1, keepdims=True) + EPS) * gffn_ref[...]).astype(BF16)

    def up(c0):
        return (jnp.dot(h, wup_ref[:, c0:c0 + FF_CHUNK], preferred_element_type=F32),
                jnp.dot(h, wup_ref[:, d_ff + c0:d_ff + c0 + FF_CHUNK], preferred_element_type=F32))

    def conv(u, c0, buf):
        cols = slice(c0, c0 + FF_CHUNK)
        w0, w1, w2 = (cw_ref[k:k + 1, cols] for k in range(3))
        outs = []
        for bi in range(nb):
            useg = u[bi * rows:(bi + 1) * rows, :]
            r0 = bi * seg + SUBLANES
            hist = carry_ref[0:2, cols] if carry_mode else st_ref[bi, :, cols]
            ubuf_ref[buf, r0 - 2:r0, :] = hist
            ubuf_ref[buf, r0:r0 + rows, :] = useg
            ns_ref[bi, :, cols] = useg[rows - 2:rows, :]
            u1 = ubuf_ref[buf, r0 - 1:r0 - 1 + rows, :]
            u2 = ubuf_ref[buf, r0 - 2:r0 - 2 + rows, :]
            outs.append(w0 * u2 + w1 * u1 + w2 * useg + cb_ref[:, cols])
        if carry_mode:
            carry_ref[0:2, cols] = u[tm - 2:tm, :]
        return outs[0] if nb == 1 else jnp.concatenate(outs, axis=0)

    chunks = list(range(0, d_ff, FF_CHUNK))
    split = chunks[len(chunks) // 2]
    nxt = up(chunks[0])
    for k, c0 in enumerate(chunks):
        ua, ub = nxt
        if k + 1 < len(chunks):
            nxt = up(chunks[k + 1])
        if c0 == split:
            y_half = x1 + jnp.dot(act_ref[:, :split], wdown_ref[:split, :], preferred_element_type=F32)
        a = conv(ua, c0, 2 * (k % 2))
        b = conv(ub, d_ff + c0, 2 * (k % 2) + 1)
        act_ref[:, c0:c0 + FF_CHUNK] = (a * jax.nn.sigmoid(a) * b).astype(BF16)
    y_ref[...] = y_half + jnp.dot(act_ref[:, split:], wdown_ref[split:, :], preferred_element_type=F32)


def _merge_ffn(x2, oa, ob, sig, state, seq, prm, tm):
    n, d = x2.shape
    d_ff = prm["w_down"].shape[0]
    assert d_ff % FF_CHUNK == 0 and seq >= 2
    carry_mode = seq >= tm
    nb = 1 if carry_mode else tm // seq
    rows = tm if carry_mode else seq
    tpb = max(1, seq // tm)
    bx = n // seq
    grid = (n // tm,)
    const = lambda a: pl.BlockSpec(a.shape, lambda t: (0,) * a.ndim, pipeline_mode=pl.Buffered(1))
    rowblk = lambda w: pl.BlockSpec((tm, w), lambda t: (t, 0))
    if carry_mode:
        st_spec = pl.BlockSpec((1, 2, 2 * d_ff), lambda t: (t // tpb, 0, 0))
    else:
        st_spec = pl.BlockSpec((nb, 2, 2 * d_ff), lambda t: (t, 0, 0))
    consts1 = [prm["w_oa"], prm["w_ob"], prm["w_out"], prm["g_ffn"], prm["w_up"], prm["conv_w"],
               prm["conv_b"], prm["w_down"]]
    return pl.pallas_call(
        functools.partial(_merge_ffn_kernel, tm=tm, seq=seq, d_ff=d_ff),
        grid=grid,
        in_specs=[rowblk(d), rowblk(W_A), rowblk(W_B), rowblk(2 * d)] + [const(a) for a in consts1] + [st_spec],
        out_specs=[rowblk(d), st_spec],
        out_shape=[jax.ShapeDtypeStruct((n, d), F32), jax.ShapeDtypeStruct((bx, 2, 2 * d_ff), F32)],
        scratch_shapes=[pltpu.VMEM((SUBLANES, 2 * d_ff), F32),
                        pltpu.VMEM((4, nb * (rows + SUBLANES), FF_CHUNK), F32),
                        pltpu.VMEM((tm, d_ff), BF16)],
        compiler_params=pltpu.CompilerParams(
            dimension_semantics=("arbitrary",), vmem_limit_bytes=VMEM_LIMIT),
        name="merge_ffn",
    )(x2, oa, ob, sig, *consts1, state)


def _tri(tm, seg):
    i = np.arange(tm)
    return jnp.asarray((i[None, :] <= i[:, None]) & (i[None, :] // seg == i[:, None] // seg), BF16)


def _piece_placer():
    p = np.zeros((LANES, HEAD_SLOTS * SLOT_W), np.float32)
    for k in range(N_CUM_PIECES):
        for h in range(HEAD_SLOTS):
            p[k * HEAD_SLOTS + h, h * SLOT_W + k] = -1.0
    return jnp.asarray(p, BF16)


def _block_diag_ones(width, blk):
    i = np.arange(width)
    return jnp.asarray(i[:, None] // blk == i[None, :] // blk, BF16)


def _rep3(a):
    pad = jnp.zeros(a.shape[:-1] + (LANES - N_CUM_PIECES * HEAD_SLOTS,), a.dtype)
    return jnp.concatenate([a] * N_CUM_PIECES + [pad], axis=-1)


def _tiles(seq):
    return dict(tm=512, tq=min(512, seq))


def _time_minor(a):
    return jnp.moveaxis(a, 1, -1)


def _time_major(a):
    return jnp.moveaxis(a, -1, 1)


def _layer(x, pos_off, cinit, past, state, prm, lam_pack, tiles):
    bx, seq, d = x.shape
    tm, tq = tiles["tm"], tiles["tq"]
    x2 = x.reshape(bx * seq, d)
    ka, va, logf, kb, vb, qaa, kaa, qba, kba, vab, vbb, sig = _in_proj(
        x2, seq, pos_off, cinit, prm, tm, key_major=past is None)
    diff_prm = (lam_pack, prm["subln_b"])
    if past is None:
        assert tq == tm
        oa = _attention("fox", qaa, kaa, vab, (), tq)
        ob = _attention("diff", qba, kba, vbb, diff_prm, tq)
    else:
        vab = vab.reshape(bx, seq, W_A)
        vbb = vbb.reshape(bx, seq, W_B)
        oa = _history_attention("fox", qaa, kaa, vab, past["kt_a"], past["vt_a"], (past["c"],))
        ob = _history_attention("diff", qba, kba, vbb, past["kt_b"], past["v_b"], diff_prm)
    y, new_state = _merge_ffn(x2, oa.reshape(bx * seq, W_A), ob.reshape(bx * seq, W_B), sig, state, seq, prm, tm)
    if seq >= LANES:
        ka, va, kb = _time_major(ka), _time_major(va), _time_major(kb.reshape(bx, H_B, 2, DH_B, seq))
    return (y.reshape(bx, seq, d), ka.reshape(bx, seq, H_A, DH_A), va.reshape(bx, seq, H_A, DH_A),
            _time_major(logf), kb.reshape(bx, seq, H_B, 2, DH_B), vb.reshape(bx, seq, H_B, DV_B), new_state)


def kernel(x_prompt, x_sample, cache_a_k, cache_a_v, cache_a_logf, cache_b_k, cache_b_v, state_ffn_conv,
           g_attn, w_in, b_f, qn_a, kn_a, qn_b, kn_b, lambda_q1, lambda_k1, lambda_q2, lambda_k2,
           subln_b, w_oa, w_ob, w_out, g_ffn, w_up, conv_w, conv_b, w_down):
    bp, tp_, d = x_prompt.shape
    bs, ts, _ = x_sample.shape
    plen = cache_a_k.shape[1]
    d_ff = w_down.shape[0]

    f0, f1 = 3 * W_A, 3 * W_A + H_A
    prm = {
        "g_attn": g_attn.reshape(1, d),
        "w_main": jnp.concatenate([w_in[:, :f0], w_in[:, f1:]], axis=1).astype(BF16),
        "w_f": _rep3(w_in[:, f0:f1]).astype(BF16),
        "b_f": _rep3(b_f.reshape(1, H_A)),
        "qn_a": jnp.tile(qn_a, H_A).reshape(1, W_A), "kn_a": jnp.tile(kn_a, H_A).reshape(1, W_A),
        "qn_b": jnp.tile(qn_b, 2 * H_B).reshape(1, W_B), "kn_b": jnp.tile(kn_b, 2 * H_B).reshape(1, W_B),
        "bd": _block_diag_ones(2 * LANES, DH_A),
        "pc": _piece_placer(),
        "subln_b": subln_b.reshape(1, DV_B),
        "w_oa": w_oa.astype(BF16), "w_ob": w_ob.astype(BF16), "w_out": w_out.astype(BF16),
        "g_ffn": g_ffn.reshape(1, d), "w_up": w_up.astype(BF16), "conv_w": conv_w,
        "conv_b": conv_b.reshape(1, 2 * d_ff), "w_down": w_down.astype(BF16),
    }
    lam_pack = jnp.stack([lambda_q1, lambda_k1, lambda_q2, lambda_k2])

    tl = _tiles(tp_)
    prm_p = dict(prm, ltri_in=_tri(tl["tm"], min(tl["tm"], tp_)))
    zeros_c = jnp.zeros((bp, 1, LANES), F32)
    zeros_state = jnp.zeros((bp, 2, 2 * d_ff), F32)
    (y_p, ka_p, va_p, lf_p, kb_p, vb_p, st_p) = _layer(
        x_prompt, 0, zeros_c, None, zeros_state, prm_p, lam_pack, tl)

    tl = _tiles(ts)
    c_past, c_tot = _cache_cumsum(_time_minor(cache_a_logf).reshape(bs * H_A, plen), min(512, plen))
    past = {
        "kt_a": _time_minor(cache_a_k), "vt_a": _time_minor(cache_a_v), "c": c_past.reshape(bs, H_A, plen),
        "kt_b": _time_minor(cache_b_k).reshape(bs, HEAD_SLOTS, DH_B, plen),
        "v_b": cache_b_v.reshape(bs, plen * H_B, DV_B),
    }
    prm_s = dict(prm, ltri_in=_tri(tl["tm"], min(tl["tm"], ts)))
    cinit_rows = jnp.repeat(_rep3(c_tot[:, 0].reshape(bs, H_A)), ts, axis=0)
    (y_s, ka_s, va_s, lf_s, kb_s, vb_s, st_s) = _layer(
        x_sample, plen, cinit_rows, past, state_ffn_conv, prm_s, lam_pack, tl)

    return (y_p, y_s, ka_p, va_p, lf_p, kb_p, vb_p, st_p, ka_s, va_s, lf_s, kb_s, vb_s, st_s)
```

```python
import functools
import math

import jax
import jax.numpy as jnp
import numpy as np
from jax import lax
from jax.experimental import pallas as pl
from jax.experimental.pallas import tpu as pltpu

F32 = jnp.float32
BF16 = jnp.bfloat16

CHUNK = 64
H_A = 8
DH_A = 64
H_B = 4
DH_B = 64
DV_B = 128
W_A = H_A * DH_A
W_B = H_B * DV_B
EPS = 1e-6
LAMBDA_INIT = 0.8 - 0.6 * math.exp(-0.3 * 0)
ALIBI_SLOPES = tuple(2.0 ** (-8.0 * (i + 1) / H_B) for i in range(H_B))

LANES = 128
SUBLANES = 8
HEAD_SLOTS = 8
SLOT_W = 64
N_CUM_PIECES = 3
N_POS_PIECES = 3
NEG_BIG = -1e30
LOG2E = math.log2(math.e)
ONES_ROWS = 16
VMEM_LIMIT = 58 * 1024 * 1024
FF_CHUNK = 256
HISTORY_STREAMS = 2


def _lane_iota(shape):
    return lax.broadcasted_iota(jnp.int32, shape, len(shape) - 1)


def _row_iota(shape):
    return lax.broadcasted_iota(jnp.int32, shape, len(shape) - 2)


def _split3(c):
    hi = c.astype(BF16).astype(F32)
    r1 = c - hi
    lo = r1.astype(BF16).astype(F32)
    lo2 = (r1 - lo).astype(BF16).astype(F32)
    return hi, lo, lo2


def _cumsum_pieces(lf, ltri_ref, base):
    hi, lo, lo2 = _split3(lf)
    ltri = ltri_ref[...]
    c = (jnp.dot(ltri, hi.astype(BF16), preferred_element_type=F32)
         + jnp.dot(ltri, lo.astype(BF16), preferred_element_type=F32)
         + jnp.dot(ltri, lo2.astype(BF16), preferred_element_type=F32)) + base
    chi, clo, clo2 = _split3(c * LOG2E)
    lane = _lane_iota(c.shape)
    pieces = jnp.where(lane < HEAD_SLOTS, chi, jnp.where(lane < 2 * HEAD_SLOTS, clo, clo2))
    return c, pieces.astype(BF16)


def _slot(z, j):
    return z[:, SLOT_W * j:SLOT_W * (j + 1)]


def _pos_extra(pos, slope):
    hi, lo, lo2 = _split3(pos.astype(F32) * (slope * LOG2E))
    lane = _lane_iota(hi.shape)
    return jnp.where(lane == 0, hi, jnp.where(lane == 1, lo, jnp.where(lane == 2, lo2, 0.0)))


def _store_tile(ref, slot64, extra64, j, nb, rows, time_minor=False):
    tile = jnp.concatenate([slot64, extra64], axis=1)
    if time_minor:
        tile_t = tile.T.astype(BF16)
        for bi in range(nb):
            ref[bi, j, :, :] = tile_t[:, bi * rows:(bi + 1) * rows]
    else:
        tile = tile.astype(BF16)
        for bi in range(nb):
            ref[bi, j, :, :] = tile[bi * rows:(bi + 1) * rows, :]


def _emit_fox_keys(kaa_ref, slots, pieces, pc_ref, nb, rows):
    extra = jnp.dot(pieces, pc_ref[...], preferred_element_type=F32)
    for j in range(HEAD_SLOTS):
        _store_tile(kaa_ref, slots[j], _slot(extra, j), j, nb, rows)


def _emit_diff_keys(kba_ref, slots, pos, nb, rows):
    for h in range(H_B):
        extra = _pos_extra(pos, ALIBI_SLOPES[h])
        for m in range(2):
            _store_tile(kba_ref, slots[2 * h + m], extra, 2 * h + m, nb, rows)


def _emit_queries(q_ref, qn, n_ones, nb, rows, time_minor):
    lane = _lane_iota((qn.shape[0], SLOT_W))
    ones = jnp.where(lane < n_ones, 1.0, 0.0)
    for j in range(HEAD_SLOTS):
        _store_tile(q_ref, _slot(qn, j), ones, j, nb, rows, time_minor)


def _cache_cumsum_kernel(lf_ref, utri_ref, c_ref, tot_ref, carry_ref, *, tb):
    t = pl.program_id(0)

    @pl.when(t == 0)
    def _():
        carry_ref[...] = jnp.zeros_like(carry_ref)

    hi, lo, lo2 = _split3(lf_ref[...])
    u = utri_ref[...]
    c = (jnp.dot(hi.astype(BF16), u, preferred_element_type=F32)
         + jnp.dot(lo.astype(BF16), u, preferred_element_type=F32)
         + jnp.dot(lo2.astype(BF16), u, preferred_element_type=F32)) + carry_ref[...]
    c_ref[...] = c
    last = c[:, tb - 1:tb]
    carry_ref[...] = last
    tot_ref[...] = jnp.broadcast_to(last, tot_ref.shape)


def _cache_cumsum(lf_t, tb):
    r, p = lf_t.shape
    i = np.arange(tb)
    utri = jnp.asarray(i[:, None] <= i[None, :], BF16)
    return pl.pallas_call(
        functools.partial(_cache_cumsum_kernel, tb=tb),
        grid=(p // tb,),
        in_specs=[pl.BlockSpec((r, tb), lambda t: (0, t)),
                  pl.BlockSpec((tb, tb), lambda t: (0, 0), pipeline_mode=pl.Buffered(1))],
        out_specs=[pl.BlockSpec((r, tb), lambda t: (0, t)), pl.BlockSpec((r, LANES), lambda t: (0, 0))],
        out_shape=[jax.ShapeDtypeStruct((r, p), F32), jax.ShapeDtypeStruct((r, LANES), F32)],
        scratch_shapes=[pltpu.VMEM((r, 1), F32)],
        compiler_params=pltpu.CompilerParams(
            dimension_semantics=("arbitrary",), vmem_limit_bytes=VMEM_LIMIT),
        name="cache_cumsum",
    )(lf_t, utri)


def _in_proj_kernel(x_ref, g_ref, wm_ref, wf_ref, bf_ref, qna_ref, kna_ref, qnb_ref, knb_ref,
                    bd_ref, ltri_ref, pc_ref, cinit_ref,
                    ka_ref, va_ref, logf_ref, kb_ref, vb_ref,
                    qaa_ref, kaa_ref, qba_ref, kba_ref, vab_ref, vbb_ref, sig_ref, carry_ref,
                    *, tm, seq, pos_off, key_major):
    t = pl.program_id(0)
    carry_mode = seq >= tm
    nb = 1 if carry_mode else tm // seq
    rows = tm if carry_mode else seq
    tpb = max(1, seq // tm)

    x = x_ref[...]
    h = (x * lax.rsqrt(jnp.mean(x * x, axis=-1, keepdims=True) + EPS) * g_ref[...]).astype(BF16)

    def proj(lo, hi):
        return jnp.dot(h, wm_ref[:, lo:hi], preferred_element_type=F32)

    def head_norm(z, w_ref):
        z2 = (z * z).astype(BF16)
        bw = bd_ref.shape[0]
        ss = jnp.concatenate([jnp.dot(z2[:, k:k + bw], bd_ref[...], preferred_element_type=F32)
                              for k in range(0, z.shape[1], bw)], axis=1)
        return z * lax.rsqrt(ss * (1.0 / DH_A) + EPS) * w_ref[...]

    def store_heads(ref, z, heads):
        w = z.shape[1] // heads
        for j in range(heads):
            ref[pl.ds(j, tm, stride=heads), :] = z[:, w * j:w * (j + 1)]

    def store_time_minor(ref, z):
        zts = [z[bi * rows:(bi + 1) * rows, :].T for bi in range(nb)]
        for bi, zt in enumerate(zts):
            ref[bi] = zt.reshape(z.shape[1] // SLOT_W, SLOT_W, rows)
        return zts

    def store_slots(ref, z):
        if seq >= LANES:
            return store_time_minor(ref, z)
        store_heads(ref, z, z.shape[1] // SLOT_W)
        return None

    def store_value_slabs(ref, vt):
        ones = jnp.ones((ONES_ROWS, tm), BF16)
        for g in range(W_A // LANES):
            ref[0, 0, g, 0:LANES, :] = vt[LANES * g:LANES * (g + 1), :].astype(BF16)
            ref[0, 0, g, LANES:LANES + ONES_ROWS, :] = ones

    zf = jnp.dot(h, wf_ref[...], preferred_element_type=F32) + bf_ref[...]
    lf = jnp.minimum(zf, 0.0) - jnp.log1p(jnp.exp(-jnp.abs(zf)))
    lane = _lane_iota(lf.shape)
    lf = jnp.where(lane < N_CUM_PIECES * HEAD_SLOTS, lf, 0.0)
    for bi in range(nb):
        logf_ref[bi] = lf[bi * rows:(bi + 1) * rows, :].T[:H_A, :]
    if carry_mode:
        @pl.when(t % tpb == 0)
        def _():
            carry_ref[...] = cinit_ref[0]
        base = carry_ref[...]
    else:
        base = cinit_ref[...]
    c, pieces = _cumsum_pieces(lf, ltri_ref, base)
    if carry_mode:
        carry_ref[...] = c[tm - 1:tm, :]

    row = t * tm + _row_iota((tm, SLOT_W))
    pos = (row & (seq - 1)) + pos_off

    qn = head_norm(proj(0, W_A), qna_ref) * (DH_A ** -0.5 * LOG2E)
    _emit_queries(qaa_ref, qn, N_CUM_PIECES, nb, rows, key_major)
    kn = head_norm(proj(W_A, 2 * W_A), kna_ref)
    store_slots(ka_ref, kn)
    _emit_fox_keys(kaa_ref, [_slot(kn, j) for j in range(HEAD_SLOTS)], pieces, pc_ref, nb, rows)
    v = proj(2 * W_A, 3 * W_A)
    vts = store_slots(va_ref, v)
    if key_major:
        store_value_slabs(vab_ref, vts[0])
    else:
        vab_ref[...] = v.astype(BF16)

    o = 3 * W_A
    qn = head_norm(proj(o, o + W_B), qnb_ref) * (DH_B ** -0.5 * LOG2E)
    _emit_queries(qba_ref, qn, N_POS_PIECES, nb, rows, key_major)
    kn = head_norm(proj(o + W_B, o + 2 * W_B), knb_ref)
    store_slots(kb_ref, kn)
    _emit_diff_keys(kba_ref, [_slot(kn, j) for j in range(HEAD_SLOTS)], pos, nb, rows)
    v = proj(o + 2 * W_B, o + 3 * W_B)
    store_heads(vb_ref, v, H_B)
    if key_major:
        store_value_slabs(vbb_ref, v.T)
    else:
        vbb_ref[...] = v.astype(BF16)

    o = 3 * W_A + 3 * W_B
    d = (wm_ref.shape[1] - o) // 2
    for k in range(2):
        sig_ref[:, k * d:(k + 1) * d] = jax.nn.sigmoid(proj(o + k * d, o + (k + 1) * d)).astype(BF16)


def _in_proj(x2, seq, pos_off, cinit, prm, tm, key_major):
    n, d = x2.shape
    carry_mode = seq >= tm
    nb = 1 if carry_mode else tm // seq
    tpb = max(1, seq // tm)
    bx = n // seq
    assert n % tm == 0 and seq & (seq - 1) == 0
    grid = (n // tm,)
    const = lambda a: pl.BlockSpec(a.shape, lambda t: (0,) * a.ndim, pipeline_mode=pl.Buffered(1))
    rowblk = lambda w, mult=1: pl.BlockSpec((tm * mult, w), lambda t: (t, 0))
    if carry_mode:
        cinit_spec = pl.BlockSpec((1, 1, LANES), lambda t: (t // tpb, 0, 0))
        aug_spec = pl.BlockSpec((1, HEAD_SLOTS, tm, LANES), lambda t: (t // tpb, 0, t % tpb, 0))
    else:
        cinit_spec = rowblk(LANES)
        aug_spec = pl.BlockSpec((nb, HEAD_SLOTS, seq, LANES), lambda t: (t, 0, 0, 0))
    aug_shape = jax.ShapeDtypeStruct((bx, HEAD_SLOTS, seq, LANES), BF16)
    if carry_mode:
        tmin_spec = pl.BlockSpec((1, HEAD_SLOTS, SLOT_W, tm), lambda t: (t // tpb, 0, 0, t % tpb))
        lf_spec = pl.BlockSpec((1, H_A, tm), lambda t: (t // tpb, 0, t % tpb))
    else:
        tmin_spec = pl.BlockSpec((nb, HEAD_SLOTS, SLOT_W, seq), lambda t: (t, 0, 0, 0))
        lf_spec = pl.BlockSpec((nb, H_A, seq), lambda t: (t, 0, 0))
    tmin_shape = jax.ShapeDtypeStruct((bx, HEAD_SLOTS, SLOT_W, seq), F32)
    if seq < LANES:
        tmin_spec = rowblk(SLOT_W, HEAD_SLOTS)
        tmin_shape = jax.ShapeDtypeStruct((n * HEAD_SLOTS, SLOT_W), F32)
    if key_major:
        assert carry_mode
        q_spec = pl.BlockSpec((1, HEAD_SLOTS, LANES, tm), lambda t: (t // tpb, 0, 0, t % tpb))
        q_shape = jax.ShapeDtypeStruct((bx, HEAD_SLOTS, LANES, seq), BF16)
        slab = (W_A // LANES, LANES + ONES_ROWS, tm)
        v_spec = pl.BlockSpec((1, 1) + slab, lambda t: (t // tpb, t % tpb, 0, 0, 0))
        v_shape = jax.ShapeDtypeStruct((bx, tpb) + slab, BF16)
    else:
        q_spec, q_shape = aug_spec, aug_shape
        v_spec = rowblk(W_A)
        v_shape = jax.ShapeDtypeStruct((n, W_A), BF16)
    consts = [prm["g_attn"], prm["w_main"], prm["w_f"], prm["b_f"], prm["qn_a"], prm["kn_a"],
              prm["qn_b"], prm["kn_b"], prm["bd"], prm["ltri_in"], prm["pc"]]
    return pl.pallas_call(
        functools.partial(_in_proj_kernel, tm=tm, seq=seq, pos_off=pos_off, key_major=key_major),
        grid=grid,
        in_specs=[rowblk(d)] + [const(a) for a in consts] + [cinit_spec],
        out_specs=[tmin_spec, tmin_spec, lf_spec, tmin_spec, rowblk(DV_B, H_B),
                   q_spec, aug_spec, q_spec, aug_spec, v_spec, v_spec, rowblk(2 * d)],
        out_shape=[
            tmin_shape, tmin_shape, jax.ShapeDtypeStruct((bx, H_A, seq), F32),
            tmin_shape, jax.ShapeDtypeStruct((n * H_B, DV_B), F32),
            q_shape, aug_shape, q_shape, aug_shape, v_shape, v_shape,
            jax.ShapeDtypeStruct((n, 2 * d), BF16),
        ],
        scratch_shapes=[pltpu.VMEM((1, LANES), F32)],
        compiler_params=pltpu.CompilerParams(
            dimension_semantics=("arbitrary",), vmem_limit_bytes=VMEM_LIMIT),
        name="in_proj",
    )(x2, *consts, cinit)


_CONTRACT_LAST = (((1,), (1,)), ((), ()))


def _attend_pair(q_ref, k_ref, v_ref, diag_bias, emit, *, tq):
    half = tq // 2
    bias_top, bias_bot = diag_bias

    def scores(c, n, j):
        qt = q_ref[0, c, :, n * tq:(n + 1) * tq]
        k0 = j * tq
        if j < n:
            return [(jnp.dot(k_ref[0, c, k0 + kk:k0 + kk + half, :], qt, preferred_element_type=F32), kk, 0)
                    for kk in (0, half)]
        top = jnp.dot(k_ref[0, c, k0:k0 + half, :], qt, preferred_element_type=F32) + bias_top
        bot = jnp.dot(k_ref[0, c, k0 + half:k0 + tq, :], qt[:, half:], preferred_element_type=F32) + bias_bot
        return [(top, 0, 0), (bot, half, half)]

    def absorb(m, acc, piece, vt):
        s, key0, q0 = piece
        m_old, acc_old = m[:, q0:], acc[:, q0:]
        m_new = jnp.maximum(m_old, jnp.max(s, axis=0, keepdims=True))
        p = jnp.exp2(s - m_new).astype(BF16)
        pv = jnp.dot(vt[:, key0:key0 + s.shape[0]], p, preferred_element_type=F32)
        acc_new = jnp.exp2(m_old - m_new) * acc_old + pv
        if q0:
            m_new = jnp.concatenate([m[:, :q0], m_new], axis=1)
            acc_new = jnp.concatenate([acc[:, :q0], acc_new], axis=1)
        return m_new, acc_new

    work = [(n, j) for n in range(v_ref.shape[1]) for j in range(n + 1)]
    m, acc = [None, None], [None, None]

    def first(t):
        return [scores(c, *work[t]) for c in range(2)]

    def second(t, s):
        n, j = work[t]
        vt = v_ref[0, j, 0]
        for c in range(2):
            if j == 0:
                m[c] = jnp.full((1, tq), NEG_BIG, F32)
                acc[c] = jnp.zeros((LANES + ONES_ROWS, tq), F32)
            for piece in s[c]:
                m[c], acc[c] = absorb(m[c], acc[c], piece, vt)
        if j == n:
            emit(n, [(acc[c][:LANES] / acc[c][LANES:LANES + 1]).T for c in range(2)])

    _pipelined(len(work), first, second, lookahead=1)


def _history_scores(q, kt_past, bias_row, k_cur, diag_bias):
    s_past = jnp.dot(q[:, :SLOT_W], kt_past, preferred_element_type=F32) + bias_row
    s_cur = lax.dot_general(q, k_cur, _CONTRACT_LAST, preferred_element_type=F32) + diag_bias
    return s_past, s_cur


def _history_output(s_past, s_cur, v_past, v_cur, v_time_minor):
    m = jnp.maximum(jnp.max(s_past, axis=-1, keepdims=True), jnp.max(s_cur, axis=-1, keepdims=True))
    p_past = jnp.exp2(s_past - m)
    p_cur = jnp.exp2(s_cur - m)
    l = jnp.sum(p_past, axis=-1, keepdims=True) + jnp.sum(p_cur, axis=-1, keepdims=True)
    if v_time_minor:
        o = lax.dot_general(p_past.astype(BF16), v_past, _CONTRACT_LAST, preferred_element_type=F32)
    else:
        o = jnp.dot(p_past.astype(BF16), v_past, preferred_element_type=F32)
    return (o + jnp.dot(p_cur.astype(BF16), v_cur, preferred_element_type=F32)) / l


def _pipelined(n, first, second, lookahead=2):
    pending = [first(h) for h in range(min(lookahead, n))]
    outs = []
    for h in range(n):
        if h + lookahead < n:
            pending.append(first(h + lookahead))
        outs.append(second(h, pending[h]))
    return outs


def _diag_indices(nk, nq, k0, q0, key_major):
    shape = (nk, nq) if key_major else (nq, nk)
    rowi, coli = _row_iota(shape), _lane_iota(shape)
    return (coli + q0, rowi + k0) if key_major else (rowi + q0, coli + k0)


def _diag_pieces(tq):
    half = tq // 2
    return _diag_indices(half, tq, 0, 0, True), _diag_indices(half, half, half, half, True)


def _causal_bias(qi, ki):
    return jnp.where(ki <= qi, 0.0, NEG_BIG)


def _diff_diag_bias(qi, ki, slope):
    ahead = jnp.maximum(ki - qi, 0).astype(F32)
    visible = (ki // CHUNK) <= (qi // CHUNK)
    return jnp.where(visible, (-2.0 * LOG2E) * slope * ahead, NEG_BIG)


def _diff_finish(o1, o2, lam_ref, sub_ref):
    lq1, lk1, lq2, lk2 = (lam_ref[k:k + 1, :] for k in range(4))
    lam = (jnp.exp(jnp.sum(lq1 * lk1, axis=-1, keepdims=True))
           - jnp.exp(jnp.sum(lq2 * lk2, axis=-1, keepdims=True)) + LAMBDA_INIT)
    o = o1 - lam * o2
    o = o * lax.rsqrt(jnp.mean(o * o, axis=-1, keepdims=True) + EPS) * sub_ref[...] * (1.0 - LAMBDA_INIT)
    return o.astype(BF16)


def _fox_kernel(q_ref, k_ref, v_ref, o_ref, *, tq):
    bias = tuple(_causal_bias(qi, ki) for qi, ki in _diag_pieces(tq))
    lane = _lane_iota((tq, LANES))

    def emit(n, outs):
        o_ref[0, n * tq:(n + 1) * tq, :] = jnp.where(lane < SLOT_W, outs[0], outs[1]).astype(BF16)

    _attend_pair(q_ref, k_ref, v_ref, bias, emit, tq=tq)


def _fox_history_kernel(q_ref, kc_ref, vc_ref, kp_ref, vp_ref, c_ref, o_ref, *, ts):
    causal = _causal_bias(*_diag_indices(ts, ts, 0, 0, False))
    lane = _lane_iota((ts, LANES))

    def scores(i):
        b, h = divmod(i, H_A)
        return _history_scores(q_ref[b, h], kp_ref[b, h].astype(BF16), c_ref[b, h:h + 1, :] * -LOG2E,
                               kc_ref[b, h], causal)

    pair_values = {}

    def output(i, s):
        b, h = divmod(i, H_A)
        g = h // 2
        if (b, g) not in pair_values:
            pair_values[b, g] = jnp.concatenate([vp_ref[b, 2 * g], vp_ref[b, 2 * g + 1]], axis=0).astype(BF16)
        return _history_output(*s, pair_values[b, g], vc_ref[b, :, LANES * g:LANES * (g + 1)], True)

    streams = q_ref.shape[0]
    outs = _pipelined(streams * H_A, scores, output)
    for b in range(streams):
        for g in range(H_A // 2):
            o_ref[b, :, LANES * g:LANES * (g + 1)] = jnp.where(
                lane < SLOT_W, outs[b * H_A + 2 * g], outs[b * H_A + 2 * g + 1]).astype(BF16)


def _diff_kernel(q_ref, k_ref, v_ref, lam_ref, sub_ref, o_ref, *, tq):
    hd = pl.program_id(1)
    slope = jnp.float32(ALIBI_SLOPES[0])
    for k in range(1, H_B):
        slope = jnp.where(hd == k, jnp.float32(ALIBI_SLOPES[k]), slope)
    bias = tuple(_diff_diag_bias(qi, ki, slope) for qi, ki in _diag_pieces(tq))

    def emit(n, outs):
        o_ref[0, n * tq:(n + 1) * tq, :] = _diff_finish(outs[0], outs[1], lam_ref, sub_ref)

    _attend_pair(q_ref, k_ref, v_ref, bias, emit, tq=tq)


def _diff_history_kernel(q_ref, kc_ref, vc_ref, kp_ref, vp_ref, lam_ref, sub_ref, o_ref, *, ts):
    plen = kp_ref.shape[3]
    pos = _lane_iota((1, plen)).astype(F32)
    qi, ki = _diag_indices(ts, ts, 0, 0, False)

    def scores(i):
        b, j = divmod(i, HEAD_SLOTS)
        slope = ALIBI_SLOPES[j // 2]
        return _history_scores(q_ref[b, j], kp_ref[b, j].astype(BF16), pos * (slope * LOG2E),
                               kc_ref[b, j], _diff_diag_bias(qi, ki, slope))

    head_values = {}

    def output(i, s):
        b, j = divmod(i, HEAD_SLOTS)
        h = j // 2
        if (b, h) not in head_values:
            head_values[b, h] = vp_ref[b, pl.ds(h, plen, stride=H_B), :].astype(BF16)
        return _history_output(*s, head_values[b, h], vc_ref[b, :, DV_B * h:DV_B * (h + 1)], False)

    streams = q_ref.shape[0]
    outs = _pipelined(streams * HEAD_SLOTS, scores, output)
    for b in range(streams):
        for h in range(H_B):
            o_ref[b, :, DV_B * h:DV_B * (h + 1)] = _diff_finish(
                outs[b * HEAD_SLOTS + 2 * h], outs[b * HEAD_SLOTS + 2 * h + 1], lam_ref, sub_ref)


def _attention(kind, q_aug, k_aug, v_cur, extras, tq):
    bx, _, seq, _ = k_aug.shape
    groups = HEAD_SLOTS // 2
    nblk = seq // tq
    slab_rows = LANES + ONES_ROWS
    assert v_cur.shape == (bx, nblk, groups, slab_rows, tq)
    grid = (bx, groups)
    in_specs = [
        pl.BlockSpec((1, 2, LANES, seq), lambda b, g: (b, g, 0, 0)),
        pl.BlockSpec((1, 2, seq, LANES), lambda b, g: (b, g, 0, 0)),
        pl.BlockSpec((1, nblk, 1, slab_rows, tq), lambda b, g: (b, 0, g, 0, 0)),
    ] + [pl.BlockSpec(a.shape, lambda b, g: (0,) * a.ndim) for a in extras]
    body = _fox_kernel if kind == "fox" else _diff_kernel
    return pl.pallas_call(
        functools.partial(body, tq=tq),
        grid=grid,
        in_specs=in_specs,
        out_specs=pl.BlockSpec((1, seq, LANES), lambda b, g: (b, 0, g)),
        out_shape=jax.ShapeDtypeStruct((bx, seq, groups * LANES), BF16),
        compiler_params=pltpu.CompilerParams(
            dimension_semantics=("arbitrary", "arbitrary"), vmem_limit_bytes=VMEM_LIMIT),
        name=kind + "_attn",
    )(q_aug, k_aug, v_cur, *extras)


def _history_attention(kind, q_aug, k_aug, v_cur, kt_past, v_past, extras):
    bx, _, ts, _ = q_aug.shape
    sps = HISTORY_STREAMS if bx % HISTORY_STREAMS == 0 else 1
    whole = lambda a: pl.BlockSpec((sps,) + a.shape[1:], lambda b: (b,) + (0,) * (a.ndim - 1))
    shared = lambda a: pl.BlockSpec(a.shape, lambda b: (0,) * a.ndim)
    args = [q_aug, k_aug, v_cur, kt_past, v_past]
    in_specs = [whole(a) for a in args] + [whole(a) if kind == "fox" else shared(a) for a in extras]
    body = _fox_history_kernel if kind == "fox" else _diff_history_kernel
    return pl.pallas_call(
        functools.partial(body, ts=ts),
        grid=(bx // sps,),
        in_specs=in_specs,
        out_specs=pl.BlockSpec((sps, ts, v_cur.shape[2]), lambda b: (b, 0, 0)),
        out_shape=jax.ShapeDtypeStruct((bx, ts, v_cur.shape[2]), BF16),
        compiler_params=pltpu.CompilerParams(
            dimension_semantics=("arbitrary",), vmem_limit_bytes=VMEM_LIMIT),
        name=kind + "_history_attn",
    )(*args, *extras)


def _merge_ffn_kernel(x_ref, oa_ref, ob_ref, sig_ref, woa_ref, wob_ref, wout_ref, gffn_ref,
                      wup_ref, cw_ref, cb_ref, wdown_ref, st_ref,
                      y_ref, ns_ref, carry_ref, ubuf_ref, act_ref, *, tm, seq, d_ff):
    t = pl.program_id(0)
    carry_mode = seq >= tm
    nb = 1 if carry_mode else tm // seq
    rows = tm if carry_mode else seq
    tpb = max(1, seq // tm)
    d = x_ref.shape[1]
    seg = rows + SUBLANES

    if carry_mode:
        @pl.when(t % tpb == 0)
        def _():
            carry_ref[0:2, :] = st_ref[0]

    ya = jnp.dot(oa_ref[...], woa_ref[...], preferred_element_type=F32)
    yb = jnp.dot(ob_ref[...], wob_ref[...], preferred_element_type=F32)
    m = sig_ref[:, :d].astype(F32) * ya + sig_ref[:, d:].astype(F32) * yb
    x1 = x_ref[...] + jnp.dot(m.astype(BF16), wout_ref[...], preferred_element_type=F32)
    h = (x1 * lax.rsqrt(jnp.mean(x1 * x1, axis=-1, keepdims=True) + EPS) * gffn_ref[...]).astype(BF16)

    def up(c0):
        return (jnp.dot(h, wup_ref[:, c0:c0 + FF_CHUNK], preferred_element_type=F32),
                jnp.dot(h, wup_ref[:, d_ff + c0:d_ff + c0 + FF_CHUNK], preferred_element_type=F32))

    def conv(u, c0, buf):
        cols = slice(c0, c0 + FF_CHUNK)
        w0, w1, w2 = (cw_ref[k:k + 1, cols] for k in range(3))
        outs = []
        for bi in range(nb):
            useg = u[bi * rows:(bi + 1) * rows, :]
            r0 = bi * seg + SUBLANES
            hist = carry_ref[0:2, cols] if carry_mode else st_ref[bi, :, cols]
            ubuf_ref[buf, r0 - 2:r0, :] = hist
            ubuf_ref[buf, r0:r0 + rows, :] = useg
            ns_ref[bi, :, cols] = useg[rows - 2:rows, :]
            u1 = ubuf_ref[buf, r0 - 1:r0 - 1 + rows, :]
            u2 = ubuf_ref[buf, r0 - 2:r0 - 2 + rows, :]
            outs.append(w0 * u2 + w1 * u1 + w2 * useg + cb_ref[:, cols])
        if carry_mode:
            carry_ref[0:2, cols] = u[tm - 2:tm, :]
        return outs[0] if nb == 1 else jnp.concatenate(outs, axis=0)

    chunks = list(range(0, d_ff, FF_CHUNK))
    split = chunks[len(chunks) // 2]
    nxt = up(chunks[0])
    for k, c0 in enumerate(chunks):
        ua, ub = nxt
        if k + 1 < len(chunks):
            nxt = up(chunks[k + 1])
        if c0 == split:
            y_half = x1 + jnp.dot(act_ref[:, :split], wdown_ref[:split, :], preferred_element_type=F32)
        a = conv(ua, c0, 2 * (k % 2))
        b = conv(ub, d_ff + c0, 2 * (k % 2) + 1)
        act_ref[:, c0:c0 + FF_CHUNK] = (a * jax.nn.sigmoid(a) * b).astype(BF16)
    y_ref[...] = y_half + jnp.dot(act_ref[:, split:], wdown_ref[split:, :], preferred_element_type=F32)


def _merge_ffn(x2, oa, ob, sig, state, seq, prm, tm):
    n, d = x2.shape
    d_ff = prm["w_down"].shape[0]
    assert d_ff % FF_CHUNK == 0 and seq >= 2
    carry_mode = seq >= tm
    nb = 1 if carry_mode else tm // seq
    rows = tm if carry_mode else seq
    tpb = max(1, seq // tm)
    bx = n // seq
    grid = (n // tm,)
    const = lambda a: pl.BlockSpec(a.shape, lambda t: (0,) * a.ndim, pipeline_mode=pl.Buffered(1))
    rowblk = lambda w: pl.BlockSpec((tm, w), lambda t: (t, 0))
    if carry_mode:
        st_spec = pl.BlockSpec((1, 2, 2 * d_ff), lambda t: (t // tpb, 0, 0))
    else:
        st_spec = pl.BlockSpec((nb, 2, 2 * d_ff), lambda t: (t, 0, 0))
    consts1 = [prm["w_oa"], prm["w_ob"], prm["w_out"], prm["g_ffn"], prm["w_up"], prm["conv_w"],
               prm["conv_b"], prm["w_down"]]
    return pl.pallas_call(
        functools.partial(_merge_ffn_kernel, tm=tm, seq=seq, d_ff=d_ff),
        grid=grid,
        in_specs=[rowblk(d), rowblk(W_A), rowblk(W_B), rowblk(2 * d)] + [const(a) for a in consts1] + [st_spec],
        out_specs=[rowblk(d), st_spec],
        out_shape=[jax.ShapeDtypeStruct((n, d), F32), jax.ShapeDtypeStruct((bx, 2, 2 * d_ff), F32)],
        scratch_shapes=[pltpu.VMEM((SUBLANES, 2 * d_ff), F32),
                        pltpu.VMEM((4, nb * (rows + SUBLANES), FF_CHUNK), F32),
                        pltpu.VMEM((tm, d_ff), BF16)],
        compiler_params=pltpu.CompilerParams(
            dimension_semantics=("arbitrary",), vmem_limit_bytes=VMEM_LIMIT),
        name="merge_ffn",
    )(x2, oa, ob, sig, *consts1, state)


def _tri(tm, seg):
    i = np.arange(tm)
    return jnp.asarray((i[None, :] <= i[:, None]) & (i[None, :] // seg == i[:, None] // seg), BF16)


def _piece_placer():
    p = np.zeros((LANES, HEAD_SLOTS * SLOT_W), np.float32)
    for k in range(N_CUM_PIECES):
        for h in range(HEAD_SLOTS):
            p[k * HEAD_SLOTS + h, h * SLOT_W + k] = -1.0
    return jnp.asarray(p, BF16)


def _block_diag_ones(width, blk):
    i = np.arange(width)
    return jnp.asarray(i[:, None] // blk == i[None, :] // blk, BF16)


def _rep3(a):
    pad = jnp.zeros(a.shape[:-1] + (LANES - N_CUM_PIECES * HEAD_SLOTS,), a.dtype)
    return jnp.concatenate([a] * N_CUM_PIECES + [pad], axis=-1)


def _tiles(seq):
    return dict(tm=512, tq=min(512, seq))


def _time_minor(a):
    return jnp.moveaxis(a, 1, -1)


def _time_major(a):
    return jnp.moveaxis(a, -1, 1)


def _layer(x, pos_off, cinit, past, state, prm, lam_pack, tiles):
    bx, seq, d = x.shape
    tm, tq = tiles["tm"], tiles["tq"]
    x2 = x.reshape(bx * seq, d)
    ka, va, logf, kb, vb, qaa, kaa, qba, kba, vab, vbb, sig = _in_proj(
        x2, seq, pos_off, cinit, prm, tm, key_major=past is None)
    diff_prm = (lam_pack, prm["subln_b"])
    if past is None:
        assert tq == tm
        oa = _attention("fox", qaa, kaa, vab, (), tq)
        ob = _attention("diff", qba, kba, vbb, diff_prm, tq)
    else:
        vab = vab.reshape(bx, seq, W_A)
        vbb = vbb.reshape(bx, seq, W_B)
        oa = _history_attention("fox", qaa, kaa, vab, past["kt_a"], past["vt_a"], (past["c"],))
        ob = _history_attention("diff", qba, kba, vbb, past["kt_b"], past["v_b"], diff_prm)
    y, new_state = _merge_ffn(x2, oa.reshape(bx * seq, W_A), ob.reshape(bx * seq, W_B), sig, state, seq, prm, tm)
    if seq >= LANES:
        ka, va, kb = _time_major(ka), _time_major(va), _time_major(kb.reshape(bx, H_B, 2, DH_B, seq))
    return (y.reshape(bx, seq, d), ka.reshape(bx, seq, H_A, DH_A), va.reshape(bx, seq, H_A, DH_A),
            _time_major(logf), kb.reshape(bx, seq, H_B, 2, DH_B), vb.reshape(bx, seq, H_B, DV_B), new_state)


def kernel(x_prompt, x_sample, cache_a_k, cache_a_v, cache_a_logf, cache_b_k, cache_b_v, state_ffn_conv,
           g_attn, w_in, b_f, qn_a, kn_a, qn_b, kn_b, lambda_q1, lambda_k1, lambda_q2, lambda_k2,
           subln_b, w_oa, w_ob, w_out, g_ffn, w_up, conv_w, conv_b, w_down):
    bp, tp_, d = x_prompt.shape
    bs, ts, _ = x_sample.shape
    plen = cache_a_k.shape[1]
    d_ff = w_down.shape[0]

    f0, f1 = 3 * W_A, 3 * W_A + H_A
    prm = {
        "g_attn": g_attn.reshape(1, d),
        "w_main": jnp.concatenate([w_in[:, :f0], w_in[:, f1:]], axis=1).astype(BF16),
        "w_f": _rep3(w_in[:, f0:f1]).astype(BF16),
        "b_f": _rep3(b_f.reshape(1, H_A)),
        "qn_a": jnp.tile(qn_a, H_A).reshape(1, W_A), "kn_a": jnp.tile(kn_a, H_A).reshape(1, W_A),
        "qn_b": jnp.tile(qn_b, 2 * H_B).reshape(1, W_B), "kn_b": jnp.tile(kn_b, 2 * H_B).reshape(1, W_B),
        "bd": _block_diag_ones(2 * LANES, DH_A),
        "pc": _piece_placer(),
        "subln_b": subln_b.reshape(1, DV_B),
        "w_oa": w_oa.astype(BF16), "w_ob": w_ob.astype(BF16), "w_out": w_out.astype(BF16),
        "g_ffn": g_ffn.reshape(1, d), "w_up": w_up.astype(BF16), "conv_w": conv_w,
        "conv_b": conv_b.reshape(1, 2 * d_ff), "w_down": w_down.astype(BF16),
    }
    lam_pack = jnp.stack([lambda_q1, lambda_k1, lambda_q2, lambda_k2])

    tl = _tiles(tp_)
    prm_p = dict(prm, ltri_in=_tri(tl["tm"], min(tl["tm"], tp_)))
    zeros_c = jnp.zeros((bp, 1, LANES), F32)
    zeros_state = jnp.zeros((bp, 2, 2 * d_ff), F32)
    (y_p, ka_p, va_p, lf_p, kb_p, vb_p, st_p) = _layer(
        x_prompt, 0, zeros_c, None, zeros_state, prm_p, lam_pack, tl)

    tl = _tiles(ts)
    c_past, c_tot = _cache_cumsum(_time_minor(cache_a_logf).reshape(bs * H_A, plen), min(512, plen))
    past = {
        "kt_a": _time_minor(cache_a_k), "vt_a": _time_minor(cache_a_v), "c": c_past.reshape(bs, H_A, plen),
        "kt_b": _time_minor(cache_b_k).reshape(bs, HEAD_SLOTS, DH_B, plen),
        "v_b": cache_b_v.reshape(bs, plen * H_B, DV_B),
    }
    prm_s = dict(prm, ltri_in=_tri(tl["tm"], min(tl["tm"], ts)))
    cinit_rows = jnp.repeat(_rep3(c_tot[:, 0].reshape(bs, H_A)), ts, axis=0)
    (y_s, ka_s, va_s, lf_s, kb_s, vb_s, st_s) = _layer(
        x_sample, plen, cinit_rows, past, state_ffn_conv, prm_s, lam_pack, tl)

    return (y_p, y_s, ka_p, va_p, lf_p, kb_p, vb_p, st_p, ka_s, va_s, lf_s, kb_s, vb_s, st_s)
```

```python
import functools
import math

import jax
import jax.numpy as jnp
import numpy as np
from jax import lax
from jax.experimental import pallas as pl
from jax.experimental.pallas import tpu as pltpu

F32 = jnp.float32
BF16 = jnp.bfloat16

CHUNK = 64
H_A = 8
DH_A = 64
H_B = 4
DH_B = 64
DV_B = 128
W_A = H_A * DH_A
W_B = H_B * DV_B
EPS = 1e-6
LAMBDA_INIT = 0.8 - 0.6 * math.exp(-0.3 * 0)
ALIBI_SLOPES = tuple(2.0 ** (-8.0 * (i + 1) / H_B) for i in range(H_B))

LANES = 128
SUBLANES = 8
HEAD_SLOTS = 8
SLOT_W = 64
N_CUM_PIECES = 3
N_POS_PIECES = 3
NEG_BIG = -1e30
LOG2E = math.log2(math.e)
ONES_ROWS = 16
VMEM_LIMIT = 58 * 1024 * 1024
FF_CHUNK = 256
HISTORY_STREAMS = 2


def _lane_iota(shape):
    return lax.broadcasted_iota(jnp.int32, shape, len(shape) - 1)


def _row_iota(shape):
    return lax.broadcasted_iota(jnp.int32, shape, len(shape) - 2)


def _split3(c):
    hi = c.astype(BF16).astype(F32)
    r1 = c - hi
    lo = r1.astype(BF16).astype(F32)
    lo2 = (r1 - lo).astype(BF16).astype(F32)
    return hi, lo, lo2


def _cumsum_pieces(lf, ltri_ref, base):
    hi, lo, lo2 = _split3(lf)
    ltri = ltri_ref[...]
    c = (jnp.dot(ltri, hi.astype(BF16), preferred_element_type=F32)
         + jnp.dot(ltri, lo.astype(BF16), preferred_element_type=F32)
         + jnp.dot(ltri, lo2.astype(BF16), preferred_element_type=F32)) + base
    chi, clo, clo2 = _split3(c * LOG2E)
    lane = _lane_iota(c.shape)
    pieces = jnp.where(lane < HEAD_SLOTS, chi, jnp.where(lane < 2 * HEAD_SLOTS, clo, clo2))
    return c, pieces.astype(BF16)


def _slot(z, j):
    return z[:, SLOT_W * j:SLOT_W * (j + 1)]


def _pos_extra(pos, slope):
    hi, lo, lo2 = _split3(pos.astype(F32) * (slope * LOG2E))
    lane = _lane_iota(hi.shape)
    return jnp.where(lane == 0, hi, jnp.where(lane == 1, lo, jnp.where(lane == 2, lo2, 0.0)))


def _store_tile(ref, slot64, extra64, j, nb, rows, time_minor=False):
    tile = jnp.concatenate([slot64, extra64], axis=1)
    if time_minor:
        tile_t = tile.T.astype(BF16)
        for bi in range(nb):
            ref[bi, j, :, :] = tile_t[:, bi * rows:(bi + 1) * rows]
    else:
        tile = tile.astype(BF16)
        for bi in range(nb):
            ref[bi, j, :, :] = tile[bi * rows:(bi + 1) * rows, :]


def _emit_fox_keys(kaa_ref, slots, pieces, pc_ref, nb, rows):
    extra = jnp.dot(pieces, pc_ref[...], preferred_element_type=F32)
    for j in range(HEAD_SLOTS):
        _store_tile(kaa_ref, slots[j], _slot(extra, j), j, nb, rows)


def _emit_diff_keys(kba_ref, slots, pos, nb, rows):
    for h in range(H_B):
        extra = _pos_extra(pos, ALIBI_SLOPES[h])
        for m in range(2):
            _store_tile(kba_ref, slots[2 * h + m], extra, 2 * h + m, nb, rows)


def _emit_queries(q_ref, qn, n_ones, nb, rows, time_minor):
    lane = _lane_iota((qn.shape[0], SLOT_W))
    ones = jnp.where(lane < n_ones, 1.0, 0.0)
    for j in range(HEAD_SLOTS):
        _store_tile(q_ref, _slot(qn, j), ones, j, nb, rows, time_minor)


def _cache_cumsum_kernel(lf_ref, utri_ref, c_ref, tot_ref, carry_ref, *, tb):
    t = pl.program_id(0)

    @pl.when(t == 0)
    def _():
        carry_ref[...] = jnp.zeros_like(carry_ref)

    hi, lo, lo2 = _split3(lf_ref[...])
    u = utri_ref[...]
    c = (jnp.dot(hi.astype(BF16), u, preferred_element_type=F32)
         + jnp.dot(lo.astype(BF16), u, preferred_element_type=F32)
         + jnp.dot(lo2.astype(BF16), u, preferred_element_type=F32)) + carry_ref[...]
    c_ref[...] = c
    last = c[:, tb - 1:tb]
    carry_ref[...] = last
    tot_ref[...] = jnp.broadcast_to(last, tot_ref.shape)


def _cache_cumsum(lf_t, tb):
    r, p = lf_t.shape
    i = np.arange(tb)
    utri = jnp.asarray(i[:, None] <= i[None, :], BF16)
    return pl.pallas_call(
        functools.partial(_cache_cumsum_kernel, tb=tb),
        grid=(p // tb,),
        in_specs=[pl.BlockSpec((r, tb), lambda t: (0, t)),
                  pl.BlockSpec((tb, tb), lambda t: (0, 0), pipeline_mode=pl.Buffered(1))],
        out_specs=[pl.BlockSpec((r, tb), lambda t: (0, t)), pl.BlockSpec((r, LANES), lambda t: (0, 0))],
        out_shape=[jax.ShapeDtypeStruct((r, p), F32), jax.ShapeDtypeStruct((r, LANES), F32)],
        scratch_shapes=[pltpu.VMEM((r, 1), F32)],
        compiler_params=pltpu.CompilerParams(
            dimension_semantics=("arbitrary",), vmem_limit_bytes=VMEM_LIMIT),
        name="cache_cumsum",
    )(lf_t, utri)


def _in_proj_kernel(x_ref, g_ref, wm_ref, wf_ref, bf_ref, qna_ref, kna_ref, qnb_ref, knb_ref,
                    bd_ref, ltri_ref, pc_ref, cinit_ref,
                    ka_ref, va_ref, logf_ref, kb_ref, vb_ref,
                    qaa_ref, kaa_ref, qba_ref, kba_ref, vab_ref, vbb_ref, sig_ref, carry_ref,
                    *, tm, seq, pos_off, key_major):
    t = pl.program_id(0)
    carry_mode = seq >= tm
    nb = 1 if carry_mode else tm // seq
    rows = tm if carry_mode else seq
    tpb = max(1, seq // tm)

    x = x_ref[...]
    h = (x * lax.rsqrt(jnp.mean(x * x, axis=-1, keepdims=True) + EPS) * g_ref[...]).astype(BF16)

    def proj(lo, hi):
        return jnp.dot(h, wm_ref[:, lo:hi], preferred_element_type=F32)

    def head_norm(z, w_ref):
        z2 = (z * z).astype(BF16)
        bw = bd_ref.shape[0]
        ss = jnp.concatenate([jnp.dot(z2[:, k:k + bw], bd_ref[...], preferred_element_type=F32)
                              for k in range(0, z.shape[1], bw)], axis=1)
        return z * lax.rsqrt(ss * (1.0 / DH_A) + EPS) * w_ref[...]

    def store_heads(ref, z, heads):
        w = z.shape[1] // heads
        for j in range(heads):
            ref[pl.ds(j, tm, stride=heads), :] = z[:, w * j:w * (j + 1)]

    def store_time_minor(ref, z):
        zts = [z[bi * rows:(bi + 1) * rows, :].T for bi in range(nb)]
        for bi, zt in enumerate(zts):
            ref[bi] = zt.reshape(z.shape[1] // SLOT_W, SLOT_W, rows)
        return zts

    def store_slots(ref, z):
        if seq >= LANES:
            return store_time_minor(ref, z)
        store_heads(ref, z, z.shape[1] // SLOT_W)
        return None

    def store_value_slabs(ref, vt):
        ones = jnp.ones((ONES_ROWS, tm), BF16)
        for g in range(W_A // LANES):
            ref[0, 0, g, 0:LANES, :] = vt[LANES * g:LANES * (g + 1), :].astype(BF16)
            ref[0, 0, g, LANES:LANES + ONES_ROWS, :] = ones

    zf = jnp.dot(h, wf_ref[...], preferred_element_type=F32) + bf_ref[...]
    lf = jnp.minimum(zf, 0.0) - jnp.log1p(jnp.exp(-jnp.abs(zf)))
    lane = _lane_iota(lf.shape)
    lf = jnp.where(lane < N_CUM_PIECES * HEAD_SLOTS, lf, 0.0)
    for bi in range(nb):
        logf_ref[bi] = lf[bi * rows:(bi + 1) * rows, :].T[:H_A, :]
    if carry_mode:
        @pl.when(t % tpb == 0)
        def _():
            carry_ref[...] = cinit_ref[0]
        base = carry_ref[...]
    else:
        base = cinit_ref[...]
    c, pieces = _cumsum_pieces(lf, ltri_ref, base)
    if carry_mode:
        carry_ref[...] = c[tm - 1:tm, :]

    row = t * tm + _row_iota((tm, SLOT_W))
    pos = (row & (seq - 1)) + pos_off

    qn = head_norm(proj(0, W_A), qna_ref) * (DH_A ** -0.5 * LOG2E)
    _emit_queries(qaa_ref, qn, N_CUM_PIECES, nb, rows, key_major)
    kn = head_norm(proj(W_A, 2 * W_A), kna_ref)
    store_slots(ka_ref, kn)
    _emit_fox_keys(kaa_ref, [_slot(kn, j) for j in range(HEAD_SLOTS)], pieces, pc_ref, nb, rows)
    v = proj(2 * W_A, 3 * W_A)
    vts = store_slots(va_ref, v)
    if key_major:
        store_value_slabs(vab_ref, vts[0])
    else:
        vab_ref[...] = v.astype(BF16)

    o = 3 * W_A
    qn = head_norm(proj(o, o + W_B), qnb_ref) * (DH_B ** -0.5 * LOG2E)
    _emit_queries(qba_ref, qn, N_POS_PIECES, nb, rows, key_major)
    kn = head_norm(proj(o + W_B, o + 2 * W_B), knb_ref)
    store_slots(kb_ref, kn)
    _emit_diff_keys(kba_ref, [_slot(kn, j) for j in range(HEAD_SLOTS)], pos, nb, rows)
    v = proj(o + 2 * W_B, o + 3 * W_B)
    store_heads(vb_ref, v, H_B)
    if key_major:
        store_value_slabs(vbb_ref, v.T)
    else:
        vbb_ref[...] = v.astype(BF16)

    o = 3 * W_A + 3 * W_B
    d = (wm_ref.shape[1] - o) // 2
    for k in range(2):
        sig_ref[:, k * d:(k + 1) * d] = jax.nn.sigmoid(proj(o + k * d, o + (k + 1) * d)).astype(BF16)


def _in_proj(x2, seq, pos_off, cinit, prm, tm, key_major):
    n, d = x2.shape
    carry_mode = seq >= tm
    nb = 1 if carry_mode else tm // seq
    tpb = max(1, seq // tm)
    bx = n // seq
    assert n % tm == 0 and seq & (seq - 1) == 0
    grid = (n // tm,)
    const = lambda a: pl.BlockSpec(a.shape, lambda t: (0,) * a.ndim, pipeline_mode=pl.Buffered(1))
    rowblk = lambda w, mult=1: pl.BlockSpec((tm * mult, w), lambda t: (t, 0))
    if carry_mode:
        cinit_spec = pl.BlockSpec((1, 1, LANES), lambda t: (t // tpb, 0, 0))
        aug_spec = pl.BlockSpec((1, HEAD_SLOTS, tm, LANES), lambda t: (t // tpb, 0, t % tpb, 0))
    else:
        cinit_spec = rowblk(LANES)
        aug_spec = pl.BlockSpec((nb, HEAD_SLOTS, seq, LANES), lambda t: (t, 0, 0, 0))
    aug_shape = jax.ShapeDtypeStruct((bx, HEAD_SLOTS, seq, LANES), BF16)
    if carry_mode:
        tmin_spec = pl.BlockSpec((1, HEAD_SLOTS, SLOT_W, tm), lambda t: (t // tpb, 0, 0, t % tpb))
        lf_spec = pl.BlockSpec((1, H_A, tm), lambda t: (t // tpb, 0, t % tpb))
    else:
        tmin_spec = pl.BlockSpec((nb, HEAD_SLOTS, SLOT_W, seq), lambda t: (t, 0, 0, 0))
        lf_spec = pl.BlockSpec((nb, H_A, seq), lambda t: (t, 0, 0))
    tmin_shape = jax.ShapeDtypeStruct((bx, HEAD_SLOTS, SLOT_W, seq), F32)
    if seq < LANES:
        tmin_spec = rowblk(SLOT_W, HEAD_SLOTS)
        tmin_shape = jax.ShapeDtypeStruct((n * HEAD_SLOTS, SLOT_W), F32)
    if key_major:
        assert carry_mode
        q_spec = pl.BlockSpec((1, HEAD_SLOTS, LANES, tm), lambda t: (t // tpb, 0, 0, t % tpb))
        q_shape = jax.ShapeDtypeStruct((bx, HEAD_SLOTS, LANES, seq), BF16)
        slab = (W_A // LANES, LANES + ONES_ROWS, tm)
        v_spec = pl.BlockSpec((1, 1) + slab, lambda t: (t // tpb, t % tpb, 0, 0, 0))
        v_shape = jax.ShapeDtypeStruct((bx, tpb) + slab, BF16)
    else:
        q_spec, q_shape = aug_spec, aug_shape
        v_spec = rowblk(W_A)
        v_shape = jax.ShapeDtypeStruct((n, W_A), BF16)
    consts = [prm["g_attn"], prm["w_main"], prm["w_f"], prm["b_f"], prm["qn_a"], prm["kn_a"],
              prm["qn_b"], prm["kn_b"], prm["bd"], prm["ltri_in"], prm["pc"]]
    return pl.pallas_call(
        functools.partial(_in_proj_kernel, tm=tm, seq=seq, pos_off=pos_off, key_major=key_major),
        grid=grid,
        in_specs=[rowblk(d)] + [const(a) for a in consts] + [cinit_spec],
        out_specs=[tmin_spec, tmin_spec, lf_spec, tmin_spec, rowblk(DV_B, H_B),
                   q_spec, aug_spec, q_spec, aug_spec, v_spec, v_spec, rowblk(2 * d)],
        out_shape=[
            tmin_shape, tmin_shape, jax.ShapeDtypeStruct((bx, H_A, seq), F32),
            tmin_shape, jax.ShapeDtypeStruct((n * H_B, DV_B), F32),
            q_shape, aug_shape, q_shape, aug_shape, v_shape, v_shape,
            jax.ShapeDtypeStruct((n, 2 * d), BF16),
        ],
        scratch_shapes=[pltpu.VMEM((1, LANES), F32)],
        compiler_params=pltpu.CompilerParams(
            dimension_semantics=("arbitrary",), vmem_limit_bytes=VMEM_LIMIT),
        name="in_proj",
    )(x2, *consts, cinit)


_CONTRACT_LAST = (((1,), (1,)), ((), ()))


def _attend_pair(q_ref, k_ref, v_ref, diag_bias, emit, *, tq, lookahead):
    half = tq // 2
    bias_top, bias_bot = diag_bias

    def scores(c, n, j):
        qt = q_ref[0, c, :, n * tq:(n + 1) * tq]
        k0 = j * tq
        if j < n:
            return [(jnp.dot(k_ref[0, c, k0 + kk:k0 + kk + half, :], qt, preferred_element_type=F32), kk, 0)
                    for kk in (0, half)]
        top = jnp.dot(k_ref[0, c, k0:k0 + half, :], qt, preferred_element_type=F32) + bias_top
        bot = jnp.dot(k_ref[0, c, k0 + half:k0 + tq, :], qt[:, half:], preferred_element_type=F32) + bias_bot
        return [(top, 0, 0), (bot, half, half)]

    def absorb(m, acc, piece, vt):
        s, key0, q0 = piece
        m_old, acc_old = m[:, q0:], acc[:, q0:]
        m_new = jnp.maximum(m_old, jnp.max(s, axis=0, keepdims=True))
        p = jnp.exp2(s - m_new).astype(BF16)
        pv = jnp.dot(vt[:, key0:key0 + s.shape[0]], p, preferred_element_type=F32)
        acc_new = jnp.exp2(m_old - m_new) * acc_old + pv
        if q0:
            m_new = jnp.concatenate([m[:, :q0], m_new], axis=1)
            acc_new = jnp.concatenate([acc[:, :q0], acc_new], axis=1)
        return m_new, acc_new

    work = [(n, j) for n in range(v_ref.shape[1]) for j in range(n + 1)]
    m, acc = [None, None], [None, None]

    def first(t):
        return [scores(c, *work[t]) for c in range(2)]

    def second(t, s):
        n, j = work[t]
        vt = v_ref[0, j, 0]
        for c in range(2):
            if j == 0:
                m[c] = jnp.full((1, tq), NEG_BIG, F32)
                acc[c] = jnp.zeros((LANES + ONES_ROWS, tq), F32)
            for piece in s[c]:
                m[c], acc[c] = absorb(m[c], acc[c], piece, vt)
        if j == n:
            emit(n, [(acc[c][:LANES] / acc[c][LANES:LANES + 1]).T for c in range(2)])

    _pipelined(len(work), first, second, lookahead=lookahead)


def _history_scores(q, kt_past, bias_row, k_cur, diag_bias):
    s_past = jnp.dot(q[:, :SLOT_W], kt_past, preferred_element_type=F32) + bias_row
    s_cur = lax.dot_general(q, k_cur, _CONTRACT_LAST, preferred_element_type=F32) + diag_bias
    return s_past, s_cur


def _history_output(s_past, s_cur, v_past, v_cur, v_time_minor):
    m = jnp.maximum(jnp.max(s_past, axis=-1, keepdims=True), jnp.max(s_cur, axis=-1, keepdims=True))
    p_past = jnp.exp2(s_past - m)
    p_cur = jnp.exp2(s_cur - m)
    l = jnp.sum(p_past, axis=-1, keepdims=True) + jnp.sum(p_cur, axis=-1, keepdims=True)
    if v_time_minor:
        o = lax.dot_general(p_past.astype(BF16), v_past, _CONTRACT_LAST, preferred_element_type=F32)
    else:
        o = jnp.dot(p_past.astype(BF16), v_past, preferred_element_type=F32)
    return (o + jnp.dot(p_cur.astype(BF16), v_cur, preferred_element_type=F32)) / l


def _pipelined(n, first, second, lookahead=2):
    pending = [first(h) for h in range(min(lookahead, n))]
    outs = []
    for h in range(n):
        if h + lookahead < n:
            pending.append(first(h + lookahead))
        outs.append(second(h, pending[h]))
    return outs


def _diag_indices(nk, nq, k0, q0, key_major):
    shape = (nk, nq) if key_major else (nq, nk)
    rowi, coli = _row_iota(shape), _lane_iota(shape)
    return (coli + q0, rowi + k0) if key_major else (rowi + q0, coli + k0)


def _diag_pieces(tq):
    half = tq // 2
    return _diag_indices(half, tq, 0, 0, True), _diag_indices(half, half, half, half, True)


def _causal_bias(qi, ki):
    return jnp.where(ki <= qi, 0.0, NEG_BIG)


def _diff_diag_bias(qi, ki, slope):
    ahead = jnp.maximum(ki - qi, 0).astype(F32)
    visible = (ki // CHUNK) <= (qi // CHUNK)
    return jnp.where(visible, (-2.0 * LOG2E) * slope * ahead, NEG_BIG)


def _diff_finish(o1, o2, lam_ref, sub_ref):
    lq1, lk1, lq2, lk2 = (lam_ref[k:k + 1, :] for k in range(4))
    lam = (jnp.exp(jnp.sum(lq1 * lk1, axis=-1, keepdims=True))
           - jnp.exp(jnp.sum(lq2 * lk2, axis=-1, keepdims=True)) + LAMBDA_INIT)
    o = o1 - lam * o2
    o = o * lax.rsqrt(jnp.mean(o * o, axis=-1, keepdims=True) + EPS) * sub_ref[...] * (1.0 - LAMBDA_INIT)
    return o.astype(BF16)


def _fox_kernel(q_ref, k_ref, v_ref, o_ref, *, tq):
    bias = tuple(_causal_bias(qi, ki) for qi, ki in _diag_pieces(tq))
    lane = _lane_iota((tq, LANES))

    def emit(n, outs):
        o_ref[0, n * tq:(n + 1) * tq, :] = jnp.where(lane < SLOT_W, outs[0], outs[1]).astype(BF16)

    _attend_pair(q_ref, k_ref, v_ref, bias, emit, tq=tq, lookahead=1)


def _fox_history_kernel(q_ref, kc_ref, vc_ref, kp_ref, vp_ref, c_ref, o_ref, *, ts):
    causal = _causal_bias(*_diag_indices(ts, ts, 0, 0, False))
    lane = _lane_iota((ts, LANES))

    def scores(i):
        b, h = divmod(i, H_A)
        return _history_scores(q_ref[b, h], kp_ref[b, h].astype(BF16), c_ref[b, h:h + 1, :] * -LOG2E,
                               kc_ref[b, h], causal)

    pair_values = {}

    def output(i, s):
        b, h = divmod(i, H_A)
        g = h // 2
        if (b, g) not in pair_values:
            pair_values[b, g] = jnp.concatenate([vp_ref[b, 2 * g], vp_ref[b, 2 * g + 1]], axis=0).astype(BF16)
        return _history_output(*s, pair_values[b, g], vc_ref[b, :, LANES * g:LANES * (g + 1)], True)

    streams = q_ref.shape[0]
    outs = _pipelined(streams * H_A, scores, output)
    for b in range(streams):
        for g in range(H_A // 2):
            o_ref[b, :, LANES * g:LANES * (g + 1)] = jnp.where(
                lane < SLOT_W, outs[b * H_A + 2 * g], outs[b * H_A + 2 * g + 1]).astype(BF16)


def _diff_kernel(q_ref, k_ref, v_ref, lam_ref, sub_ref, o_ref, *, tq):
    hd = pl.program_id(1)
    slope = jnp.float32(ALIBI_SLOPES[0])
    for k in range(1, H_B):
        slope = jnp.where(hd == k, jnp.float32(ALIBI_SLOPES[k]), slope)
    bias = tuple(_diff_diag_bias(qi, ki, slope) for qi, ki in _diag_pieces(tq))

    def emit(n, outs):
        o_ref[0, n * tq:(n + 1) * tq, :] = _diff_finish(outs[0], outs[1], lam_ref, sub_ref)

    _attend_pair(q_ref, k_ref, v_ref, bias, emit, tq=tq, lookahead=2)


def _diff_history_kernel(q_ref, kc_ref, vc_ref, kp_ref, vp_ref, lam_ref, sub_ref, o_ref, *, ts):
    plen = kp_ref.shape[3]
    pos = _lane_iota((1, plen)).astype(F32)
    qi, ki = _diag_indices(ts, ts, 0, 0, False)

    def scores(i):
        b, j = divmod(i, HEAD_SLOTS)
        slope = ALIBI_SLOPES[j // 2]
        return _history_scores(q_ref[b, j], kp_ref[b, j].astype(BF16), pos * (slope * LOG2E),
                               kc_ref[b, j], _diff_diag_bias(qi, ki, slope))

    head_values = {}

    def output(i, s):
        b, j = divmod(i, HEAD_SLOTS)
        h = j // 2
        if (b, h) not in head_values:
            head_values[b, h] = vp_ref[b, pl.ds(h, plen, stride=H_B), :].astype(BF16)
        return _history_output(*s, head_values[b, h], vc_ref[b, :, DV_B * h:DV_B * (h + 1)], False)

    streams = q_ref.shape[0]
    outs = _pipelined(streams * HEAD_SLOTS, scores, output)
    for b in range(streams):
        for h in range(H_B):
            o_ref[b, :, DV_B * h:DV_B * (h + 1)] = _diff_finish(
                outs[b * HEAD_SLOTS + 2 * h], outs[b * HEAD_SLOTS + 2 * h + 1], lam_ref, sub_ref)


def _attention(kind, q_aug, k_aug, v_cur, extras, tq):
    bx, _, seq, _ = k_aug.shape
    groups = HEAD_SLOTS // 2
    nblk = seq // tq
    slab_rows = LANES + ONES_ROWS
    assert v_cur.shape == (bx, nblk, groups, slab_rows, tq)
    grid = (bx, groups)
    in_specs = [
        pl.BlockSpec((1, 2, LANES, seq), lambda b, g: (b, g, 0, 0)),
        pl.BlockSpec((1, 2, seq, LANES), lambda b, g: (b, g, 0, 0)),
        pl.BlockSpec((1, nblk, 1, slab_rows, tq), lambda b, g: (b, 0, g, 0, 0)),
    ] + [pl.BlockSpec(a.shape, lambda b, g: (0,) * a.ndim) for a in extras]
    body = _fox_kernel if kind == "fox" else _diff_kernel
    return pl.pallas_call(
        functools.partial(body, tq=tq),
        grid=grid,
        in_specs=in_specs,
        out_specs=pl.BlockSpec((1, seq, LANES), lambda b, g: (b, 0, g)),
        out_shape=jax.ShapeDtypeStruct((bx, seq, groups * LANES), BF16),
        compiler_params=pltpu.CompilerParams(
            dimension_semantics=("arbitrary", "arbitrary"), vmem_limit_bytes=VMEM_LIMIT),
        name=kind + "_attn",
    )(q_aug, k_aug, v_cur, *extras)


def _history_attention(kind, q_aug, k_aug, v_cur, kt_past, v_past, extras):
    bx, _, ts, _ = q_aug.shape
    sps = HISTORY_STREAMS if bx % HISTORY_STREAMS == 0 else 1
    whole = lambda a: pl.BlockSpec((sps,) + a.shape[1:], lambda b: (b,) + (0,) * (a.ndim - 1))
    shared = lambda a: pl.BlockSpec(a.shape, lambda b: (0,) * a.ndim)
    args = [q_aug, k_aug, v_cur, kt_past, v_past]
    in_specs = [whole(a) for a in args] + [whole(a) if kind == "fox" else shared(a) for a in extras]
    body = _fox_history_kernel if kind == "fox" else _diff_history_kernel
    return pl.pallas_call(
        functools.partial(body, ts=ts),
        grid=(bx // sps,),
        in_specs=in_specs,
        out_specs=pl.BlockSpec((sps, ts, v_cur.shape[2]), lambda b: (b, 0, 0)),
        out_shape=jax.ShapeDtypeStruct((bx, ts, v_cur.shape[2]), BF16),
        compiler_params=pltpu.CompilerParams(
            dimension_semantics=("arbitrary",), vmem_limit_bytes=VMEM_LIMIT),
        name=kind + "_history_attn",
    )(*args, *extras)


def _merge_ffn_kernel(x_ref, oa_ref, ob_ref, sig_ref, woa_ref, wob_ref, wout_ref, gffn_ref,
                      wup_ref, cw_ref, cb_ref, wdown_ref, st_ref,
                      y_ref, ns_ref, carry_ref, ubuf_ref, act_ref, *, tm, seq, d_ff):
    t = pl.program_id(0)
    carry_mode = seq >= tm
    nb = 1 if carry_mode else tm // seq
    rows = tm if carry_mode else seq
    tpb = max(1, seq // tm)
    d = x_ref.shape[1]
    seg = rows + SUBLANES

    if carry_mode:
        @pl.when(t % tpb == 0)
        def _():
            carry_ref[0:2, :] = st_ref[0]

    ya = jnp.dot(oa_ref[...], woa_ref[...], preferred_element_type=F32)
    yb = jnp.dot(ob_ref[...], wob_ref[...], preferred_element_type=F32)
    m = sig_ref[:, :d].astype(F32) * ya + sig_ref[:, d:].astype(F32) * yb
    x1 = x_ref[...] + jnp.dot(m.astype(BF16), wout_ref[...], preferred_element_type=F32)
    h = (x1 * lax.rsqrt(jnp.mean(x1 * x1, axis=-1, keepdims=True) + EPS) * gffn_ref[...]).astype(BF16)

    def up(c0):
        return (jnp.dot(h, wup_ref[:, c0:c0 + FF_CHUNK], preferred_element_type=F32),
                jnp.dot(h, wup_ref[:, d_ff + c0:d_ff + c0 + FF_CHUNK], preferred_element_type=F32))

    def conv(u, c0, buf):
        cols = slice(c0, c0 + FF_CHUNK)
        w0, w1, w2 = (cw_ref[k:k + 1, cols] for k in range(3))
        outs = []
        for bi in range(nb):
            useg = u[bi * rows:(bi + 1) * rows, :]
            r0 = bi * seg + SUBLANES
            hist = carry_ref[0:2, cols] if carry_mode else st_ref[bi, :, cols]
            ubuf_ref[buf, r0 - 2:r0, :] = hist
            ubuf_ref[buf, r0:r0 + rows, :] = useg
            ns_ref[bi, :, cols] = useg[rows - 2:rows, :]
            u1 = ubuf_ref[buf, r0 - 1:r0 - 1 + rows, :]
            u2 = ubuf_ref[buf, r0 - 2:r0 - 2 + rows, :]
            outs.append(w0 * u2 + w1 * u1 + w2 * useg + cb_ref[:, cols])
        if carry_mode:
            carry_ref[0:2, cols] = u[tm - 2:tm, :]
        return outs[0] if nb == 1 else jnp.concatenate(outs, axis=0)

    chunks = list(range(0, d_ff, FF_CHUNK))
    split = chunks[len(chunks) // 2]
    nxt = up(chunks[0])
    for k, c0 in enumerate(chunks):
        ua, ub = nxt
        if k + 1 < len(chunks):
            nxt = up(chunks[k + 1])
        if c0 == split:
            y_half = x1 + jnp.dot(act_ref[:, :split], wdown_ref[:split, :], preferred_element_type=F32)
        a = conv(ua, c0, 2 * (k % 2))
        b = conv(ub, d_ff + c0, 2 * (k % 2) + 1)
        act_ref[:, c0:c0 + FF_CHUNK] = (a * jax.nn.sigmoid(a) * b).astype(BF16)
    y_ref[...] = y_half + jnp.dot(act_ref[:, split:], wdown_ref[split:, :], preferred_element_type=F32)


def _merge_ffn(x2, oa, ob, sig, state, seq, prm, tm):
    n, d = x2.shape
    d_ff = prm["w_down"].shape[0]
    assert d_ff % FF_CHUNK == 0 and seq >= 2
    carry_mode = seq >= tm
    nb = 1 if carry_mode else tm // seq
    rows = tm if carry_mode else seq
    tpb = max(1, seq // tm)
    bx = n // seq
    grid = (n // tm,)
    const = lambda a: pl.BlockSpec(a.shape, lambda t: (0,) * a.ndim, pipeline_mode=pl.Buffered(1))
    rowblk = lambda w: pl.BlockSpec((tm, w), lambda t: (t, 0))
    if carry_mode:
        st_spec = pl.BlockSpec((1, 2, 2 * d_ff), lambda t: (t // tpb, 0, 0))
    else:
        st_spec = pl.BlockSpec((nb, 2, 2 * d_ff), lambda t: (t, 0, 0))
    consts1 = [prm["w_oa"], prm["w_ob"], prm["w_out"], prm["g_ffn"], prm["w_up"], prm["conv_w"],
               prm["conv_b"], prm["w_down"]]
    return pl.pallas_call(
        functools.partial(_merge_ffn_kernel, tm=tm, seq=seq, d_ff=d_ff),
        grid=grid,
        in_specs=[rowblk(d), rowblk(W_A), rowblk(W_B), rowblk(2 * d)] + [const(a) for a in consts1] + [st_spec],
        out_specs=[rowblk(d), st_spec],
        out_shape=[jax.ShapeDtypeStruct((n, d), F32), jax.ShapeDtypeStruct((bx, 2, 2 * d_ff), F32)],
        scratch_shapes=[pltpu.VMEM((SUBLANES, 2 * d_ff), F32),
                        pltpu.VMEM((4, nb * (rows + SUBLANES), FF_CHUNK), F32),
                        pltpu.VMEM((tm, d_ff), BF16)],
        compiler_params=pltpu.CompilerParams(
            dimension_semantics=("arbitrary",), vmem_limit_bytes=VMEM_LIMIT),
        name="merge_ffn",
    )(x2, oa, ob, sig, *consts1, state)


def _tri(tm, seg):
    i = np.arange(tm)
    return jnp.asarray((i[None, :] <= i[:, None]) & (i[None, :] // seg == i[:, None] // seg), BF16)


def _piece_placer():
    p = np.zeros((LANES, HEAD_SLOTS * SLOT_W), np.float32)
    for k in range(N_CUM_PIECES):
        for h in range(HEAD_SLOTS):
            p[k * HEAD_SLOTS + h, h * SLOT_W + k] = -1.0
    return jnp.asarray(p, BF16)


def _block_diag_ones(width, blk):
    i = np.arange(width)
    return jnp.asarray(i[:, None] // blk == i[None, :] // blk, BF16)


def _rep3(a):
    pad = jnp.zeros(a.shape[:-1] + (LANES - N_CUM_PIECES * HEAD_SLOTS,), a.dtype)
    return jnp.concatenate([a] * N_CUM_PIECES + [pad], axis=-1)


def _tiles(seq):
    return dict(tm=512, tq=min(512, seq))


def _time_minor(a):
    return jnp.moveaxis(a, 1, -1)


def _time_major(a):
    return jnp.moveaxis(a, -1, 1)


def _layer(x, pos_off, cinit, past, state, prm, lam_pack, tiles):
    bx, seq, d = x.shape
    tm, tq = tiles["tm"], tiles["tq"]
    x2 = x.reshape(bx * seq, d)
    ka, va, logf, kb, vb, qaa, kaa, qba, kba, vab, vbb, sig = _in_proj(
        x2, seq, pos_off, cinit, prm, tm, key_major=past is None)
    diff_prm = (lam_pack, prm["subln_b"])
    if past is None:
        assert tq == tm
        oa = _attention("fox", qaa, kaa, vab, (), tq)
        ob = _attention("diff", qba, kba, vbb, diff_prm, tq)
    else:
        vab = vab.reshape(bx, seq, W_A)
        vbb = vbb.reshape(bx, seq, W_B)
        oa = _history_attention("fox", qaa, kaa, vab, past["kt_a"], past["vt_a"], (past["c"],))
        ob = _history_attention("diff", qba, kba, vbb, past["kt_b"], past["v_b"], diff_prm)
    y, new_state = _merge_ffn(x2, oa.reshape(bx * seq, W_A), ob.reshape(bx * seq, W_B), sig, state, seq, prm, tm)
    if seq >= LANES:
        ka, va, kb = _time_major(ka), _time_major(va), _time_major(kb.reshape(bx, H_B, 2, DH_B, seq))
    return (y.reshape(bx, seq, d), ka.reshape(bx, seq, H_A, DH_A), va.reshape(bx, seq, H_A, DH_A),
            _time_major(logf), kb.reshape(bx, seq, H_B, 2, DH_B), vb.reshape(bx, seq, H_B, DV_B), new_state)


def kernel(x_prompt, x_sample, cache_a_k, cache_a_v, cache_a_logf, cache_b_k, cache_b_v, state_ffn_conv,
           g_attn, w_in, b_f, qn_a, kn_a, qn_b, kn_b, lambda_q1, lambda_k1, lambda_q2, lambda_k2,
           subln_b, w_oa, w_ob, w_out, g_ffn, w_up, conv_w, conv_b, w_down):
    bp, tp_, d = x_prompt.shape
    bs, ts, _ = x_sample.shape
    plen = cache_a_k.shape[1]
    d_ff = w_down.shape[0]

    f0, f1 = 3 * W_A, 3 * W_A + H_A
    prm = {
        "g_attn": g_attn.reshape(1, d),
        "w_main": jnp.concatenate([w_in[:, :f0], w_in[:, f1:]], axis=1).astype(BF16),
        "w_f": _rep3(w_in[:, f0:f1]).astype(BF16),
        "b_f": _rep3(b_f.reshape(1, H_A)),
        "qn_a": jnp.tile(qn_a, H_A).reshape(1, W_A), "kn_a": jnp.tile(kn_a, H_A).reshape(1, W_A),
        "qn_b": jnp.tile(qn_b, 2 * H_B).reshape(1, W_B), "kn_b": jnp.tile(kn_b, 2 * H_B).reshape(1, W_B),
        "bd": _block_diag_ones(2 * LANES, DH_A),
        "pc": _piece_placer(),
        "subln_b": subln_b.reshape(1, DV_B),
        "w_oa": w_oa.astype(BF16), "w_ob": w_ob.astype(BF16), "w_out": w_out.astype(BF16),
        "g_ffn": g_ffn.reshape(1, d), "w_up": w_up.astype(BF16), "conv_w": conv_w,
        "conv_b": conv_b.reshape(1, 2 * d_ff), "w_down": w_down.astype(BF16),
    }
    lam_pack = jnp.stack([lambda_q1, lambda_k1, lambda_q2, lambda_k2])

    tl = _tiles(tp_)
    prm_p = dict(prm, ltri_in=_tri(tl["tm"], min(tl["tm"], tp_)))
    zeros_c = jnp.zeros((bp, 1, LANES), F32)
    zeros_state = jnp.zeros((bp, 2, 2 * d_ff), F32)
    (y_p, ka_p, va_p, lf_p, kb_p, vb_p, st_p) = _layer(
        x_prompt, 0, zeros_c, None, zeros_state, prm_p, lam_pack, tl)

    tl = _tiles(ts)
    c_past, c_tot = _cache_cumsum(_time_minor(cache_a_logf).reshape(bs * H_A, plen), min(512, plen))
    past = {
        "kt_a": _time_minor(cache_a_k), "vt_a": _time_minor(cache_a_v), "c": c_past.reshape(bs, H_A, plen),
        "kt_b": _time_minor(cache_b_k).reshape(bs, HEAD_SLOTS, DH_B, plen),
        "v_b": cache_b_v.reshape(bs, plen * H_B, DV_B),
    }
    prm_s = dict(prm, ltri_in=_tri(tl["tm"], min(tl["tm"], ts)))
    cinit_rows = jnp.repeat(_rep3(c_tot[:, 0].reshape(bs, H_A)), ts, axis=0)
    (y_s, ka_s, va_s, lf_s, kb_s, vb_s, st_s) = _layer(
        x_sample, plen, cinit_rows, past, state_ffn_conv, prm_s, lam_pack, tl)

    return (y_p, y_s, ka_p, va_p, lf_p, kb_p, vb_p, st_p, ka_s, va_s, lf_s, kb_s, vb_s, st_s)
```
